```python
import math
import jax
import jax.numpy as jnp
from jax import lax
import numpy as np

D_MODEL = 1024
BATCH = 4
SEQ = 4096
DEPTH = 2

GRID_W = 64
CTX_LEN = 256
EPS = 1e-6
NEG_INF = -1e30

A_HEADS = 4
A_DK = 128
A_DV = 128
A_WIDTH = A_HEADS * A_DV
CONV_K = 5
CHUNK = 64
B_HEADS = 4
B_DH = 64
B_WIDTH = B_HEADS * 2 * B_DH
ROPE_BASE = 10000.0
Q_BLOCK = 128
C_HEADS = 8
C_DH = 64
C_WIDTH = C_HEADS * C_DH
WIN_R = 8
WIN_C = 16
N_BRANCH = 3
BRANCH_W = 512
P_HEADS = 8
N_KEYS = 128
N_EXPERTS = N_KEYS * N_KEYS
P_DKH = 128
P_TOPK = 16
P_BLOCK = 128

SPLIT_SIZES = (A_HEADS * A_DK, A_HEADS * A_DK, A_WIDTH, A_WIDTH, 2 * A_HEADS, 2 * A_HEADS,
               B_HEADS * 2 * B_DH, B_HEADS * 2 * B_DH, B_WIDTH,
               C_WIDTH, C_WIDTH, C_WIDTH,
               N_BRANCH * D_MODEL)
IN_COLS = sum(SPLIT_SIZES)

kernel_name = 'hybrid_diffusion_trunk'


def rmsnorm(x, g):
    xf = x.astype(jnp.float32)
    y = xf * lax.rsqrt(jnp.mean(xf * xf, axis=-1, keepdims=True) + EPS)
    return (y * g.astype(jnp.float32)).astype(x.dtype)


def l2norm(x):
    return x * lax.rsqrt(jnp.sum(x * x, axis=-1, keepdims=True) + EPS)


def modulate(h, shift, scale):
    return h * (1 + scale) + shift


def split_cols(t):
    return jnp.split(t, [int(s) for s in np.cumsum(SPLIT_SIZES)[:-1]], axis=-1)


def dwconv_centred(x, w):
    ch = x.shape[-1]
    pad = CONV_K // 2
    return lax.conv_general_dilated(x, w[:, None, :].astype(x.dtype), window_strides=(1,),
                                    padding=[(pad, pad)], dimension_numbers=('NWC', 'WIO', 'NWC'),
                                    feature_group_count=ch)


def rope_2d(x, row_pos, col_pos):
    def rot(t, pos):
        half = t.shape[-1] // 2
        inv = 1.0 / (ROPE_BASE ** (jnp.arange(half, dtype=jnp.float32) / half))
        ang = pos.astype(jnp.float32)[:, None] * inv
        cos, sin = jnp.cos(ang), jnp.sin(ang)
        t1, t2 = t[..., :half], t[..., half:]
        return jnp.concatenate([t1 * cos - t2 * sin, t1 * sin + t2 * cos], axis=-1)
    d_axis = x.shape[-1] // 2
    return jnp.concatenate([rot(x[..., :d_axis], row_pos), rot(x[..., d_axis:], col_pos)], axis=-1).astype(x.dtype)


def gated_delta_chunked(q, k, v, beta, g, s0):
    Bn, H, L, DK = q.shape
    DV = v.shape[-1]
    n = L // CHUNK
    rs = lambda t: t.reshape(Bn, H, n, CHUNK, *t.shape[3:])
    q, k, v, beta, g = rs(q), rs(k), rs(v), rs(beta), rs(g)
    g = jnp.cumsum(g, axis=-1)
    kb = k * beta[..., None]
    vb = v * beta[..., None]
    tri_incl = jnp.tril(jnp.ones((CHUNK, CHUNK), bool))
    tri_strict = jnp.tril(jnp.ones((CHUNK, CHUNK), bool), -1)
    decay_mat = jnp.exp(jnp.where(tri_incl, g[..., :, None] - g[..., None, :], -jnp.inf))
    a_strict = jnp.where(tri_strict, jnp.einsum('bhnid,bhnjd->bhnij', kb, k) * decay_mat, 0.0)
    rhs = jnp.concatenate([vb, kb * jnp.exp(g)[..., None]], axis=-1)
    sol = jax.lax.linalg.triangular_solve(a_strict, rhs, left_side=True, lower=True, unit_diagonal=True)
    u, w = sol[..., :DV], sol[..., DV:]
    qk = jnp.where(tri_incl, jnp.einsum('bhnid,bhnjd->bhnij', q, k) * decay_mat, 0.0)
    g_last = g[..., -1]
    k_dec = k * jnp.exp(g_last[..., None] - g)[..., None]
    q_dec = q * jnp.exp(g)[..., None]

    def step(S, xs):
        qc, qkc, uc, wc, kdc, glc = xs
        v_new = uc - jnp.einsum('bhcd,bhde->bhce', wc, S)
        o = jnp.einsum('bhcd,bhde->bhce', qc, S) + jnp.einsum('bhij,bhje->bhie', qkc, v_new)
        S = S * jnp.exp(glc)[..., None, None] + jnp.einsum('bhcd,bhce->bhde', kdc, v_new)
        return S, o

    xs = tuple(jnp.moveaxis(t, 2, 0) for t in (q_dec, qk, u, w, k_dec, g_last))
    s_final, o = lax.scan(step, s0, xs)
    return jnp.moveaxis(o, 0, 2).reshape(Bn, H, L, DV), s_final


def gdn_prep(parts, conv_w, a_log, dt_bias):
    q, k, v, _, b, a = parts
    Bn, L, _ = q.shape
    qkv = jax.nn.silu(dwconv_centred(jnp.concatenate([q, k, v], axis=-1), conv_w)).astype(jnp.float32)
    q, k, v = jnp.split(qkv, 3, axis=-1)
    heads = lambda t, d: t.reshape(Bn, L, A_HEADS, d).transpose(0, 2, 1, 3)
    q = l2norm(heads(q, A_DK)) * (A_DK ** -0.5)
    k = l2norm(heads(k, A_DK))
    v = heads(v, A_DV)
    dirs = lambda t: t.astype(jnp.float32).reshape(Bn, L, 2, A_HEADS).transpose(2, 0, 3, 1)
    beta = jax.nn.sigmoid(dirs(b))
    g = -jnp.exp(a_log.astype(jnp.float32))[:, None, :, None] * jax.nn.softplus(
        dirs(a) + dt_bias.astype(jnp.float32)[:, None, :, None])
    return q, k, v, beta, g


def gdn_out(o, z, norm_g):
    Bn, H, L, DV = o.shape
    o = rmsnorm(o.transpose(0, 2, 1, 3), norm_g)
    z = z.reshape(Bn, L, H, DV).astype(jnp.float32)
    return (o * jax.nn.silu(z)).reshape(Bn, L, H * DV).astype(z.dtype if False else o.dtype)


def gdn_mixer(lat, ctx, conv_w, a_log, dt_bias, norm_g, ctx_out):
    ql, kl, vl, bl, gl = gdn_prep(lat, conv_w, a_log, dt_bias)
    qc, kc, vc, bc, gc = gdn_prep(ctx, conv_w, a_log, dt_bias)
    flip = lambda t: jnp.flip(t, axis=2)
    s0 = jnp.zeros((ql.shape[0], A_HEADS, A_DK, A_DV), jnp.float32)
    oc_f, sc_f = gated_delta_chunked(qc, kc, vc, bc[0], gc[0], s0)
    oc_b, sc_b = gated_delta_chunked(flip(qc), flip(kc), flip(vc), flip(bc[1]), flip(gc[1]), s0)
    ol_f, _ = gated_delta_chunked(ql, kl, vl, bl[0], gl[0], sc_f)
    ol_b, _ = gated_delta_chunked(flip(ql), flip(kl), flip(vl), flip(bl[1]), flip(gl[1]), sc_b)
    y_lat = gdn_out(ol_f + flip(ol_b), lat[3], norm_g).astype(lat[3].dtype)
    y_ctx = gdn_out(oc_f + flip(oc_b), ctx[3], norm_g).astype(ctx[3].dtype) if ctx_out else None
    return y_lat, y_ctx


def diff_attend(q, k, v, lam):
    s = jnp.einsum('bhmqd,bhmkd->bhmqk', q, k).astype(jnp.float32) * (B_DH ** -0.5)
    p = jax.nn.softmax(s, axis=-1)
    attn = p[:, :, 0] - lam * p[:, :, 1]
    return jnp.einsum('bhqk,bhkd->bhqd', attn.astype(v.dtype), v)


def diff_mixer(lat, ctx, lam_p, subln_g, lam_init, row_pos, col_pos, ctx_out):
    def heads_qk(t):
        return t.reshape(t.shape[0], t.shape[1], B_HEADS, 2, B_DH).transpose(0, 2, 3, 1, 4)

    def heads_v(t):
        return t.reshape(t.shape[0], t.shape[1], B_HEADS, 2 * B_DH).transpose(0, 2, 1, 3)

    ql = rope_2d(heads_qk(lat[0]), row_pos, col_pos)
    kl = rope_2d(heads_qk(lat[1]), row_pos, col_pos)
    vl = heads_v(lat[2])
    qc, kc, vc = heads_qk(ctx[0]), heads_qk(ctx[1]), heads_v(ctx[2])
    lp = lam_p.astype(jnp.float32)
    lam = jnp.exp(jnp.sum(lp[0] * lp[1])) - jnp.exp(jnp.sum(lp[2] * lp[3])) + lam_init
    k_all = jnp.concatenate([kl, kc], axis=3)
    v_all = jnp.concatenate([vl, vc], axis=2)
    Bn, H, _, L, Dh = ql.shape
    nb = L // Q_BLOCK
    qb = jnp.moveaxis(ql.reshape(Bn, H, 2, nb, Q_BLOCK, Dh), 3, 0)
    ol = lax.map(lambda qblk: diff_attend(qblk, k_all, v_all, lam), qb)
    ol = jnp.moveaxis(ol, 0, 2).reshape(Bn, H, L, 2 * Dh)

    def out(o):
        o = rmsnorm(o, subln_g) * (1 - lam_init)
        return o.transpose(0, 2, 1, 3).reshape(o.shape[0], o.shape[2], B_HEADS * 2 * B_DH)

    y_ctx = out(diff_attend(qc, kc, vc, lam)) if ctx_out else None
    return out(ol), y_ctx


def dense_attention(q, k, v):
    s = jnp.einsum('bhqd,bhkd->bhqk', q, k).astype(jnp.float32) * (q.shape[-1] ** -0.5)
    return jnp.einsum('bhqk,bhkd->bhqd', jax.nn.softmax(s, axis=-1).astype(v.dtype), v)


def neighbourhood_attention(q, k, v, kc, vc, rpb):
    Bn, H, L, Dh = q.shape
    rows = L // GRID_W
    wr = min(WIN_R, rows)
    grid = lambda t: t.reshape(Bn, H, rows, GRID_W, Dh)
    qg, kg, vg = grid(q), grid(k), grid(v)
    col = jnp.arange(GRID_W)
    col_start = jnp.clip(col - WIN_C // 2, 0, GRID_W - WIN_C)
    col_mask = (col[None, :] >= col_start[:, None]) & (col[None, :] < col_start[:, None] + WIN_C)
    mask = jnp.broadcast_to(col_mask[:, None, :], (GRID_W, wr, GRID_W)).reshape(GRID_W, wr * GRID_W)
    dc = jnp.clip(col[None, :] - col[:, None], -(WIN_C - 1), WIN_C - 1) + WIN_C - 1
    rpb = rpb.astype(jnp.float32)
    scale = Dh ** -0.5

    def row_block(r):
        rs = jnp.clip(r - WIN_R // 2, 0, rows - wr)
        kr = lax.dynamic_slice_in_dim(kg, rs, wr, axis=2).reshape(Bn, H, wr * GRID_W, Dh)
        vr = lax.dynamic_slice_in_dim(vg, rs, wr, axis=2).reshape(Bn, H, wr * GRID_W, Dh)
        qr = lax.dynamic_index_in_dim(qg, r, axis=2, keepdims=False)
        dr = rs + jnp.arange(wr) - r + WIN_R - 1
        bias = rpb[:, dr[:, None, None], dc[None, :, :]]
        bias = bias.transpose(0, 2, 1, 3).reshape(H, GRID_W, wr * GRID_W)
        s_lat = jnp.einsum('bhqd,bhkd->bhqk', qr, kr).astype(jnp.float32) * scale + bias
        s_lat = jnp.where(mask, s_lat, NEG_INF)
        s_ctx = jnp.einsum('bhqd,bhkd->bhqk', qr, kc).astype(jnp.float32) * scale
        p = jax.nn.softmax(jnp.concatenate([s_lat, s_ctx], axis=-1), axis=-1).astype(v.dtype)
        return jnp.einsum('bhqk,bhkd->bhqd', p, jnp.concatenate([vr, vc], axis=2))

    out = lax.map(row_block, jnp.arange(rows))
    return jnp.moveaxis(out, 0, 2).reshape(Bn, H, L, Dh)


def natten_mixer(lat, ctx, rpb, ctx_out):
    heads = lambda t: t.reshape(t.shape[0], t.shape[1], C_HEADS, C_DH).transpose(0, 2, 1, 3)
    flat = lambda o: o.transpose(0, 2, 1, 3).reshape(o.shape[0], o.shape[2], C_WIDTH)
    ql, kl, vl = heads(lat[0]), heads(lat[1]), heads(lat[2])
    qc, kc, vc = heads(ctx[0]), heads(ctx[1]), heads(ctx[2])
    y_lat = flat(neighbourhood_attention(ql, kl, vl, kc, vc, rpb))
    y_ctx = flat(dense_attention(qc, kc, vc)) if ctx_out else None
    return y_lat, y_ctx


def merge_branches(ya, yb, yc, gate_logits, w_up, w_out):
    y = jnp.stack([ya, yb, yc], axis=-2)
    up = jnp.einsum('blnw,nwd->blnd', y, w_up)
    gates = jax.nn.sigmoid(gate_logits.reshape(*gate_logits.shape[:-1], N_BRANCH, D_MODEL))
    return jnp.sum(gates * up, axis=-2) @ w_out


def peer_ffn(h, wq, keys, u, v):
    Bn, L, D = h.shape
    tok = h.reshape(Bn * L, D)
    T = Bn * L
    q = (tok @ wq).reshape(T, P_HEADS, 2, P_DKH)
    s = jnp.einsum('thpd,hpnd->thpn', q, keys).astype(jnp.float32)
    sv, si = lax.top_k(s, P_TOPK)
    cand_s = (sv[:, :, 0, :, None] + sv[:, :, 1, None, :]).reshape(T, P_HEADS, P_TOPK * P_TOPK)
    cand_i = (si[:, :, 0, :, None] * N_KEYS + si[:, :, 1, None, :]).reshape(T, P_HEADS, P_TOPK * P_TOPK)
    top_s, top_pos = lax.top_k(cand_s, P_TOPK)
    idx = jnp.take_along_axis(cand_i, top_pos, axis=-1)
    gate = jax.nn.softmax(top_s, axis=-1)
    nb = T // P_BLOCK

    def block(args):
        xb, ib, gb = args
        act = jax.nn.gelu(jnp.einsum('td,thkd->thk', xb, u[ib]).astype(jnp.float32), approximate=False)
        return jnp.einsum('thk,thkd->td', (gb * act).astype(v.dtype), v[ib])

    out = lax.map(block, (tok.reshape(nb, P_BLOCK, D), idx.reshape(nb, P_BLOCK, P_HEADS, P_TOPK),
                          gate.reshape(nb, P_BLOCK, P_HEADS, P_TOPK)))
    return out.reshape(Bn, L, D).astype(h.dtype)


def trunk_layer(x, xc, mod_lat, mod_ctx, n1_g, n2_g, w_in, conv_w, a_log, dt_bias, gdn_g,
                lam_p, subln_g, rpb, w_up, w_out, p_wq, p_keys, p_u, p_v, lam_init, ctx_out):
    L = x.shape[1]
    t = jnp.arange(L)
    row_pos, col_pos = t // GRID_W, t % GRID_W
    sh1, sc1, gt1, sh2, sc2, gt2 = jnp.split(mod_lat, 6, axis=-1)
    csh1, csc1, cgt1, csh2, csc2, cgt2 = jnp.split(mod_ctx, 6, axis=-1)
    pl = split_cols(modulate(rmsnorm(x, n1_g), sh1, sc1) @ w_in)
    pc = split_cols(modulate(rmsnorm(xc, n1_g), csh1, csc1) @ w_in)
    ya, ya_c = gdn_mixer(pl[0:6], pc[0:6], conv_w, a_log, dt_bias, gdn_g, ctx_out)
    yb, yb_c = diff_mixer(pl[6:9], pc[6:9], lam_p, subln_g, lam_init, row_pos, col_pos, ctx_out)
    yc, yc_c = natten_mixer(pl[9:12], pc[9:12], rpb, ctx_out)
    x = x + gt1 * merge_branches(ya, yb, yc, pl[12], w_up, w_out)
    x = x + gt2 * peer_ffn(modulate(rmsnorm(x, n2_g), sh2, sc2), p_wq, p_keys, p_u, p_v)
    if ctx_out:
        xc = xc + cgt1 * merge_branches(ya_c, yb_c, yc_c, pc[12], w_up, w_out)
        xc = xc + cgt2 * peer_ffn(modulate(rmsnorm(xc, n2_g), csh2, csc2), p_wq, p_keys, p_u, p_v)
    return x, xc


def setup_inputs(seed: int = 0) -> dict:
    key = jax.random.key(seed)
    ks = jax.random.split(key, 24)
    f32 = jnp.float32
    nrm = lambda k, shape, s: jax.random.normal(k, shape, f32) * s
    dt = jnp.exp(jax.random.uniform(ks[11], (DEPTH, 2, A_HEADS), f32, math.log(1e-3), math.log(1e-1)))
    return {
        'x': nrm(ks[0], (BATCH, SEQ, D_MODEL), 1.0),
        'c': nrm(ks[1], (BATCH, D_MODEL), 1.0),
        'ctx': nrm(ks[2], (BATCH, CTX_LEN, D_MODEL), 1.0),
        'c_ctx': nrm(ks[3], (D_MODEL,), 1.0),
        'norm1_g': 1.0 + nrm(ks[4], (DEPTH, D_MODEL), 0.01),
        'norm2_g': 1.0 + nrm(ks[5], (DEPTH, D_MODEL), 0.01),
        'ada_w': nrm(ks[6], (DEPTH, D_MODEL, 6 * D_MODEL), 0.5 * D_MODEL ** -0.5),
        'ada_b': nrm(ks[7], (DEPTH, 6 * D_MODEL), 0.01),
        'w_in': nrm(ks[8], (DEPTH, D_MODEL, IN_COLS), D_MODEL ** -0.5),
        'gdn_conv': nrm(ks[9], (DEPTH, CONV_K, 3 * A_WIDTH), CONV_K ** -0.5),
        'gdn_a_log': jnp.log(jax.random.uniform(ks[10], (DEPTH, 2, A_HEADS), f32, 1.0, 16.0)),
        'gdn_dt_bias': dt + jnp.log(-jnp.expm1(-dt)),
        'gdn_norm_g': 1.0 + nrm(ks[12], (DEPTH, A_DV), 0.01),
        'diff_lambda': nrm(ks[13], (DEPTH, 4, B_DH), 0.1),
        'diff_subln_g': 1.0 + nrm(ks[14], (DEPTH, 2 * B_DH), 0.01),
        'na_rpb': nrm(ks[15], (DEPTH, C_HEADS, 2 * WIN_R - 1, 2 * WIN_C - 1), 0.02),
        'w_up': nrm(ks[16], (DEPTH, N_BRANCH, BRANCH_W, D_MODEL), BRANCH_W ** -0.5),
        'w_out': nrm(ks[17], (DEPTH, D_MODEL, D_MODEL), D_MODEL ** -0.5),
        'peer_wq': nrm(ks[18], (DEPTH, D_MODEL, P_HEADS * 2 * P_DKH), D_MODEL ** -0.5),
        'peer_keys': nrm(ks[19], (DEPTH, P_HEADS, 2, N_KEYS, P_DKH), P_DKH ** -0.5),
        'peer_u': nrm(ks[20], (DEPTH, N_EXPERTS, D_MODEL), D_MODEL ** -0.5),
        'peer_v': nrm(ks[21], (DEPTH, N_EXPERTS, D_MODEL), P_HEADS ** -0.5),
        'final_g': 1.0 + nrm(ks[22], (D_MODEL,), 0.01),
    }


def reference(x, c, ctx, c_ctx, norm1_g, norm2_g, ada_w, ada_b, w_in, gdn_conv, gdn_a_log, gdn_dt_bias,
              gdn_norm_g, diff_lambda, diff_subln_g, na_rpb, w_up, w_out, peer_wq, peer_keys, peer_u, peer_v,
              final_g):
    xc = ctx
    for i in range(DEPTH):
        mod_lat = (jax.nn.silu(c) @ ada_w[i] + ada_b[i])[:, None, :]
        mod_ctx = jax.nn.silu(c_ctx) @ ada_w[i] + ada_b[i]
        lam_init = 0.8 - 0.6 * math.exp(-0.3 * i)
        x, xc = trunk_layer(x, xc, mod_lat, mod_ctx, norm1_g[i], norm2_g[i], w_in[i], gdn_conv[i],
                            gdn_a_log[i], gdn_dt_bias[i], gdn_norm_g[i], diff_lambda[i], diff_subln_g[i],
                            na_rpb[i], w_up[i], w_out[i], peer_wq[i], peer_keys[i], peer_u[i], peer_v[i],
                            lam_init, i < DEPTH - 1)
    return rmsnorm(x, final_g)
```

```python
import functools
import math

import numpy as np
import jax
import jax.numpy as jnp
from jax import lax
from jax.experimental import pallas as pl
from jax.experimental.pallas import tpu as pltpu

F32 = jnp.float32
BF16 = jnp.bfloat16

D_MODEL = 1024
DEPTH = 2
GRID_W = 64
EPS = 1e-6
NEG_INF = -1e30

A_HEADS = 4
A_DK = 128
A_DV = 128
CONV_K = 5
CHUNK = 64
B_HEADS = 4
B_DH = 64
ROPE_BASE = 10000.0
C_HEADS = 8
C_DH = 64
WIN_R = 8
WIN_C = 16
N_BRANCH = 3
BRANCH_W = 512
P_HEADS = 8
N_KEYS = 128
N_EXPERTS = N_KEYS * N_KEYS
P_DKH = 128
P_TOPK = 16

LANES = 128
VMEM_LIMIT = 56 * 1024 * 1024

COL_A_Q, COL_A_K, COL_A_V, COL_A_Z = 0, 4, 8, 12
COL_B_Q, COL_B_K, COL_B_V = 16, 20, 24
COL_C_Q, COL_C_K, COL_C_V = 28, 32, 36
COL_GATE = 40
MAIN_COLS = 64 * LANES


def _cparams(sem):
    return pltpu.CompilerParams(dimension_semantics=sem, vmem_limit_bytes=VMEM_LIMIT)


def _dot(a, b):
    return jnp.dot(a, b, preferred_element_type=F32)


def _dot_nt(a, b):
    return lax.dot_general(a, b, (((1,), (1,)), ((), ())), preferred_element_type=F32)


def _dot_tn(a, b):
    return lax.dot_general(a, b, (((0,), (0,)), ((), ())), preferred_element_type=F32)


def _inproj_kernel(x_ref, sh_ref, sc_ref, g_ref, w_ref, wba_ref, o_ref, oba_ref, hn_ref):
    @pl.when(pl.program_id(2) == 0)
    def _():
        x = x_ref[0]
        y = x * lax.rsqrt(jnp.mean(x * x, axis=-1, keepdims=True) + EPS) * g_ref[...]
        h = (y * (1.0 + sc_ref[0]) + sh_ref[0]).astype(BF16)
        hn_ref[...] = h
        oba_ref[0] = _dot(h, wba_ref[...])

    o_ref[0] = _dot(hn_ref[...], w_ref[...])


def _inproj(x, shift, scale, g, w_main, w_ba, tm):
    B, L, D = x.shape
    tn = 1024
    return pl.pallas_call(
        _inproj_kernel,
        grid=(B, L // tm, MAIN_COLS // tn),
        in_specs=[
            pl.BlockSpec((1, tm, D), lambda b, i, j: (b, i, 0)),
            pl.BlockSpec((1, 1, D), lambda b, i, j: (b, 0, 0)),
            pl.BlockSpec((1, 1, D), lambda b, i, j: (b, 0, 0)),
            pl.BlockSpec((1, D), lambda b, i, j: (0, 0)),
            pl.BlockSpec((D, tn), lambda b, i, j: (0, j)),
            pl.BlockSpec((D, LANES), lambda b, i, j: (0, 0)),
        ],
        out_specs=[
            pl.BlockSpec((1, tm, tn), lambda b, i, j: (b, i, j)),
            pl.BlockSpec((1, tm, LANES), lambda b, i, j: (b, i, 0)),
        ],
        out_shape=[
            jax.ShapeDtypeStruct((B, L, MAIN_COLS), F32),
            jax.ShapeDtypeStruct((B, L, LANES), F32),
        ],
        scratch_shapes=[pltpu.VMEM((tm, D), BF16)],
        compiler_params=_cparams(("parallel", "parallel", "arbitrary")),
        name="inproj",
    )(x, shift, scale, g, w_main, w_ba)


def _rope_tables(L):
    t = np.arange(L)
    row_pos, col_pos = t // GRID_W, t % GRID_W
    lane = np.arange(LANES)
    axis = (lane % 64) // 32
    f = lane % 16
    inv = 1.0 / (ROPE_BASE ** (f.astype(np.float32) / 16.0))
    pos = np.where(axis[None, :] == 0, row_pos[:, None], col_pos[:, None]).astype(np.float32)
    ang = jnp.asarray(pos) * jnp.asarray(inv.astype(np.float32))[None, :]
    first = jnp.asarray(((lane % 32) < 16)[None, :])
    return jnp.cos(ang), jnp.where(first, -jnp.sin(ang), jnp.sin(ang))


def _rope_kernel(q_ref, k_ref, cos_ref, sin_ref, qo_ref, ko_ref):
    lane = lax.broadcasted_iota(jnp.int32, (1, LANES), 1)
    first = (lane % 32) < 16
    c, s = cos_ref[...], sin_ref[...]

    def rope(x):
        partner = jnp.where(first, pltpu.roll(x, LANES - 16, 1), pltpu.roll(x, 16, 1))
        return x * c + partner * s

    for h in range(B_HEADS):
        sl = slice(h * LANES, (h + 1) * LANES)
        qo_ref[0, :, sl] = (rope(q_ref[0, :, sl]) * (B_DH ** -0.5)).astype(BF16)
        ko_ref[0, :, sl] = rope(k_ref[0, :, sl]).astype(BF16)


def _rope(proj, cos_t, sin_t, tr):
    B, L, _ = proj.shape
    W = B_HEADS * LANES
    return pl.pallas_call(
        _rope_kernel,
        grid=(B, L // tr),
        in_specs=[
            pl.BlockSpec((1, tr, W), lambda b, i: (b, i, COL_B_Q * LANES // W)),
            pl.BlockSpec((1, tr, W), lambda b, i: (b, i, COL_B_K * LANES // W)),
            pl.BlockSpec((tr, LANES), lambda b, i: (i, 0)),
            pl.BlockSpec((tr, LANES), lambda b, i: (i, 0)),
        ],
        out_specs=[pl.BlockSpec((1, tr, W), lambda b, i: (b, i, 0))] * 2,
        out_shape=[jax.ShapeDtypeStruct((B, L, W), BF16)] * 2,
        compiler_params=_cparams(("parallel", "parallel")),
        name="rope",
    )(proj, proj, cos_t, sin_t)


def _softmax_pv(q, key_vals):
    scores = [_dot_nt(q, k) for k, _ in key_vals]
    m = functools.reduce(jnp.maximum, [s.max(axis=-1, keepdims=True) for s in scores])
    es = [jnp.exp(s - m) for s in scores]
    denom = functools.reduce(jnp.add, [e.sum(axis=-1, keepdims=True) for e in es])
    o = functools.reduce(jnp.add, [_dot(e.astype(BF16), v) for e, (_, v) in zip(es, key_vals)])
    return o / denom


def _diff_finish(o0, o1, lam, g, out_scale):
    o = o0 - lam * o1
    return o * lax.rsqrt(jnp.mean(o * o, axis=-1, keepdims=True) + EPS) * g * out_scale


def _diff_lat_kernel(lam_ref, q_ref, kl_ref, vl_ref, kc_ref, vc_ref, g_ref, o_ref, *, out_scale):
    lane = lax.broadcasted_iota(jnp.int32, (1, LANES), 1)
    low = lane < B_DH
    q = q_ref[0]
    kv = [(kl_ref[0], vl_ref[0].astype(BF16)), (kc_ref[0].astype(BF16), vc_ref[0].astype(BF16))]
    o0 = _softmax_pv(jnp.where(low, q, 0).astype(BF16), kv)
    o1 = _softmax_pv(jnp.where(low, 0, q).astype(BF16), kv)
    o_ref[0] = _diff_finish(o0, o1, lam_ref[0], g_ref[...], out_scale)


def _diff_lat(lam, q_r, k_r, proj_lat, proj_ctx, subln_g, out_scale, tq):
    B, L, _ = proj_lat.shape
    Lc = proj_ctx.shape[1]
    return pl.pallas_call(
        functools.partial(_diff_lat_kernel, out_scale=out_scale),
        grid=(B, B_HEADS, L // tq),
        in_specs=[
            pl.BlockSpec(memory_space=pltpu.SMEM),
            pl.BlockSpec((1, tq, LANES), lambda b, h, i: (b, i, h)),
            pl.BlockSpec((1, L, LANES), lambda b, h, i: (b, 0, h)),
            pl.BlockSpec((1, L, LANES), lambda b, h, i: (b, 0, COL_B_V + h)),
            pl.BlockSpec((1, Lc, LANES), lambda b, h, i: (b, 0, COL_B_K + h)),
            pl.BlockSpec((1, Lc, LANES), lambda b, h, i: (b, 0, COL_B_V + h)),
            pl.BlockSpec((1, LANES), lambda b, h, i: (0, 0)),
        ],
        out_specs=pl.BlockSpec((1, tq, LANES), lambda b, h, i: (b, i, h)),
        out_shape=jax.ShapeDtypeStruct((B, L, B_HEADS * LANES), F32),
        compiler_params=_cparams(("parallel", "parallel", "parallel")),
        name="diff_lat",
    )(lam, q_r, k_r, proj_lat, proj_ctx, proj_ctx, subln_g)


def _diff_ctx_kernel(lam_ref, q_ref, k_ref, v_ref, g_ref, o_ref, *, out_scale):
    lane = lax.broadcasted_iota(jnp.int32, (1, LANES), 1)
    low = lane < B_DH
    q = q_ref[0] * (B_DH ** -0.5)
    kv = [(k_ref[0].astype(BF16), v_ref[0].astype(BF16))]
    o0 = _softmax_pv(jnp.where(low, q, 0).astype(BF16), kv)
    o1 = _softmax_pv(jnp.where(low, 0, q).astype(BF16), kv)
    o_ref[0] = _diff_finish(o0, o1, lam_ref[0], g_ref[...], out_scale)


def _diff_ctx(lam, proj_ctx, subln_g, out_scale):
    B, Lc, _ = proj_ctx.shape
    return pl.pallas_call(
        functools.partial(_diff_ctx_kernel, out_scale=out_scale),
        grid=(B, B_HEADS),
        in_specs=[
            pl.BlockSpec(memory_space=pltpu.SMEM),
            pl.BlockSpec((1, Lc, LANES), lambda b, h: (b, 0, COL_B_Q + h)),
            pl.BlockSpec((1, Lc, LANES), lambda b, h: (b, 0, COL_B_K + h)),
            pl.BlockSpec((1, Lc, LANES), lambda b, h: (b, 0, COL_B_V + h)),
            pl.BlockSpec((1, LANES), lambda b, h: (0, 0)),
        ],
        out_specs=pl.BlockSpec((1, Lc, LANES), lambda b, h: (b, 0, h)),
        out_shape=jax.ShapeDtypeStruct((B, Lc, B_HEADS * LANES), F32),
        compiler_params=_cparams(("parallel", "parallel")),
        name="diff_ctx",
    )(lam, proj_ctx, proj_ctx, proj_ctx, subln_g)


NA_ROWS_PER_STEP = 8
NA_KEYS = WIN_R * GRID_W


def _natten_bias(rpb):
    col = np.arange(GRID_W)
    col_start = np.clip(col - WIN_C // 2, 0, GRID_W - WIN_C)
    col_mask = (col[None, :] >= col_start[:, None]) & (col[None, :] < col_start[:, None] + WIN_C)
    dc = np.clip(col[None, :] - col[:, None], -(WIN_C - 1), WIN_C - 1) + WIN_C - 1
    d = np.arange(WIN_R)
    dr = np.arange(WIN_R)[None, :] - d[:, None] + WIN_R - 1
    bias = rpb.astype(F32)[:, dr[:, :, None, None], dc[None, None, :, :]]
    bias = jnp.where(jnp.asarray(col_mask)[None, None, None], bias, NEG_INF)
    return bias.transpose(0, 1, 3, 2, 4).reshape(C_HEADS, WIN_R, GRID_W, NA_KEYS)


def _natten_kernel(q_ref, k_ref, v_ref, kc_ref, vc_ref, bias_ref, o_ref, *, rows):
    lane = lax.broadcasted_iota(jnp.int32, (1, LANES), 1)
    low = lane < C_DH
    kc = kc_ref[0].astype(BF16)
    vc = vc_ref[0].astype(BF16)
    for rr in range(NA_ROWS_PER_STEP):
        r = pl.program_id(2) * NA_ROWS_PER_STEP + rr
        rs = jnp.clip(r - WIN_R // 2, 0, rows - WIN_R)
        cfg = r - rs
        start = pl.multiple_of(rs * GRID_W, GRID_W)
        kw = k_ref[0, pl.ds(start, NA_KEYS), :].astype(BF16)
        vw = v_ref[0, pl.ds(start, NA_KEYS), :].astype(BF16)
        q = q_ref[0, rr * GRID_W:(rr + 1) * GRID_W, :] * (C_DH ** -0.5)
        outs = []
        for hh in range(2):
            qm = (jnp.where(low, q, 0) if hh == 0 else jnp.where(low, 0, q)).astype(BF16)
            s_lat = _dot_nt(qm, kw) + bias_ref[hh, cfg]
            s_ctx = _dot_nt(qm, kc)
            m = jnp.maximum(s_lat.max(axis=-1, keepdims=True), s_ctx.max(axis=-1, keepdims=True))
            e_lat, e_ctx = jnp.exp(s_lat - m), jnp.exp(s_ctx - m)
            denom = e_lat.sum(axis=-1, keepdims=True) + e_ctx.sum(axis=-1, keepdims=True)
            outs.append((_dot(e_lat.astype(BF16), vw) + _dot(e_ctx.astype(BF16), vc)) / denom)
        o_ref[0, rr * GRID_W:(rr + 1) * GRID_W, :] = jnp.where(low, outs[0], outs[1])


def _natten_lat(proj_lat, proj_ctx, bias):
    B, L, _ = proj_lat.shape
    Lc = proj_ctx.shape[1]
    rows = L // GRID_W
    HP = C_HEADS // 2
    tq = NA_ROWS_PER_STEP * GRID_W
    return pl.pallas_call(
        functools.partial(_natten_kernel, rows=rows),
        grid=(B, HP, rows // NA_ROWS_PER_STEP),
        in_specs=[
            pl.BlockSpec((1, tq, LANES), lambda b, h, i: (b, i, COL_C_Q + h)),
            pl.BlockSpec((1, L, LANES), lambda b, h, i: (b, 0, COL_C_K + h)),
            pl.BlockSpec((1, L, LANES), lambda b, h, i: (b, 0, COL_C_V + h)),
            pl.BlockSpec((1, Lc, LANES), lambda b, h, i: (b, 0, COL_C_K + h)),
            pl.BlockSpec((1, Lc, LANES), lambda b, h, i: (b, 0, COL_C_V + h)),
            pl.BlockSpec((2, WIN_R, GRID_W, NA_KEYS), lambda b, h, i: (h, 0, 0, 0)),
        ],
        out_specs=pl.BlockSpec((1, tq, LANES), lambda b, h, i: (b, i, h)),
        out_shape=jax.ShapeDtypeStruct((B, L, HP * LANES), F32),
        compiler_params=_cparams(("parallel", "parallel", "parallel")),
        name="natten_lat",
    )(proj_lat, proj_lat, proj_lat, proj_ctx, proj_ctx, bias)


def _natten_ctx_kernel(q_ref, k_ref, v_ref, o_ref):
    lane = lax.broadcasted_iota(jnp.int32, (1, LANES), 1)
    low = lane < C_DH
    q = q_ref[0] * (C_DH ** -0.5)
    kv = [(k_ref[0].astype(BF16), v_ref[0].astype(BF16))]
    o0 = _softmax_pv(jnp.where(low, q, 0).astype(BF16), kv)
    o1 = _softmax_pv(jnp.where(low, 0, q).astype(BF16), kv)
    o_ref[0] = jnp.where(low, o0, o1)


def _natten_ctx(proj_ctx):
    B, Lc, _ = proj_ctx.shape
    HP = C_HEADS // 2
    return pl.pallas_call(
        _natten_ctx_kernel,
        grid=(B, HP),
        in_specs=[
            pl.BlockSpec((1, Lc, LANES), lambda b, h: (b, 0, COL_C_Q + h)),
            pl.BlockSpec((1, Lc, LANES), lambda b, h: (b, 0, COL_C_K + h)),
            pl.BlockSpec((1, Lc, LANES), lambda b, h: (b, 0, COL_C_V + h)),
        ],
        out_specs=pl.BlockSpec((1, Lc, LANES), lambda b, h: (b, 0, h)),
        out_shape=jax.ShapeDtypeStruct((B, Lc, HP * LANES), F32),
        compiler_params=_cparams(("parallel", "parallel")),
        name="natten_ctx",
    )(proj_ctx, proj_ctx, proj_ctx)


def _merge_kernel(ya_ref, yb_ref, yc_ref, g0_ref, g1_ref, g2_ref, wup_ref, wout_ref, x_ref, gt_ref, o_ref):
    acc = None
    for n, (y_ref, g_ref) in enumerate(((ya_ref, g0_ref), (yb_ref, g1_ref), (yc_ref, g2_ref))):
        up = _dot(y_ref[0].astype(BF16), wup_ref[n])
        t = jax.nn.sigmoid(g_ref[0]) * up
        acc = t if acc is None else acc + t
    r = _dot(acc.astype(BF16), wout_ref[...])
    o_ref[0] = x_ref[0] + gt_ref[0] * r


def _merge(ya, yb, yc, proj, w_up, w_out, x, gate, tm):
    B, L, D = x.shape
    gcol = COL_GATE * LANES // D
    yspec = pl.BlockSpec((1, tm, BRANCH_W), lambda b, i: (b, i, 0))
    return pl.pallas_call(
        _merge_kernel,
        grid=(B, L // tm),
        in_specs=[
            yspec, yspec, yspec,
            pl.BlockSpec((1, tm, D), lambda b, i: (b, i, gcol)),
            pl.BlockSpec((1, tm, D), lambda b, i: (b, i, gcol + 1)),
            pl.BlockSpec((1, tm, D), lambda b, i: (b, i, gcol + 2)),
            pl.BlockSpec((N_BRANCH, BRANCH_W, D), lambda b, i: (0, 0, 0)),
            pl.BlockSpec((D, D), lambda b, i: (0, 0)),
            pl.BlockSpec((1, tm, D), lambda b, i: (b, i, 0)),
            pl.BlockSpec((1, 1, D), lambda b, i: (b, 0, 0)),
        ],
        out_specs=pl.BlockSpec((1, tm, D), lambda b, i: (b, i, 0)),
        out_shape=jax.ShapeDtypeStruct((B, L, D), F32),
        compiler_params=_cparams(("parallel", "parallel")),
        name="merge",
    )(ya, yb, yc, proj, proj, proj, w_up, w_out, x, gate)


PEER_CAND = P_TOPK + 1
PEER_A_PAD = 24
PEER_CAND_ROWS = PEER_A_PAD + 7 * 8 + 16


def _extract_top(s, n):
    vals = []
    for _ in range(n):
        m = jnp.max(s, axis=0, keepdims=True)
        vals.append(m)
        s = jnp.where(s == m, NEG_INF, s)
    return vals


def _peerq_kernel(x_ref, sh_ref, sc_ref, g_ref, wq_ref, keys_ref,
                  xn_ref, s2_ref, tau_ref, e1_ref, e2_ref, ab_ref, cand_ref):
    x = x_ref[0]
    tm = x.shape[0]
    y = x * lax.rsqrt(jnp.mean(x * x, axis=-1, keepdims=True) + EPS) * g_ref[...]
    xn = (y * (1.0 + sc_ref[0]) + sh_ref[0]).astype(BF16)
    xn_ref[...] = xn
    q = _dot(xn, wq_ref[...]).astype(BF16)
    row = lax.broadcasted_iota(jnp.int32, (PEER_A_PAD, 1), 0)
    row8 = lax.broadcasted_iota(jnp.int32, (8, 1), 0)
    row16 = lax.broadcasted_iota(jnp.int32, (16, 1), 0)
    for h in range(P_HEADS):
        s = []
        for p in range(2):
            hp = 2 * h + p
            st = _dot_nt(keys_ref[hp], q[:, hp * P_DKH:(hp + 1) * P_DKH])
            s.append(st)
            ab_ref[p] = jnp.full((PEER_A_PAD, tm), NEG_INF, F32)
            for r, m in enumerate(_extract_top(st, PEER_CAND)):
                ab_ref[p, r:r + 1, :] = m
        a_all, b_all = ab_ref[0], ab_ref[1]
        cand_ref[0:PEER_A_PAD, :] = a_all[0:1, :] + b_all
        for i in range(1, 8):
            n_i = PEER_CAND // (i + 1)
            cand_ref[PEER_A_PAD + 8 * (i - 1):PEER_A_PAD + 8 * i, :] = jnp.where(
                row8 < n_i, a_all[i:i + 1, :] + b_all[0:8, :], NEG_INF)
        cand_ref[PEER_A_PAD + 56:PEER_A_PAD + 72, :] = jnp.where(
            row16 + 8 < PEER_CAND, a_all[8:24, :] + b_all[0:1, :], NEG_INF)
        top = _extract_top(cand_ref[...], PEER_CAND)
        z = functools.reduce(jnp.add, [jnp.exp(v - top[0]) for v in top[:P_TOPK]])
        thr = 0.5 * (top[P_TOPK - 1] + top[P_TOPK])
        s2_ref[h] = s[1]
        tau_ref[h] = thr - s[0]
        e1_ref[h] = jnp.exp(s[0] - a_all[0:1, :])
        e2_ref[h] = jnp.exp(s[1] - b_all[0:1, :]) / z


def _peerq(x, shift, scale, g, wq, keys, tm):
    B, L, D = x.shape
    T = B * L
    nb = L // tm
    tok_spec = pl.BlockSpec((P_HEADS, N_KEYS, tm), lambda b, i: (0, 0, b * nb + i))
    tok_shape = jax.ShapeDtypeStruct((P_HEADS, N_KEYS, T), F32)
    return pl.pallas_call(
        _peerq_kernel,
        grid=(B, nb),
        in_specs=[
            pl.BlockSpec((1, tm, D), lambda b, i: (b, i, 0)),
            pl.BlockSpec((1, 1, D), lambda b, i: (b, 0, 0)),
            pl.BlockSpec((1, 1, D), lambda b, i: (b, 0, 0)),
            pl.BlockSpec((1, D), lambda b, i: (0, 0)),
            pl.BlockSpec((D, 2 * P_HEADS * P_DKH), lambda b, i: (0, 0)),
            pl.BlockSpec((2 * P_HEADS, N_KEYS, P_DKH), lambda b, i: (0, 0, 0)),
        ],
        out_specs=[pl.BlockSpec((tm, D), lambda b, i: (b * nb + i, 0))] + [tok_spec] * 4,
        out_shape=[jax.ShapeDtypeStruct((T, D), BF16)] + [tok_shape] * 4,
        scratch_shapes=[pltpu.VMEM((2, PEER_A_PAD, tm), F32), pltpu.VMEM((PEER_CAND_ROWS, tm), F32)],
        compiler_params=_cparams(("parallel", "parallel")),
        name="peer_query",
    )(x, shift, scale, g, wq, keys)


PEER_EC = 1024
PEER_TT = 512


def _gelu(x):
    return 0.5 * x * (1.0 + lax.erf(x * math.sqrt(0.5)))


def _peer_kernel(xn_ref, u_ref, vt_ref, s2_ref, tau_ref, e1_ref, e2_ref, x_ref, gt_ref, fg_ref,
                 o_ref, acc_ref, pt_ref, *, final_norm):
    c = pl.program_id(1)

    @pl.when(c == 0)
    def _():
        acc_ref[...] = jnp.zeros_like(acc_ref)

    act = _gelu(_dot_nt(u_ref[...], xn_ref[...]))
    for aa in range(PEER_EC // N_KEYS):
        a = c * (PEER_EC // N_KEYS) + aa
        gate = None
        for h in range(P_HEADS):
            tau = tau_ref[h, pl.ds(a, 1), :]
            e1 = e1_ref[h, pl.ds(a, 1), :]
            t = e1 * jnp.where(s2_ref[h] >= tau, e2_ref[h], 0.0)
            gate = t if gate is None else gate + t
        pt_ref[aa * N_KEYS:(aa + 1) * N_KEYS, :] = (gate * act[aa * N_KEYS:(aa + 1) * N_KEYS, :]).astype(BF16)
    acc_ref[...] += _dot(vt_ref[...], pt_ref[...])

    @pl.when(c == pl.num_programs(1) - 1)
    def _():
        y = x_ref[...] + gt_ref[0] * acc_ref[...].T
        if final_norm:
            y = y * lax.rsqrt(jnp.mean(y * y, axis=-1, keepdims=True) + EPS) * fg_ref[...]
        o_ref[...] = y


def _peer(xn, u, vt, s2, tau, e1, e2, x, gate, final_g, final_norm):
    B, L, D = x.shape
    T = B * L
    TT = min(PEER_TT, L)
    per_b = L // TT
    tok_spec = pl.BlockSpec((P_HEADS, N_KEYS, TT), lambda i, c: (0, 0, i))
    out = pl.pallas_call(
        functools.partial(_peer_kernel, final_norm=final_norm),
        grid=(T // TT, N_EXPERTS // PEER_EC),
        in_specs=[
            pl.BlockSpec((TT, D), lambda i, c: (i, 0)),
            pl.BlockSpec((PEER_EC, D), lambda i, c: (c, 0)),
            pl.BlockSpec((D, PEER_EC), lambda i, c: (0, c)),
            tok_spec, tok_spec, tok_spec, tok_spec,
            pl.BlockSpec((TT, D), lambda i, c: (i, 0)),
            pl.BlockSpec((1, 1, D), lambda i, c: (i // per_b, 0, 0)),
            pl.BlockSpec((1, D), lambda i, c: (0, 0)),
        ],
        out_specs=pl.BlockSpec((TT, D), lambda i, c: (i, 0)),
        out_shape=jax.ShapeDtypeStruct((T, D), F32),
        scratch_shapes=[pltpu.VMEM((D, TT), F32), pltpu.VMEM((PEER_EC, TT), BF16)],
        compiler_params=_cparams(("parallel", "arbitrary")),
        name="peer_experts",
    )(xn, u, vt, s2, tau, e1, e2, x.reshape(T, D), gate, final_g)
    return out.reshape(B, L, D)


def _rmsnorm(x, g):
    return x * lax.rsqrt(jnp.mean(x * x, axis=-1, keepdims=True) + EPS) * g


def _l2norm(x):
    return x * lax.rsqrt(jnp.sum(x * x, axis=-1, keepdims=True) + EPS)


def _gated_delta_chunked(q, k, v, beta, g, s0):
    Bn, H, L, DK = q.shape
    DV = v.shape[-1]
    n = L // CHUNK
    rs = lambda t: t.reshape(Bn, H, n, CHUNK, *t.shape[3:])
    q, k, v, beta, g = rs(q), rs(k), rs(v), rs(beta), rs(g)
    g = jnp.cumsum(g, axis=-1)
    kb = k * beta[..., None]
    vb = v * beta[..., None]
    tri_incl = jnp.tril(jnp.ones((CHUNK, CHUNK), bool))
    tri_strict = jnp.tril(jnp.ones((CHUNK, CHUNK), bool), -1)
    decay_mat = jnp.exp(jnp.where(tri_incl, g[..., :, None] - g[..., None, :], -jnp.inf))
    a_strict = jnp.where(tri_strict, jnp.einsum('bhnid,bhnjd->bhnij', kb, k) * decay_mat, 0.0)
    rhs = jnp.concatenate([vb, kb * jnp.exp(g)[..., None]], axis=-1)
    sol = jax.lax.linalg.triangular_solve(a_strict, rhs, left_side=True, lower=True, unit_diagonal=True)
    u, w = sol[..., :DV], sol[..., DV:]
    qk = jnp.where(tri_incl, jnp.einsum('bhnid,bhnjd->bhnij', q, k) * decay_mat, 0.0)
    g_last = g[..., -1]
    k_dec = k * jnp.exp(g_last[..., None] - g)[..., None]
    q_dec = q * jnp.exp(g)[..., None]

    def step(S, xs):
        qc, qkc, uc, wc, kdc, glc = xs
        v_new = uc - jnp.einsum('bhcd,bhde->bhce', wc, S)
        o = jnp.einsum('bhcd,bhde->bhce', qc, S) + jnp.einsum('bhij,bhje->bhie', qkc, v_new)
        S = S * jnp.exp(glc)[..., None, None] + jnp.einsum('bhcd,bhce->bhde', kdc, v_new)
        return S, o

    xs = tuple(jnp.moveaxis(t, 2, 0) for t in (q_dec, qk, u, w, k_dec, g_last))
    s_final, o = lax.scan(step, s0, xs)
    return jnp.moveaxis(o, 0, 2).reshape(Bn, H, L, DV), s_final


def _gdn_prep(proj, ba, conv_w, a_log, dt_bias):
    Bn, L, _ = proj.shape
    W = A_HEADS * A_DK
    x = proj[..., :3 * W]
    pad = CONV_K // 2
    y = lax.conv_general_dilated(x, conv_w[:, None, :], window_strides=(1,), padding=[(pad, pad)],
                                 dimension_numbers=('NWC', 'WIO', 'NWC'), feature_group_count=3 * W)
    qkv = jax.nn.silu(y)
    q, k, v = jnp.split(qkv, 3, axis=-1)
    heads = lambda t, d: t.reshape(Bn, L, A_HEADS, d).transpose(0, 2, 1, 3)
    q = _l2norm(heads(q, A_DK)) * (A_DK ** -0.5)
    k = _l2norm(heads(k, A_DK))
    v = heads(v, A_DV)
    dirs = lambda t: t.reshape(Bn, L, 2, A_HEADS).transpose(2, 0, 3, 1)
    beta = jax.nn.sigmoid(dirs(ba[..., :2 * A_HEADS]))
    g = -jnp.exp(a_log)[:, None, :, None] * jax.nn.softplus(
        dirs(ba[..., 2 * A_HEADS:4 * A_HEADS]) + dt_bias[:, None, :, None])
    return q, k, v, beta, g


def _gdn_out(o, z, norm_g):
    Bn, H, L, DV = o.shape
    o = _rmsnorm(o.transpose(0, 2, 1, 3), norm_g)
    z = z.reshape(Bn, L, H, DV)
    return (o * jax.nn.silu(z)).reshape(Bn, L, H * DV)


def _gdn_mixer(proj_lat, ba_lat, proj_ctx, ba_ctx, conv_w, a_log, dt_bias, norm_g, ctx_out):
    ql, kl, vl, bl, gl = _gdn_prep(proj_lat, ba_lat, conv_w, a_log, dt_bias)
    qc, kc, vc, bc, gc = _gdn_prep(proj_ctx, ba_ctx, conv_w, a_log, dt_bias)
    flip = lambda t: jnp.flip(t, axis=2)
    s0 = jnp.zeros((ql.shape[0], A_HEADS, A_DK, A_DV), F32)
    oc_f, sc_f = _gated_delta_chunked(qc, kc, vc, bc[0], gc[0], s0)
    oc_b, sc_b = _gated_delta_chunked(flip(qc), flip(kc), flip(vc), flip(bc[1]), flip(gc[1]), s0)
    ol_f, _ = _gated_delta_chunked(ql, kl, vl, bl[0], gl[0], sc_f)
    ol_b, _ = _gated_delta_chunked(flip(ql), flip(kl), flip(vl), flip(bl[1]), flip(gl[1]), sc_b)
    W = A_HEADS * A_DV
    zl = proj_lat[..., COL_A_Z * LANES:COL_A_Z * LANES + W]
    zc = proj_ctx[..., COL_A_Z * LANES:COL_A_Z * LANES + W]
    y_lat = _gdn_out(ol_f + flip(ol_b), zl, norm_g)
    y_ctx = _gdn_out(oc_f + flip(oc_b), zc, norm_g) if ctx_out else None
    return y_lat, y_ctx


def _layer(i, x, xc, c, c_ctx, p, cos_t, sin_t, ctx_out, final_g, final_norm):
    B, L, D = x.shape
    Lc = xc.shape[1]
    mod_lat = (jax.nn.silu(c) @ p['ada_w'] + p['ada_b'])[:, None, :]
    mod_ctx = jnp.broadcast_to((jax.nn.silu(c_ctx) @ p['ada_w'] + p['ada_b'])[None, None, :], (B, 1, 6 * D))
    sh1, sc1, gt1, sh2, sc2, gt2 = jnp.split(mod_lat, 6, axis=-1)
    csh1, csc1, cgt1, csh2, csc2, cgt2 = jnp.split(mod_ctx, 6, axis=-1)
    lam_init = 0.8 - 0.6 * math.exp(-0.3 * i)

    w_in = p['w_in']
    n_a = 4 * A_HEADS * A_DK
    w_main = jnp.concatenate([w_in[:, :n_a], w_in[:, n_a + 4 * A_HEADS:]], axis=1).astype(BF16)
    w_ba = jnp.pad(w_in[:, n_a:n_a + 4 * A_HEADS], ((0, 0), (0, LANES - 4 * A_HEADS))).astype(BF16)
    n1 = p['norm1_g'][None, :]
    proj_lat, ba_lat = _inproj(x, sh1, sc1, n1, w_main, w_ba, 512)
    proj_ctx, ba_ctx = _inproj(xc, csh1, csc1, n1, w_main, w_ba, Lc)

    ya, ya_c = _gdn_mixer(proj_lat, ba_lat, proj_ctx, ba_ctx, p['gdn_conv'], p['gdn_a_log'], p['gdn_dt_bias'],
                          p['gdn_norm_g'], ctx_out)

    lp = p['diff_lambda']
    lam = (jnp.exp(jnp.sum(lp[0] * lp[1])) - jnp.exp(jnp.sum(lp[2] * lp[3])) + lam_init).reshape(1)
    q_r, k_r = _rope(proj_lat, cos_t, sin_t, 512)
    subln = p['diff_subln_g'][None, :]
    yb = _diff_lat(lam, q_r, k_r, proj_lat, proj_ctx, subln, 1.0 - lam_init, 256)
    yb_c = _diff_ctx(lam, proj_ctx, subln, 1.0 - lam_init) if ctx_out else None

    yc = _natten_lat(proj_lat, proj_ctx, _natten_bias(p['na_rpb']))
    yc_c = _natten_ctx(proj_ctx) if ctx_out else None

    w_up = p['w_up'].astype(BF16)
    w_out = p['w_out'].astype(BF16)
    wq = p['peer_wq'].astype(BF16)
    keys = p['peer_keys'].reshape(2 * P_HEADS, N_KEYS, P_DKH).astype(BF16)
    u = p['peer_u'].astype(BF16)
    vt = p['peer_v'].T.astype(BF16)
    n2 = p['norm2_g'][None, :]
    fg = final_g[None, :]

    x = _merge(ya, yb, yc, proj_lat, w_up, w_out, x, gt1, 512)
    pq = _peerq(x, sh2, sc2, n2, wq, keys, 256)
    x = _peer(pq[0], u, vt, pq[1], pq[2], pq[3], pq[4], x, gt2, fg, final_norm)
    if ctx_out:
        xc = _merge(ya_c, yb_c, yc_c, proj_ctx, w_up, w_out, xc, cgt1, Lc)
        pq = _peerq(xc, csh2, csc2, n2, wq, keys, Lc)
        xc = _peer(pq[0], u, vt, pq[1], pq[2], pq[3], pq[4], xc, cgt2, fg, False)
    return x, xc


def kernel(x, c, ctx, c_ctx, norm1_g, norm2_g, ada_w, ada_b, w_in, gdn_conv, gdn_a_log, gdn_dt_bias, gdn_norm_g,
           diff_lambda, diff_subln_g, na_rpb, w_up, w_out, peer_wq, peer_keys, peer_u, peer_v, final_g):
    cos_t, sin_t = _rope_tables(x.shape[1])
    xc = ctx
    for i in range(DEPTH):
        p = dict(norm1_g=norm1_g[i], norm2_g=norm2_g[i], ada_w=ada_w[i], ada_b=ada_b[i], w_in=w_in[i],
                 gdn_conv=gdn_conv[i], gdn_a_log=gdn_a_log[i], gdn_dt_bias=gdn_dt_bias[i], gdn_norm_g=gdn_norm_g[i],
                 diff_lambda=diff_lambda[i], diff_subln_g=diff_subln_g[i], na_rpb=na_rpb[i], w_up=w_up[i],
                 w_out=w_out[i], peer_wq=peer_wq[i], peer_keys=peer_keys[i], peer_u=peer_u[i], peer_v=peer_v[i])
        x, xc = _layer(i, x, xc, c, c_ctx, p, cos_t, sin_t, i < DEPTH - 1, final_g, i == DEPTH - 1)
    return x
```

```python
import functools
import math

import numpy as np
import jax
import jax.numpy as jnp
from jax import lax
from jax.experimental import pallas as pl
from jax.experimental.pallas import tpu as pltpu

F32 = jnp.float32
BF16 = jnp.bfloat16

D_MODEL = 1024
DEPTH = 2
GRID_W = 64
EPS = 1e-6
NEG_INF = -1e30

A_HEADS = 4
A_DK = 128
A_DV = 128
CONV_K = 5
CHUNK = 64
B_HEADS = 4
B_DH = 64
ROPE_BASE = 10000.0
C_HEADS = 8
C_DH = 64
WIN_R = 8
WIN_C = 16
N_BRANCH = 3
BRANCH_W = 512
P_HEADS = 8
N_KEYS = 128
N_EXPERTS = N_KEYS * N_KEYS
P_DKH = 128
P_TOPK = 16

LANES = 128
VMEM_LIMIT = 56 * 1024 * 1024

COL_A_Q, COL_A_K, COL_A_V, COL_A_Z = 0, 4, 8, 12
COL_B_Q, COL_B_K, COL_B_V = 16, 20, 24
COL_C_Q, COL_C_K, COL_C_V = 28, 32, 36
COL_GATE = 40
MAIN_COLS = 64 * LANES


def _cparams(sem):
    return pltpu.CompilerParams(dimension_semantics=sem, vmem_limit_bytes=VMEM_LIMIT)


def _dot(a, b):
    return jnp.dot(a, b, preferred_element_type=F32)


def _dot_nt(a, b):
    return lax.dot_general(a, b, (((1,), (1,)), ((), ())), preferred_element_type=F32)


def _dot_tn(a, b):
    return lax.dot_general(a, b, (((0,), (0,)), ((), ())), preferred_element_type=F32)


def _inproj_kernel(x_ref, sh_ref, sc_ref, g_ref, w_ref, wba_ref, o_ref, oba_ref, hn_ref):
    @pl.when(pl.program_id(2) == 0)
    def _():
        x = x_ref[0]
        y = x * lax.rsqrt(jnp.mean(x * x, axis=-1, keepdims=True) + EPS) * g_ref[...]
        h = (y * (1.0 + sc_ref[0]) + sh_ref[0]).astype(BF16)
        hn_ref[...] = h
        oba_ref[0] = _dot(h, wba_ref[...])

    o_ref[0] = _dot(hn_ref[...], w_ref[...])


def _inproj(x, shift, scale, g, w_main, w_ba, tm):
    B, L, D = x.shape
    tn = 1024
    return pl.pallas_call(
        _inproj_kernel,
        grid=(B, L // tm, MAIN_COLS // tn),
        in_specs=[
            pl.BlockSpec((1, tm, D), lambda b, i, j: (b, i, 0)),
            pl.BlockSpec((1, 1, D), lambda b, i, j: (b, 0, 0)),
            pl.BlockSpec((1, 1, D), lambda b, i, j: (b, 0, 0)),
            pl.BlockSpec((1, D), lambda b, i, j: (0, 0)),
            pl.BlockSpec((D, tn), lambda b, i, j: (0, j)),
            pl.BlockSpec((D, LANES), lambda b, i, j: (0, 0)),
        ],
        out_specs=[
            pl.BlockSpec((1, tm, tn), lambda b, i, j: (b, i, j)),
            pl.BlockSpec((1, tm, LANES), lambda b, i, j: (b, i, 0)),
        ],
        out_shape=[
            jax.ShapeDtypeStruct((B, L, MAIN_COLS), F32),
            jax.ShapeDtypeStruct((B, L, LANES), F32),
        ],
        scratch_shapes=[pltpu.VMEM((tm, D), BF16)],
        compiler_params=_cparams(("parallel", "parallel", "arbitrary")),
        name="inproj",
    )(x, shift, scale, g, w_main, w_ba)


def _rope_tables(L):
    t = np.arange(L)
    row_pos, col_pos = t // GRID_W, t % GRID_W
    lane = np.arange(LANES)
    axis = (lane % 64) // 32
    f = lane % 16
    inv = 1.0 / (ROPE_BASE ** (f.astype(np.float32) / 16.0))
    pos = np.where(axis[None, :] == 0, row_pos[:, None], col_pos[:, None]).astype(np.float32)
    ang = jnp.asarray(pos) * jnp.asarray(inv.astype(np.float32))[None, :]
    first = jnp.asarray(((lane % 32) < 16)[None, :])
    return jnp.cos(ang), jnp.where(first, -jnp.sin(ang), jnp.sin(ang))


def _rope_kernel(q_ref, k_ref, cos_ref, sin_ref, qo_ref, ko_ref):
    lane = lax.broadcasted_iota(jnp.int32, (1, LANES), 1)
    first = (lane % 32) < 16
    c, s = cos_ref[...], sin_ref[...]

    def rope(x):
        partner = jnp.where(first, pltpu.roll(x, LANES - 16, 1), pltpu.roll(x, 16, 1))
        return x * c + partner * s

    for h in range(B_HEADS):
        sl = slice(h * LANES, (h + 1) * LANES)
        qo_ref[0, :, sl] = (rope(q_ref[0, :, sl]) * (B_DH ** -0.5)).astype(BF16)
        ko_ref[0, :, sl] = rope(k_ref[0, :, sl]).astype(BF16)


def _rope(proj, cos_t, sin_t, tr):
    B, L, _ = proj.shape
    W = B_HEADS * LANES
    return pl.pallas_call(
        _rope_kernel,
        grid=(B, L // tr),
        in_specs=[
            pl.BlockSpec((1, tr, W), lambda b, i: (b, i, COL_B_Q * LANES // W)),
            pl.BlockSpec((1, tr, W), lambda b, i: (b, i, COL_B_K * LANES // W)),
            pl.BlockSpec((tr, LANES), lambda b, i: (i, 0)),
            pl.BlockSpec((tr, LANES), lambda b, i: (i, 0)),
        ],
        out_specs=[pl.BlockSpec((1, tr, W), lambda b, i: (b, i, 0))] * 2,
        out_shape=[jax.ShapeDtypeStruct((B, L, W), BF16)] * 2,
        compiler_params=_cparams(("parallel", "parallel")),
        name="rope",
    )(proj, proj, cos_t, sin_t)


def _softmax_pv(q, key_vals):
    scores = [_dot_nt(q, k) for k, _ in key_vals]
    m = functools.reduce(jnp.maximum, [s.max(axis=-1, keepdims=True) for s in scores])
    es = [jnp.exp(s - m) for s in scores]
    denom = functools.reduce(jnp.add, [e.sum(axis=-1, keepdims=True) for e in es])
    o = functools.reduce(jnp.add, [_dot(e.astype(BF16), v) for e, (_, v) in zip(es, key_vals)])
    return o / denom


def _diff_finish(o0, o1, lam, g, out_scale):
    o = o0 - lam * o1
    return o * lax.rsqrt(jnp.mean(o * o, axis=-1, keepdims=True) + EPS) * g * out_scale


def _diff_lat_kernel(lam_ref, q_ref, kl_ref, vl_ref, kc_ref, vc_ref, g_ref, o_ref, *, out_scale):
    lane = lax.broadcasted_iota(jnp.int32, (1, LANES), 1)
    low = lane < B_DH
    q = q_ref[0]
    kv = [(kl_ref[0], vl_ref[0].astype(BF16)), (kc_ref[0].astype(BF16), vc_ref[0].astype(BF16))]
    o0 = _softmax_pv(jnp.where(low, q, 0).astype(BF16), kv)
    o1 = _softmax_pv(jnp.where(low, 0, q).astype(BF16), kv)
    o_ref[0] = _diff_finish(o0, o1, lam_ref[0], g_ref[...], out_scale)


def _diff_lat(lam, q_r, k_r, proj_lat, proj_ctx, subln_g, out_scale, tq):
    B, L, _ = proj_lat.shape
    Lc = proj_ctx.shape[1]
    return pl.pallas_call(
        functools.partial(_diff_lat_kernel, out_scale=out_scale),
        grid=(B, B_HEADS, L // tq),
        in_specs=[
            pl.BlockSpec(memory_space=pltpu.SMEM),
            pl.BlockSpec((1, tq, LANES), lambda b, h, i: (b, i, h)),
            pl.BlockSpec((1, L, LANES), lambda b, h, i: (b, 0, h)),
            pl.BlockSpec((1, L, LANES), lambda b, h, i: (b, 0, COL_B_V + h)),
            pl.BlockSpec((1, Lc, LANES), lambda b, h, i: (b, 0, COL_B_K + h)),
            pl.BlockSpec((1, Lc, LANES), lambda b, h, i: (b, 0, COL_B_V + h)),
            pl.BlockSpec((1, LANES), lambda b, h, i: (0, 0)),
        ],
        out_specs=pl.BlockSpec((1, tq, LANES), lambda b, h, i: (b, i, h)),
        out_shape=jax.ShapeDtypeStruct((B, L, B_HEADS * LANES), F32),
        compiler_params=_cparams(("parallel", "parallel", "parallel")),
        name="diff_lat",
    )(lam, q_r, k_r, proj_lat, proj_ctx, proj_ctx, subln_g)


def _diff_ctx_kernel(lam_ref, q_ref, k_ref, v_ref, g_ref, o_ref, *, out_scale):
    lane = lax.broadcasted_iota(jnp.int32, (1, LANES), 1)
    low = lane < B_DH
    q = q_ref[0] * (B_DH ** -0.5)
    kv = [(k_ref[0].astype(BF16), v_ref[0].astype(BF16))]
    o0 = _softmax_pv(jnp.where(low, q, 0).astype(BF16), kv)
    o1 = _softmax_pv(jnp.where(low, 0, q).astype(BF16), kv)
    o_ref[0] = _diff_finish(o0, o1, lam_ref[0], g_ref[...], out_scale)


def _diff_ctx(lam, proj_ctx, subln_g, out_scale):
    B, Lc, _ = proj_ctx.shape
    return pl.pallas_call(
        functools.partial(_diff_ctx_kernel, out_scale=out_scale),
        grid=(B, B_HEADS),
        in_specs=[
            pl.BlockSpec(memory_space=pltpu.SMEM),
            pl.BlockSpec((1, Lc, LANES), lambda b, h: (b, 0, COL_B_Q + h)),
            pl.BlockSpec((1, Lc, LANES), lambda b, h: (b, 0, COL_B_K + h)),
            pl.BlockSpec((1, Lc, LANES), lambda b, h: (b, 0, COL_B_V + h)),
            pl.BlockSpec((1, LANES), lambda b, h: (0, 0)),
        ],
        out_specs=pl.BlockSpec((1, Lc, LANES), lambda b, h: (b, 0, h)),
        out_shape=jax.ShapeDtypeStruct((B, Lc, B_HEADS * LANES), F32),
        compiler_params=_cparams(("parallel", "parallel")),
        name="diff_ctx",
    )(lam, proj_ctx, proj_ctx, proj_ctx, subln_g)


NA_ROWS_PER_STEP = 8
NA_KEYS = WIN_R * GRID_W


def _natten_bias(rpb):
    col = np.arange(GRID_W)
    col_start = np.clip(col - WIN_C // 2, 0, GRID_W - WIN_C)
    col_mask = (col[None, :] >= col_start[:, None]) & (col[None, :] < col_start[:, None] + WIN_C)
    dc = np.clip(col[None, :] - col[:, None], -(WIN_C - 1), WIN_C - 1) + WIN_C - 1
    bias = jnp.where(jnp.asarray(col_mask), rpb.astype(F32)[:, :, dc], NEG_INF)
    return jnp.concatenate([bias[:, :-1], bias[:, 1:]], axis=-1)


def _natten_kernel(q_ref, k_ref, v_ref, kc_ref, vc_ref, bias_ref, o_ref, *, rows):
    lane = lax.broadcasted_iota(jnp.int32, (1, LANES), 1)
    low = lane < C_DH
    kc = kc_ref[0].astype(BF16)
    vc = vc_ref[0].astype(BF16)
    for rr in range(NA_ROWS_PER_STEP):
        r = pl.program_id(2) * NA_ROWS_PER_STEP + rr
        rs = jnp.clip(r - WIN_R // 2, 0, rows - WIN_R)
        cfg = r - rs
        start = pl.multiple_of(rs * GRID_W, GRID_W)
        kw = k_ref[0, pl.ds(start, NA_KEYS), :].astype(BF16)
        vw = v_ref[0, pl.ds(start, NA_KEYS), :].astype(BF16)
        q = q_ref[0, rr * GRID_W:(rr + 1) * GRID_W, :] * (C_DH ** -0.5)
        outs = []
        for hh in range(2):
            qm = (jnp.where(low, q, 0) if hh == 0 else jnp.where(low, 0, q)).astype(BF16)
            bias = jnp.concatenate([bias_ref[hh, WIN_R - 1 - cfg + j] for j in range(0, WIN_R, 2)], axis=-1)
            s_lat = _dot_nt(qm, kw) + bias
            s_ctx = _dot_nt(qm, kc)
            m = jnp.maximum(s_lat.max(axis=-1, keepdims=True), s_ctx.max(axis=-1, keepdims=True))
            e_lat, e_ctx = jnp.exp(s_lat - m), jnp.exp(s_ctx - m)
            denom = e_lat.sum(axis=-1, keepdims=True) + e_ctx.sum(axis=-1, keepdims=True)
            outs.append((_dot(e_lat.astype(BF16), vw) + _dot(e_ctx.astype(BF16), vc)) / denom)
        o_ref[0, rr * GRID_W:(rr + 1) * GRID_W, :] = jnp.where(low, outs[0], outs[1])


def _natten_lat(proj_lat, proj_ctx, bias):
    B, L, _ = proj_lat.shape
    Lc = proj_ctx.shape[1]
    rows = L // GRID_W
    HP = C_HEADS // 2
    tq = NA_ROWS_PER_STEP * GRID_W
    return pl.pallas_call(
        functools.partial(_natten_kernel, rows=rows),
        grid=(B, HP, rows // NA_ROWS_PER_STEP),
        in_specs=[
            pl.BlockSpec((1, tq, LANES), lambda b, h, i: (b, i, COL_C_Q + h)),
            pl.BlockSpec((1, L, LANES), lambda b, h, i: (b, 0, COL_C_K + h)),
            pl.BlockSpec((1, L, LANES), lambda b, h, i: (b, 0, COL_C_V + h)),
            pl.BlockSpec((1, Lc, LANES), lambda b, h, i: (b, 0, COL_C_K + h)),
            pl.BlockSpec((1, Lc, LANES), lambda b, h, i: (b, 0, COL_C_V + h)),
            pl.BlockSpec((2, 2 * WIN_R - 2, GRID_W, 2 * GRID_W), lambda b, h, i: (h, 0, 0, 0)),
        ],
        out_specs=pl.BlockSpec((1, tq, LANES), lambda b, h, i: (b, i, h)),
        out_shape=jax.ShapeDtypeStruct((B, L, HP * LANES), F32),
        compiler_params=_cparams(("parallel", "parallel", "parallel")),
        name="natten_lat",
    )(proj_lat, proj_lat, proj_lat, proj_ctx, proj_ctx, bias)


def _natten_ctx_kernel(q_ref, k_ref, v_ref, o_ref):
    lane = lax.broadcasted_iota(jnp.int32, (1, LANES), 1)
    low = lane < C_DH
    q = q_ref[0] * (C_DH ** -0.5)
    kv = [(k_ref[0].astype(BF16), v_ref[0].astype(BF16))]
    o0 = _softmax_pv(jnp.where(low, q, 0).astype(BF16), kv)
    o1 = _softmax_pv(jnp.where(low, 0, q).astype(BF16), kv)
    o_ref[0] = jnp.where(low, o0, o1)


def _natten_ctx(proj_ctx):
    B, Lc, _ = proj_ctx.shape
    HP = C_HEADS // 2
    return pl.pallas_call(
        _natten_ctx_kernel,
        grid=(B, HP),
        in_specs=[
            pl.BlockSpec((1, Lc, LANES), lambda b, h: (b, 0, COL_C_Q + h)),
            pl.BlockSpec((1, Lc, LANES), lambda b, h: (b, 0, COL_C_K + h)),
            pl.BlockSpec((1, Lc, LANES), lambda b, h: (b, 0, COL_C_V + h)),
        ],
        out_specs=pl.BlockSpec((1, Lc, LANES), lambda b, h: (b, 0, h)),
        out_shape=jax.ShapeDtypeStruct((B, Lc, HP * LANES), F32),
        compiler_params=_cparams(("parallel", "parallel")),
        name="natten_ctx",
    )(proj_ctx, proj_ctx, proj_ctx)


def _merge_kernel(ya_ref, yb_ref, yc_ref, g0_ref, g1_ref, g2_ref, wup_ref, wout_ref, x_ref, gt_ref, o_ref):
    acc = None
    for n, (y_ref, g_ref) in enumerate(((ya_ref, g0_ref), (yb_ref, g1_ref), (yc_ref, g2_ref))):
        up = _dot(y_ref[0].astype(BF16), wup_ref[n])
        t = jax.nn.sigmoid(g_ref[0]) * up
        acc = t if acc is None else acc + t
    r = _dot(acc.astype(BF16), wout_ref[...])
    o_ref[0] = x_ref[0] + gt_ref[0] * r


def _merge(ya, yb, yc, proj, w_up, w_out, x, gate, tm):
    B, L, D = x.shape
    gcol = COL_GATE * LANES // D
    yspec = pl.BlockSpec((1, tm, BRANCH_W), lambda b, i: (b, i, 0))
    return pl.pallas_call(
        _merge_kernel,
        grid=(B, L // tm),
        in_specs=[
            yspec, yspec, yspec,
            pl.BlockSpec((1, tm, D), lambda b, i: (b, i, gcol)),
            pl.BlockSpec((1, tm, D), lambda b, i: (b, i, gcol + 1)),
            pl.BlockSpec((1, tm, D), lambda b, i: (b, i, gcol + 2)),
            pl.BlockSpec((N_BRANCH, BRANCH_W, D), lambda b, i: (0, 0, 0)),
            pl.BlockSpec((D, D), lambda b, i: (0, 0)),
            pl.BlockSpec((1, tm, D), lambda b, i: (b, i, 0)),
            pl.BlockSpec((1, 1, D), lambda b, i: (b, 0, 0)),
        ],
        out_specs=pl.BlockSpec((1, tm, D), lambda b, i: (b, i, 0)),
        out_shape=jax.ShapeDtypeStruct((B, L, D), F32),
        compiler_params=_cparams(("parallel", "parallel")),
        name="merge",
    )(ya, yb, yc, proj, proj, proj, w_up, w_out, x, gate)


PEER_CAND = P_TOPK + 1
PEER_A_PAD = 24
PEER_CAND_ROWS = PEER_A_PAD + 7 * 8 + 16


def _extract_top(s, n):
    vals = []
    for _ in range(n):
        m = jnp.max(s, axis=0, keepdims=True)
        vals.append(m)
        s = jnp.where(s == m, NEG_INF, s)
    return vals


def _peerq_kernel(x_ref, sh_ref, sc_ref, g_ref, wq_ref, keys_ref,
                  xn_ref, s2_ref, tau_ref, e1_ref, e2_ref, ab_ref, cand_ref):
    x = x_ref[0]
    tm = x.shape[0]
    y = x * lax.rsqrt(jnp.mean(x * x, axis=-1, keepdims=True) + EPS) * g_ref[...]
    xn = (y * (1.0 + sc_ref[0]) + sh_ref[0]).astype(BF16)
    xn_ref[...] = xn
    q = _dot(xn, wq_ref[...]).astype(BF16)
    row = lax.broadcasted_iota(jnp.int32, (PEER_A_PAD, 1), 0)
    row8 = lax.broadcasted_iota(jnp.int32, (8, 1), 0)
    row16 = lax.broadcasted_iota(jnp.int32, (16, 1), 0)
    for h in range(P_HEADS):
        s = []
        for p in range(2):
            hp = 2 * h + p
            st = _dot_nt(keys_ref[hp], q[:, hp * P_DKH:(hp + 1) * P_DKH])
            s.append(st)
            ab_ref[p] = jnp.full((PEER_A_PAD, tm), NEG_INF, F32)
            for r, m in enumerate(_extract_top(st, PEER_CAND)):
                ab_ref[p, r:r + 1, :] = m
        a_all, b_all = ab_ref[0], ab_ref[1]
        cand_ref[0:PEER_A_PAD, :] = a_all[0:1, :] + b_all
        for i in range(1, 8):
            n_i = PEER_CAND // (i + 1)
            cand_ref[PEER_A_PAD + 8 * (i - 1):PEER_A_PAD + 8 * i, :] = jnp.where(
                row8 < n_i, a_all[i:i + 1, :] + b_all[0:8, :], NEG_INF)
        cand_ref[PEER_A_PAD + 56:PEER_A_PAD + 72, :] = jnp.where(
            row16 + 8 < PEER_CAND, a_all[8:24, :] + b_all[0:1, :], NEG_INF)
        top = _extract_top(cand_ref[...], PEER_CAND)
        z = functools.reduce(jnp.add, [jnp.exp(v - top[0]) for v in top[:P_TOPK]])
        thr = 0.5 * (top[P_TOPK - 1] + top[P_TOPK])
        s2_ref[h] = s[1]
        tau_ref[h] = thr - s[0]
        e1_ref[h] = jnp.exp(s[0] - a_all[0:1, :])
        e2_ref[h] = jnp.exp(s[1] - b_all[0:1, :]) / z


def _peerq(x, shift, scale, g, wq, keys, tm):
    B, L, D = x.shape
    T = B * L
    nb = L // tm
    tok_spec = pl.BlockSpec((P_HEADS, N_KEYS, tm), lambda b, i: (0, 0, b * nb + i))
    tok_shape = jax.ShapeDtypeStruct((P_HEADS, N_KEYS, T), F32)
    return pl.pallas_call(
        _peerq_kernel,
        grid=(B, nb),
        in_specs=[
            pl.BlockSpec((1, tm, D), lambda b, i: (b, i, 0)),
            pl.BlockSpec((1, 1, D), lambda b, i: (b, 0, 0)),
            pl.BlockSpec((1, 1, D), lambda b, i: (b, 0, 0)),
            pl.BlockSpec((1, D), lambda b, i: (0, 0)),
            pl.BlockSpec((D, 2 * P_HEADS * P_DKH), lambda b, i: (0, 0)),
            pl.BlockSpec((2 * P_HEADS, N_KEYS, P_DKH), lambda b, i: (0, 0, 0)),
        ],
        out_specs=[pl.BlockSpec((tm, D), lambda b, i: (b * nb + i, 0))] + [tok_spec] * 4,
        out_shape=[jax.ShapeDtypeStruct((T, D), BF16)] + [tok_shape] * 4,
        scratch_shapes=[pltpu.VMEM((2, PEER_A_PAD, tm), F32), pltpu.VMEM((PEER_CAND_ROWS, tm), F32)],
        compiler_params=_cparams(("parallel", "parallel")),
        name="peer_query",
    )(x, shift, scale, g, wq, keys)


PEER_EC = 1024
PEER_TT = 512


def _gelu(x):
    return 0.5 * x * (1.0 + lax.erf(x * math.sqrt(0.5)))


def _peer_kernel(xn_ref, u_ref, vt_ref, s2_ref, tau_ref, e1_ref, e2_ref, x_ref, gt_ref, fg_ref,
                 o_ref, acc_ref, pt_ref, *, final_norm):
    c = pl.program_id(1)

    @pl.when(c == 0)
    def _():
        acc_ref[...] = jnp.zeros_like(acc_ref)

    act = _gelu(_dot_nt(u_ref[...], xn_ref[...]))
    for aa in range(PEER_EC // N_KEYS):
        a = c * (PEER_EC // N_KEYS) + aa
        gate = None
        for h in range(P_HEADS):
            tau = tau_ref[h, pl.ds(a, 1), :]
            e1 = e1_ref[h, pl.ds(a, 1), :]
            t = e1 * jnp.where(s2_ref[h] >= tau, e2_ref[h], 0.0)
            gate = t if gate is None else gate + t
        pt_ref[aa * N_KEYS:(aa + 1) * N_KEYS, :] = (gate * act[aa * N_KEYS:(aa + 1) * N_KEYS, :]).astype(BF16)
    acc_ref[...] += _dot(vt_ref[...], pt_ref[...])

    @pl.when(c == pl.num_programs(1) - 1)
    def _():
        y = x_ref[...] + gt_ref[0] * acc_ref[...].T
        if final_norm:
            y = y * lax.rsqrt(jnp.mean(y * y, axis=-1, keepdims=True) + EPS) * fg_ref[...]
        o_ref[...] = y


def _peer(xn, u, vt, s2, tau, e1, e2, x, gate, final_g, final_norm):
    B, L, D = x.shape
    T = B * L
    TT = min(PEER_TT, L)
    per_b = L // TT
    tok_spec = pl.BlockSpec((P_HEADS, N_KEYS, TT), lambda i, c: (0, 0, i))
    out = pl.pallas_call(
        functools.partial(_peer_kernel, final_norm=final_norm),
        grid=(T // TT, N_EXPERTS // PEER_EC),
        in_specs=[
            pl.BlockSpec((TT, D), lambda i, c: (i, 0)),
            pl.BlockSpec((PEER_EC, D), lambda i, c: (c, 0)),
            pl.BlockSpec((D, PEER_EC), lambda i, c: (0, c)),
            tok_spec, tok_spec, tok_spec, tok_spec,
            pl.BlockSpec((TT, D), lambda i, c: (i, 0)),
            pl.BlockSpec((1, 1, D), lambda i, c: (i // per_b, 0, 0)),
            pl.BlockSpec((1, D), lambda i, c: (0, 0)),
        ],
        out_specs=pl.BlockSpec((TT, D), lambda i, c: (i, 0)),
        out_shape=jax.ShapeDtypeStruct((T, D), F32),
        scratch_shapes=[pltpu.VMEM((D, TT), F32), pltpu.VMEM((PEER_EC, TT), BF16)],
        compiler_params=_cparams(("parallel", "arbitrary")),
        name="peer_experts",
    )(xn, u, vt, s2, tau, e1, e2, x.reshape(T, D), gate, final_g)
    return out.reshape(B, L, D)


GDN_CONV_TILE = 512
GDN_PAD = 8


def _gdn_prep_kernel(q_ref, k_ref, v_ref, cwq_ref, cwk_ref, cwv_ref, qo_ref, ko_ref, vo_ref, pad_ref):
    Ls = q_ref.shape[1]
    T = min(GDN_CONV_TILE, Ls)
    zeros = jnp.zeros((GDN_PAD, LANES), F32)
    pad_ref[0:GDN_PAD, :] = zeros
    pad_ref[GDN_PAD + Ls:2 * GDN_PAD + Ls, :] = zeros
    for which, (x_ref, cw_ref, o_ref) in enumerate(((q_ref, cwq_ref, qo_ref), (k_ref, cwk_ref, ko_ref),
                                                    (v_ref, cwv_ref, vo_ref))):
        pad_ref[GDN_PAD:GDN_PAD + Ls, :] = x_ref[0]
        for t0 in range(0, Ls, T):
            acc = None
            for j in range(CONV_K):
                off = GDN_PAD - CONV_K // 2 + j + t0
                t = cw_ref[j:j + 1, :] * pad_ref[off:off + T, :]
                acc = t if acc is None else acc + t
            y = acc * jax.nn.sigmoid(acc)
            if which < 2:
                y = y * lax.rsqrt(jnp.sum(y * y, axis=-1, keepdims=True) + EPS)
            if which == 0:
                y = y * (A_DK ** -0.5)
            o_ref[0, t0:t0 + T, :] = y


def _gdn_prep(proj, conv_w):
    B, Ls, _ = proj.shape
    seq = lambda col: pl.BlockSpec((1, Ls, LANES), lambda b, h: (b, 0, col + h))
    cw = lambda col: pl.BlockSpec((CONV_K, LANES), lambda b, h: (0, col + h))
    out = pl.BlockSpec((1, Ls, LANES), lambda b, h: (b, 0, h))
    shape = jax.ShapeDtypeStruct((B, Ls, A_HEADS * LANES), F32)
    return pl.pallas_call(
        _gdn_prep_kernel,
        grid=(B, A_HEADS),
        in_specs=[seq(COL_A_Q), seq(COL_A_K), seq(COL_A_V), cw(0), cw(A_HEADS), cw(2 * A_HEADS)],
        out_specs=[out] * 3,
        out_shape=[shape] * 3,
        scratch_shapes=[pltpu.VMEM((Ls + 2 * GDN_PAD, LANES), F32)],
        compiler_params=_cparams(("parallel", "parallel")),
        name="gdn_prep",
    )(proj, proj, proj, conv_w, conv_w, conv_w)


def _softplus(x):
    return jnp.maximum(x, 0.0) + jnp.log1p(jnp.exp(-jnp.abs(x)))


def _gdn_chunk_prepass(h, q, k, v, ba, nega_ref, dtb_ref, scr, row0, chunk_id):
    u_scr, w_scr, kd_scr, qd_scr, qk_scr, dl_scr = scr
    C = CHUNK
    row = lax.broadcasted_iota(jnp.int32, (C, C), 0)
    col = lax.broadcasted_iota(jnp.int32, (C, C), 1)
    lane = lax.broadcasted_iota(jnp.int32, (1, LANES), 1)
    eye = row == col
    qb, kb16 = q.astype(BF16), k.astype(BF16)
    for d in range(2):
        before_eq_rc = (col <= row) if d == 0 else (col >= row)
        before_rc = (col < row) if d == 0 else (col > row)
        before_eq_cr = (row <= col) if d == 0 else (row >= col)
        bcol = jnp.sum(jnp.where(lane == d * A_HEADS + h, ba, 0.0), axis=-1, keepdims=True)
        acol = jnp.sum(jnp.where(lane == (2 + d) * A_HEADS + h, ba, 0.0), axis=-1, keepdims=True)
        beta = jax.nn.sigmoid(bcol)
        g = nega_ref[d, h] * _softplus(acol + dtb_ref[d, h])
        g_cols = jnp.broadcast_to(g, (C, C))
        gc_row = jnp.sum(jnp.where(before_eq_cr, g_cols, 0.0), axis=0, keepdims=True)
        g_row = jnp.sum(jnp.where(eye, g_cols, 0.0), axis=0, keepdims=True)
        gc_col = jnp.sum(jnp.where(before_eq_rc, jnp.broadcast_to(g_row, (C, C)), 0.0), axis=-1, keepdims=True)
        g_total = jnp.sum(g_row, axis=-1, keepdims=True)
        decay = jnp.exp(jnp.where(before_eq_rc, gc_col - gc_row, NEG_INF))
        kbeta = k * beta
        a_mat = jnp.where(before_rc, _dot_nt(kbeta.astype(BF16), kb16) * decay, 0.0)
        qk = jnp.where(before_eq_rc, _dot_nt(qb, kb16) * decay, 0.0)
        n_pow = -a_mat
        inv = jnp.where(eye, 1.0, 0.0) + n_pow
        for _ in range(5):
            n_pow = _dot(n_pow.astype(BF16), n_pow.astype(BF16))
            inv = inv + _dot(inv.astype(BF16), n_pow.astype(BF16))
        e_gc = jnp.exp(gc_col)
        rhs = jnp.concatenate([v * beta, kbeta * e_gc], axis=-1).astype(BF16)
        sol = _dot(inv.astype(BF16), rhs)
        rows = pl.ds(row0, C)
        u_scr[d, rows, :] = sol[:, :A_DV]
        w_scr[d, rows, :] = sol[:, A_DV:].astype(BF16)
        kd_scr[d, rows, :] = (k * jnp.exp(g_total - gc_col)).astype(BF16)
        qd_scr[d, rows, :] = (q * e_gc).astype(BF16)
        qk_scr[d, rows, :] = qk.astype(BF16)
        dl_scr[d, pl.ds(chunk_id * 8, 8), :] = jnp.broadcast_to(jnp.exp(g_total), (8, LANES))


def _gdn_scan_step(d, S, scr, o_scr, row0, chunk_id):
    u_scr, w_scr, kd_scr, qd_scr, qk_scr, dl_scr = scr
    rows = pl.ds(row0, CHUNK)
    Sb = S.astype(BF16)
    v_new = u_scr[d, rows, :] - _dot(w_scr[d, rows, :], Sb)
    vb = v_new.astype(BF16)
    o_scr[d, rows, :] = _dot(qd_scr[d, rows, :], Sb) + _dot(qk_scr[d, rows, :], vb)
    return S * dl_scr[d, pl.ds(chunk_id * 8, 1), :] + _dot_tn(kd_scr[d, rows, :], vb)


def _gdn_scan_kernel(nega_ref, dtb_ref, qc_ref, kc_ref, vc_ref, bac_ref, zc_ref, ql_ref, kl_ref, vl_ref, bal_ref,
                     zl_ref, ng_ref, yl_ref, yc_ref, u_scr, w_scr, kd_scr, qd_scr, qk_scr, dl_scr, o_scr):
    h = pl.program_id(1)
    Lc, L = qc_ref.shape[1], ql_ref.shape[1]
    nc, nl = Lc // CHUNK, L // CHUNK
    scr = (u_scr, w_scr, kd_scr, qd_scr, qk_scr, dl_scr)

    for c in range(nc):
        r = c * CHUNK
        _gdn_chunk_prepass(h, qc_ref[0, r:r + CHUNK, :], kc_ref[0, r:r + CHUNK, :], vc_ref[0, r:r + CHUNK, :],
                           bac_ref[0, r:r + CHUNK, :], nega_ref, dtb_ref, scr, r, c)

    def pre_body(c, carry):
        r = pl.multiple_of(c * CHUNK, CHUNK)
        rows = pl.ds(r, CHUNK)
        _gdn_chunk_prepass(h, ql_ref[0, rows, :], kl_ref[0, rows, :], vl_ref[0, rows, :], bal_ref[0, rows, :],
                           nega_ref, dtb_ref, scr, Lc + r, nc + c)
        return carry

    lax.fori_loop(0, nl, pre_body, 0)

    S_f = jnp.zeros((A_DK, A_DV), F32)
    S_b = jnp.zeros((A_DK, A_DV), F32)
    for s in range(nc):
        S_f = _gdn_scan_step(0, S_f, scr, o_scr, s * CHUNK, s)
        S_b = _gdn_scan_step(1, S_b, scr, o_scr, (nc - 1 - s) * CHUNK, nc - 1 - s)

    def scan_body(s, carry):
        S_f, S_b = carry
        cf = s
        cb = nl - 1 - s
        S_f = _gdn_scan_step(0, S_f, scr, o_scr, pl.multiple_of(Lc + cf * CHUNK, CHUNK), nc + cf)
        S_b = _gdn_scan_step(1, S_b, scr, o_scr, pl.multiple_of(Lc + cb * CHUNK, CHUNK), nc + cb)
        return S_f, S_b

    lax.fori_loop(0, nl, scan_body, (S_f, S_b))

    def finish(z_ref, y_ref, base, n):
        T = min(GDN_CONV_TILE, n)
        for t0 in range(0, n, T):
            o = o_scr[0, base + t0:base + t0 + T, :] + o_scr[1, base + t0:base + t0 + T, :]
            y = o * lax.rsqrt(jnp.mean(o * o, axis=-1, keepdims=True) + EPS) * ng_ref[...]
            z = z_ref[0, t0:t0 + T, :]
            y_ref[0, t0:t0 + T, :] = y * (z * jax.nn.sigmoid(z))

    finish(zl_ref, yl_ref, Lc, L)
    finish(zc_ref, yc_ref, 0, Lc)


def _gdn_scan(nega, dtb, qkv_ctx, ba_ctx, proj_ctx, qkv_lat, ba_lat, proj_lat, norm_g):
    B, L, _ = proj_lat.shape
    Lc = proj_ctx.shape[1]
    Lt = L + Lc
    head = lambda n: pl.BlockSpec((1, n, LANES), lambda b, h: (b, 0, h))
    full = lambda n: pl.BlockSpec((1, n, LANES), lambda b, h: (b, 0, 0))
    zcol = lambda n: pl.BlockSpec((1, n, LANES), lambda b, h: (b, 0, COL_A_Z + h))
    smem = pl.BlockSpec(memory_space=pltpu.SMEM)
    return pl.pallas_call(
        _gdn_scan_kernel,
        grid=(B, A_HEADS),
        in_specs=[smem, smem,
                  head(Lc), head(Lc), head(Lc), full(Lc), zcol(Lc),
                  head(L), head(L), head(L), full(L), zcol(L),
                  pl.BlockSpec((1, LANES), lambda b, h: (0, 0))],
        out_specs=[head(L), head(Lc)],
        out_shape=[jax.ShapeDtypeStruct((B, L, A_HEADS * LANES), F32),
                   jax.ShapeDtypeStruct((B, Lc, A_HEADS * LANES), F32)],
        scratch_shapes=[
            pltpu.VMEM((2, Lt, A_DV), F32),
            pltpu.VMEM((2, Lt, A_DK), BF16),
            pltpu.VMEM((2, Lt, A_DK), BF16),
            pltpu.VMEM((2, Lt, A_DK), BF16),
            pltpu.VMEM((2, Lt, CHUNK), BF16),
            pltpu.VMEM((2, Lt // CHUNK * 8, LANES), F32),
            pltpu.VMEM((2, Lt, A_DV), F32),
        ],
        compiler_params=_cparams(("parallel", "parallel")),
        name="gdn_scan",
    )(nega, dtb, *qkv_ctx, ba_ctx, proj_ctx, *qkv_lat, ba_lat, proj_lat, norm_g)


def _gdn_pallas(proj_lat, ba_lat, proj_ctx, ba_ctx, conv_w, a_log, dt_bias, norm_g):
    qkv_lat = _gdn_prep(proj_lat, conv_w)
    qkv_ctx = _gdn_prep(proj_ctx, conv_w)
    return _gdn_scan(-jnp.exp(a_log), dt_bias, qkv_ctx, ba_ctx, proj_ctx, qkv_lat, ba_lat, proj_lat, norm_g[None, :])


def _layer(i, x, xc, c, c_ctx, p, cos_t, sin_t, ctx_out, final_g, final_norm):
    B, L, D = x.shape
    Lc = xc.shape[1]
    mod_lat = (jax.nn.silu(c) @ p['ada_w'] + p['ada_b'])[:, None, :]
    mod_ctx = jnp.broadcast_to((jax.nn.silu(c_ctx) @ p['ada_w'] + p['ada_b'])[None, None, :], (B, 1, 6 * D))
    sh1, sc1, gt1, sh2, sc2, gt2 = jnp.split(mod_lat, 6, axis=-1)
    csh1, csc1, cgt1, csh2, csc2, cgt2 = jnp.split(mod_ctx, 6, axis=-1)
    lam_init = 0.8 - 0.6 * math.exp(-0.3 * i)

    w_in = p['w_in']
    n_a = 4 * A_HEADS * A_DK
    w_main = jnp.concatenate([w_in[:, :n_a], w_in[:, n_a + 4 * A_HEADS:]], axis=1).astype(BF16)
    w_ba = jnp.pad(w_in[:, n_a:n_a + 4 * A_HEADS], ((0, 0), (0, LANES - 4 * A_HEADS))).astype(BF16)
    n1 = p['norm1_g'][None, :]
    proj_lat, ba_lat = _inproj(x, sh1, sc1, n1, w_main, w_ba, 512)
    proj_ctx, ba_ctx = _inproj(xc, csh1, csc1, n1, w_main, w_ba, Lc)

    ya, ya_c = _gdn_pallas(proj_lat, ba_lat, proj_ctx, ba_ctx, p['gdn_conv'], p['gdn_a_log'], p['gdn_dt_bias'],
                           p['gdn_norm_g'])

    lp = p['diff_lambda']
    lam = (jnp.exp(jnp.sum(lp[0] * lp[1])) - jnp.exp(jnp.sum(lp[2] * lp[3])) + lam_init).reshape(1)
    q_r, k_r = _rope(proj_lat, cos_t, sin_t, 512)
    subln = p['diff_subln_g'][None, :]
    yb = _diff_lat(lam, q_r, k_r, proj_lat, proj_ctx, subln, 1.0 - lam_init, 256)
    yb_c = _diff_ctx(lam, proj_ctx, subln, 1.0 - lam_init) if ctx_out else None

    yc = _natten_lat(proj_lat, proj_ctx, _natten_bias(p['na_rpb']))
    yc_c = _natten_ctx(proj_ctx) if ctx_out else None

    w_up = p['w_up'].astype(BF16)
    w_out = p['w_out'].astype(BF16)
    wq = p['peer_wq'].astype(BF16)
    keys = p['peer_keys'].reshape(2 * P_HEADS, N_KEYS, P_DKH).astype(BF16)
    u = p['peer_u'].astype(BF16)
    vt = p['peer_v'].T.astype(BF16)
    n2 = p['norm2_g'][None, :]
    fg = final_g[None, :]

    x = _merge(ya, yb, yc, proj_lat, w_up, w_out, x, gt1, 512)
    pq = _peerq(x, sh2, sc2, n2, wq, keys, 256)
    x = _peer(pq[0], u, vt, pq[1], pq[2], pq[3], pq[4], x, gt2, fg, final_norm)
    if ctx_out:
        xc = _merge(ya_c, yb_c, yc_c, proj_ctx, w_up, w_out, xc, cgt1, Lc)
        pq = _peerq(xc, csh2, csc2, n2, wq, keys, Lc)
        xc = _peer(pq[0], u, vt, pq[1], pq[2], pq[3], pq[4], xc, cgt2, fg, False)
    return x, xc


def kernel(x, c, ctx, c_ctx, norm1_g, norm2_g, ada_w, ada_b, w_in, gdn_conv, gdn_a_log, gdn_dt_bias, gdn_norm_g,
           diff_lambda, diff_subln_g, na_rpb, w_up, w_out, peer_wq, peer_keys, peer_u, peer_v, final_g):
    cos_t, sin_t = _rope_tables(x.shape[1])
    xc = ctx
    for i in range(DEPTH):
        p = dict(norm1_g=norm1_g[i], norm2_g=norm2_g[i], ada_w=ada_w[i], ada_b=ada_b[i], w_in=w_in[i],
                 gdn_conv=gdn_conv[i], gdn_a_log=gdn_a_log[i], gdn_dt_bias=gdn_dt_bias[i], gdn_norm_g=gdn_norm_g[i],
                 diff_lambda=diff_lambda[i], diff_subln_g=diff_subln_g[i], na_rpb=na_rpb[i], w_up=w_up[i],
                 w_out=w_out[i], peer_wq=peer_wq[i], peer_keys=peer_keys[i], peer_u=peer_u[i], peer_v=peer_v[i])
        x, xc = _layer(i, x, xc, c, c_ctx, p, cos_t, sin_t, i < DEPTH - 1, final_g, i == DEPTH - 1)
    return x
```

```python
import functools
import math

import numpy as np
import jax
import jax.numpy as jnp
from jax import lax
from jax.experimental import pallas as pl
from jax.experimental.pallas import tpu as pltpu

F32 = jnp.float32
BF16 = jnp.bfloat16

D_MODEL = 1024
DEPTH = 2
GRID_W = 64
EPS = 1e-6
NEG_INF = -1e30

A_HEADS = 4
A_DK = 128
A_DV = 128
CONV_K = 5
CHUNK = 64
B_HEADS = 4
B_DH = 64
ROPE_BASE = 10000.0
C_HEADS = 8
C_DH = 64
WIN_R = 8
WIN_C = 16
N_BRANCH = 3
BRANCH_W = 512
P_HEADS = 8
N_KEYS = 128
N_EXPERTS = N_KEYS * N_KEYS
P_DKH = 128
P_TOPK = 16

LANES = 128
VMEM_LIMIT = 56 * 1024 * 1024

COL_A_Q, COL_A_K, COL_A_V, COL_A_Z = 0, 4, 8, 12
COL_B_Q, COL_B_K, COL_B_V = 16, 20, 24
COL_C_Q, COL_C_K, COL_C_V = 28, 32, 36
COL_GATE = 40
MAIN_COLS = 64 * LANES


def _cparams(sem):
    return pltpu.CompilerParams(dimension_semantics=sem, vmem_limit_bytes=VMEM_LIMIT)


def _dot(a, b):
    return jnp.dot(a, b, preferred_element_type=F32)


def _dot_nt(a, b):
    return lax.dot_general(a, b, (((1,), (1,)), ((), ())), preferred_element_type=F32)


def _dot_tn(a, b):
    return lax.dot_general(a, b, (((0,), (0,)), ((), ())), preferred_element_type=F32)


def _inproj_kernel(x_ref, sh_ref, sc_ref, g_ref, w_ref, wba_ref, o_ref, oba_ref, hn_ref):
    @pl.when(pl.program_id(2) == 0)
    def _():
        x = x_ref[0]
        y = x * lax.rsqrt(jnp.mean(x * x, axis=-1, keepdims=True) + EPS) * g_ref[...]
        h = (y * (1.0 + sc_ref[0]) + sh_ref[0]).astype(BF16)
        hn_ref[...] = h
        oba_ref[0] = _dot(h, wba_ref[...])

    o_ref[0] = _dot(hn_ref[...], w_ref[...])


def _inproj(x, shift, scale, g, w_main, w_ba, tm):
    B, L, D = x.shape
    tn = 1024
    return pl.pallas_call(
        _inproj_kernel,
        grid=(B, L // tm, MAIN_COLS // tn),
        in_specs=[
            pl.BlockSpec((1, tm, D), lambda b, i, j: (b, i, 0)),
            pl.BlockSpec((1, 1, D), lambda b, i, j: (b, 0, 0)),
            pl.BlockSpec((1, 1, D), lambda b, i, j: (b, 0, 0)),
            pl.BlockSpec((1, D), lambda b, i, j: (0, 0)),
            pl.BlockSpec((D, tn), lambda b, i, j: (0, j)),
            pl.BlockSpec((D, LANES), lambda b, i, j: (0, 0)),
        ],
        out_specs=[
            pl.BlockSpec((1, tm, tn), lambda b, i, j: (b, i, j)),
            pl.BlockSpec((1, tm, LANES), lambda b, i, j: (b, i, 0)),
        ],
        out_shape=[
            jax.ShapeDtypeStruct((B, L, MAIN_COLS), F32),
            jax.ShapeDtypeStruct((B, L, LANES), F32),
        ],
        scratch_shapes=[pltpu.VMEM((tm, D), BF16)],
        compiler_params=_cparams(("parallel", "parallel", "arbitrary")),
        name="inproj",
    )(x, shift, scale, g, w_main, w_ba)


def _rope_tables(L):
    t = np.arange(L)
    row_pos, col_pos = t // GRID_W, t % GRID_W
    lane = np.arange(LANES)
    axis = (lane % 64) // 32
    f = lane % 16
    inv = 1.0 / (ROPE_BASE ** (f.astype(np.float32) / 16.0))
    pos = np.where(axis[None, :] == 0, row_pos[:, None], col_pos[:, None]).astype(np.float32)
    ang = jnp.asarray(pos) * jnp.asarray(inv.astype(np.float32))[None, :]
    first = jnp.asarray(((lane % 32) < 16)[None, :])
    return jnp.cos(ang), jnp.where(first, -jnp.sin(ang), jnp.sin(ang))


def _rope_kernel(q_ref, k_ref, cos_ref, sin_ref, qo_ref, ko_ref):
    lane = lax.broadcasted_iota(jnp.int32, (1, LANES), 1)
    first = (lane % 32) < 16
    c, s = cos_ref[...], sin_ref[...]

    def rope(x):
        partner = jnp.where(first, pltpu.roll(x, LANES - 16, 1), pltpu.roll(x, 16, 1))
        return x * c + partner * s

    for h in range(B_HEADS):
        sl = slice(h * LANES, (h + 1) * LANES)
        qo_ref[0, :, sl] = (rope(q_ref[0, :, sl]) * (B_DH ** -0.5)).astype(BF16)
        ko_ref[0, :, sl] = rope(k_ref[0, :, sl]).astype(BF16)


def _rope(proj, cos_t, sin_t, tr):
    B, L, _ = proj.shape
    W = B_HEADS * LANES
    return pl.pallas_call(
        _rope_kernel,
        grid=(B, L // tr),
        in_specs=[
            pl.BlockSpec((1, tr, W), lambda b, i: (b, i, COL_B_Q * LANES // W)),
            pl.BlockSpec((1, tr, W), lambda b, i: (b, i, COL_B_K * LANES // W)),
            pl.BlockSpec((tr, LANES), lambda b, i: (i, 0)),
            pl.BlockSpec((tr, LANES), lambda b, i: (i, 0)),
        ],
        out_specs=[pl.BlockSpec((1, tr, W), lambda b, i: (b, i, 0))] * 2,
        out_shape=[jax.ShapeDtypeStruct((B, L, W), BF16)] * 2,
        compiler_params=_cparams(("parallel", "parallel")),
        name="rope",
    )(proj, proj, cos_t, sin_t)


def _softmax_pv(q, key_vals):
    scores = [_dot_nt(q, k) for k, _ in key_vals]
    m = functools.reduce(jnp.maximum, [s.max(axis=-1, keepdims=True) for s in scores])
    es = [jnp.exp(s - m) for s in scores]
    denom = functools.reduce(jnp.add, [e.sum(axis=-1, keepdims=True) for e in es])
    o = functools.reduce(jnp.add, [_dot(e.astype(BF16), v) for e, (_, v) in zip(es, key_vals)])
    return o / denom


def _diff_finish(o0, o1, lam, g, out_scale):
    o = o0 - lam * o1
    return o * lax.rsqrt(jnp.mean(o * o, axis=-1, keepdims=True) + EPS) * g * out_scale


def _diff_lat_kernel(lam_ref, q_ref, kl_ref, vl_ref, kc_ref, vc_ref, g_ref, o_ref, *, out_scale):
    lane = lax.broadcasted_iota(jnp.int32, (1, LANES), 1)
    low = lane < B_DH
    q = q_ref[0]
    kv = [(kl_ref[0], vl_ref[0].astype(BF16)), (kc_ref[0].astype(BF16), vc_ref[0].astype(BF16))]
    o0 = _softmax_pv(jnp.where(low, q, 0).astype(BF16), kv)
    o1 = _softmax_pv(jnp.where(low, 0, q).astype(BF16), kv)
    o_ref[0] = _diff_finish(o0, o1, lam_ref[0], g_ref[...], out_scale)


def _diff_lat(lam, q_r, k_r, proj_lat, proj_ctx, subln_g, out_scale, tq):
    B, L, _ = proj_lat.shape
    Lc = proj_ctx.shape[1]
    return pl.pallas_call(
        functools.partial(_diff_lat_kernel, out_scale=out_scale),
        grid=(B, B_HEADS, L // tq),
        in_specs=[
            pl.BlockSpec(memory_space=pltpu.SMEM),
            pl.BlockSpec((1, tq, LANES), lambda b, h, i: (b, i, h)),
            pl.BlockSpec((1, L, LANES), lambda b, h, i: (b, 0, h)),
            pl.BlockSpec((1, L, LANES), lambda b, h, i: (b, 0, COL_B_V + h)),
            pl.BlockSpec((1, Lc, LANES), lambda b, h, i: (b, 0, COL_B_K + h)),
            pl.BlockSpec((1, Lc, LANES), lambda b, h, i: (b, 0, COL_B_V + h)),
            pl.BlockSpec((1, LANES), lambda b, h, i: (0, 0)),
        ],
        out_specs=pl.BlockSpec((1, tq, LANES), lambda b, h, i: (b, i, h)),
        out_shape=jax.ShapeDtypeStruct((B, L, B_HEADS * LANES), F32),
        compiler_params=_cparams(("parallel", "parallel", "parallel")),
        name="diff_lat",
    )(lam, q_r, k_r, proj_lat, proj_ctx, proj_ctx, subln_g)


def _diff_ctx_kernel(lam_ref, q_ref, k_ref, v_ref, g_ref, o_ref, *, out_scale):
    lane = lax.broadcasted_iota(jnp.int32, (1, LANES), 1)
    low = lane < B_DH
    q = q_ref[0] * (B_DH ** -0.5)
    kv = [(k_ref[0].astype(BF16), v_ref[0].astype(BF16))]
    o0 = _softmax_pv(jnp.where(low, q, 0).astype(BF16), kv)
    o1 = _softmax_pv(jnp.where(low, 0, q).astype(BF16), kv)
    o_ref[0] = _diff_finish(o0, o1, lam_ref[0], g_ref[...], out_scale)


def _diff_ctx(lam, proj_ctx, subln_g, out_scale):
    B, Lc, _ = proj_ctx.shape
    return pl.pallas_call(
        functools.partial(_diff_ctx_kernel, out_scale=out_scale),
        grid=(B, B_HEADS),
        in_specs=[
            pl.BlockSpec(memory_space=pltpu.SMEM),
            pl.BlockSpec((1, Lc, LANES), lambda b, h: (b, 0, COL_B_Q + h)),
            pl.BlockSpec((1, Lc, LANES), lambda b, h: (b, 0, COL_B_K + h)),
            pl.BlockSpec((1, Lc, LANES), lambda b, h: (b, 0, COL_B_V + h)),
            pl.BlockSpec((1, LANES), lambda b, h: (0, 0)),
        ],
        out_specs=pl.BlockSpec((1, Lc, LANES), lambda b, h: (b, 0, h)),
        out_shape=jax.ShapeDtypeStruct((B, Lc, B_HEADS * LANES), F32),
        compiler_params=_cparams(("parallel", "parallel")),
        name="diff_ctx",
    )(lam, proj_ctx, proj_ctx, proj_ctx, subln_g)


NA_ROWS_PER_STEP = 8
NA_KEYS = WIN_R * GRID_W


def _natten_bias(rpb):
    col = np.arange(GRID_W)
    col_start = np.clip(col - WIN_C // 2, 0, GRID_W - WIN_C)
    col_mask = (col[None, :] >= col_start[:, None]) & (col[None, :] < col_start[:, None] + WIN_C)
    dc = np.clip(col[None, :] - col[:, None], -(WIN_C - 1), WIN_C - 1) + WIN_C - 1
    bias = jnp.where(jnp.asarray(col_mask), rpb.astype(F32)[:, :, dc], NEG_INF)
    return jnp.concatenate([bias[:, :-1], bias[:, 1:]], axis=-1)


def _natten_kernel(q_ref, k_ref, v_ref, kc_ref, vc_ref, bias_ref, o_ref, *, rows):
    lane = lax.broadcasted_iota(jnp.int32, (1, LANES), 1)
    low = lane < C_DH
    kc = kc_ref[0].astype(BF16)
    vc = vc_ref[0].astype(BF16)
    for rr in range(NA_ROWS_PER_STEP):
        r = pl.program_id(2) * NA_ROWS_PER_STEP + rr
        rs = jnp.clip(r - WIN_R // 2, 0, rows - WIN_R)
        cfg = r - rs
        start = pl.multiple_of(rs * GRID_W, GRID_W)
        kw = k_ref[0, pl.ds(start, NA_KEYS), :].astype(BF16)
        vw = v_ref[0, pl.ds(start, NA_KEYS), :].astype(BF16)
        q = q_ref[0, rr * GRID_W:(rr + 1) * GRID_W, :] * (C_DH ** -0.5)
        outs = []
        for hh in range(2):
            qm = (jnp.where(low, q, 0) if hh == 0 else jnp.where(low, 0, q)).astype(BF16)
            bias = jnp.concatenate([bias_ref[hh, WIN_R - 1 - cfg + j] for j in range(0, WIN_R, 2)], axis=-1)
            s_lat = _dot_nt(qm, kw) + bias
            s_ctx = _dot_nt(qm, kc)
            m = jnp.maximum(s_lat.max(axis=-1, keepdims=True), s_ctx.max(axis=-1, keepdims=True))
            e_lat, e_ctx = jnp.exp(s_lat - m), jnp.exp(s_ctx - m)
            denom = e_lat.sum(axis=-1, keepdims=True) + e_ctx.sum(axis=-1, keepdims=True)
            outs.append((_dot(e_lat.astype(BF16), vw) + _dot(e_ctx.astype(BF16), vc)) / denom)
        o_ref[0, rr * GRID_W:(rr + 1) * GRID_W, :] = jnp.where(low, outs[0], outs[1])


def _natten_lat(proj_lat, proj_ctx, bias):
    B, L, _ = proj_lat.shape
    Lc = proj_ctx.shape[1]
    rows = L // GRID_W
    HP = C_HEADS // 2
    tq = NA_ROWS_PER_STEP * GRID_W
    return pl.pallas_call(
        functools.partial(_natten_kernel, rows=rows),
        grid=(B, HP, rows // NA_ROWS_PER_STEP),
        in_specs=[
            pl.BlockSpec((1, tq, LANES), lambda b, h, i: (b, i, COL_C_Q + h)),
            pl.BlockSpec((1, L, LANES), lambda b, h, i: (b, 0, COL_C_K + h)),
            pl.BlockSpec((1, L, LANES), lambda b, h, i: (b, 0, COL_C_V + h)),
            pl.BlockSpec((1, Lc, LANES), lambda b, h, i: (b, 0, COL_C_K + h)),
            pl.BlockSpec((1, Lc, LANES), lambda b, h, i: (b, 0, COL_C_V + h)),
            pl.BlockSpec((2, 2 * WIN_R - 2, GRID_W, 2 * GRID_W), lambda b, h, i: (h, 0, 0, 0)),
        ],
        out_specs=pl.BlockSpec((1, tq, LANES), lambda b, h, i: (b, i, h)),
        out_shape=jax.ShapeDtypeStruct((B, L, HP * LANES), F32),
        compiler_params=_cparams(("parallel", "parallel", "parallel")),
        name="natten_lat",
    )(proj_lat, proj_lat, proj_lat, proj_ctx, proj_ctx, bias)


def _natten_ctx_kernel(q_ref, k_ref, v_ref, o_ref):
    lane = lax.broadcasted_iota(jnp.int32, (1, LANES), 1)
    low = lane < C_DH
    q = q_ref[0] * (C_DH ** -0.5)
    kv = [(k_ref[0].astype(BF16), v_ref[0].astype(BF16))]
    o0 = _softmax_pv(jnp.where(low, q, 0).astype(BF16), kv)
    o1 = _softmax_pv(jnp.where(low, 0, q).astype(BF16), kv)
    o_ref[0] = jnp.where(low, o0, o1)


def _natten_ctx(proj_ctx):
    B, Lc, _ = proj_ctx.shape
    HP = C_HEADS // 2
    return pl.pallas_call(
        _natten_ctx_kernel,
        grid=(B, HP),
        in_specs=[
            pl.BlockSpec((1, Lc, LANES), lambda b, h: (b, 0, COL_C_Q + h)),
            pl.BlockSpec((1, Lc, LANES), lambda b, h: (b, 0, COL_C_K + h)),
            pl.BlockSpec((1, Lc, LANES), lambda b, h: (b, 0, COL_C_V + h)),
        ],
        out_specs=pl.BlockSpec((1, Lc, LANES), lambda b, h: (b, 0, h)),
        out_shape=jax.ShapeDtypeStruct((B, Lc, HP * LANES), F32),
        compiler_params=_cparams(("parallel", "parallel")),
        name="natten_ctx",
    )(proj_ctx, proj_ctx, proj_ctx)


def _merge_kernel(ya_ref, yb_ref, yc_ref, g0_ref, g1_ref, g2_ref, wup_ref, wout_ref, x_ref, gt_ref, o_ref):
    acc = None
    for n, (y_ref, g_ref) in enumerate(((ya_ref, g0_ref), (yb_ref, g1_ref), (yc_ref, g2_ref))):
        up = _dot(y_ref[0].astype(BF16), wup_ref[n])
        t = jax.nn.sigmoid(g_ref[0]) * up
        acc = t if acc is None else acc + t
    r = _dot(acc.astype(BF16), wout_ref[...])
    o_ref[0] = x_ref[0] + gt_ref[0] * r


def _merge(ya, yb, yc, proj, w_up, w_out, x, gate, tm):
    B, L, D = x.shape
    gcol = COL_GATE * LANES // D
    yspec = pl.BlockSpec((1, tm, BRANCH_W), lambda b, i: (b, i, 0))
    return pl.pallas_call(
        _merge_kernel,
        grid=(B, L // tm),
        in_specs=[
            yspec, yspec, yspec,
            pl.BlockSpec((1, tm, D), lambda b, i: (b, i, gcol)),
            pl.BlockSpec((1, tm, D), lambda b, i: (b, i, gcol + 1)),
            pl.BlockSpec((1, tm, D), lambda b, i: (b, i, gcol + 2)),
            pl.BlockSpec((N_BRANCH, BRANCH_W, D), lambda b, i: (0, 0, 0)),
            pl.BlockSpec((D, D), lambda b, i: (0, 0)),
            pl.BlockSpec((1, tm, D), lambda b, i: (b, i, 0)),
            pl.BlockSpec((1, 1, D), lambda b, i: (b, 0, 0)),
        ],
        out_specs=pl.BlockSpec((1, tm, D), lambda b, i: (b, i, 0)),
        out_shape=jax.ShapeDtypeStruct((B, L, D), F32),
        compiler_params=_cparams(("parallel", "parallel")),
        name="merge",
    )(ya, yb, yc, proj, proj, proj, w_up, w_out, x, gate)


PEER_CAND = P_TOPK + 1
PEER_A_PAD = 24
PEER_CAND_ROWS = PEER_A_PAD + 7 * 8 + 16


PEER_NO_RANK = float(N_KEYS)


def _extract_top(s, n, with_rank=False):
    vals = []
    rank = jnp.full(s.shape, PEER_NO_RANK, F32) if with_rank else None
    for r in range(n):
        m = jnp.max(s, axis=0, keepdims=True)
        vals.append(m)
        hit = s == m
        if with_rank:
            rank = jnp.where(hit, float(r), rank)
        s = jnp.where(hit, NEG_INF, s)
    return (vals, rank) if with_rank else vals


def _peerq_kernel(x_ref, sh_ref, sc_ref, g_ref, wq_ref, keys_ref,
                  xn_ref, r2_ref, e2_ref, n1_ref, e1_ref, ab_ref, cand_ref):
    x = x_ref[0]
    tm = x.shape[0]
    y = x * lax.rsqrt(jnp.mean(x * x, axis=-1, keepdims=True) + EPS) * g_ref[...]
    xn = (y * (1.0 + sc_ref[0]) + sh_ref[0]).astype(BF16)
    xn_ref[...] = xn
    q = _dot(xn, wq_ref[...]).astype(BF16)
    row = lax.broadcasted_iota(jnp.int32, (PEER_A_PAD, 1), 0)
    row8 = lax.broadcasted_iota(jnp.int32, (8, 1), 0)
    row16 = lax.broadcasted_iota(jnp.int32, (16, 1), 0)
    for h in range(P_HEADS):
        s, ranks = [], []
        for p in range(2):
            hp = 2 * h + p
            st = _dot_nt(keys_ref[hp], q[:, hp * P_DKH:(hp + 1) * P_DKH])
            s.append(st)
            ab_ref[p] = jnp.full((PEER_A_PAD, tm), NEG_INF, F32)
            vals, rank = _extract_top(st, PEER_CAND, with_rank=True)
            ranks.append(rank)
            for r, m in enumerate(vals):
                ab_ref[p, r:r + 1, :] = m
        a_all, b_all = ab_ref[0], ab_ref[1]
        cand_ref[0:PEER_A_PAD, :] = a_all[0:1, :] + b_all
        for i in range(1, 8):
            n_i = PEER_CAND // (i + 1)
            cand_ref[PEER_A_PAD + 8 * (i - 1):PEER_A_PAD + 8 * i, :] = jnp.where(
                row8 < n_i, a_all[i:i + 1, :] + b_all[0:8, :], NEG_INF)
        cand_ref[PEER_A_PAD + 56:PEER_A_PAD + 72, :] = jnp.where(
            row16 + 8 < PEER_CAND, a_all[8:24, :] + b_all[0:1, :], NEG_INF)
        top = _extract_top(cand_ref[...], PEER_CAND)
        z = functools.reduce(jnp.add, [jnp.exp(v - top[0]) for v in top[:P_TOPK]])
        thr = 0.5 * (top[P_TOPK - 1] + top[P_TOPK])
        n1 = jnp.zeros((N_KEYS, tm), F32)
        for i in range(P_TOPK):
            cnt = jnp.sum(jnp.where(a_all[i:i + 1, :] + b_all >= thr, 1.0, 0.0), axis=0, keepdims=True)
            n1 = jnp.where(ranks[0] == float(i), cnt, n1)
        outs = (ranks[1].astype(BF16), (jnp.exp(s[1] - b_all[0:1, :]) / z).astype(BF16),
                n1, jnp.exp(s[0] - a_all[0:1, :]))
        for o_ref, val in zip((r2_ref, e2_ref, n1_ref, e1_ref), outs):
            for tc in range(tm // LANES):
                o_ref[tc, h] = val[:, tc * LANES:(tc + 1) * LANES]


def _peerq(x, shift, scale, g, wq, keys, tm):
    B, L, D = x.shape
    T = B * L
    nb = L // tm
    tok_spec = pl.BlockSpec((tm // LANES, P_HEADS, N_KEYS, LANES), lambda b, i: (b * nb + i, 0, 0, 0))
    tok_shape = lambda dt: jax.ShapeDtypeStruct((T // LANES, P_HEADS, N_KEYS, LANES), dt)
    return pl.pallas_call(
        _peerq_kernel,
        grid=(B, nb),
        in_specs=[
            pl.BlockSpec((1, tm, D), lambda b, i: (b, i, 0)),
            pl.BlockSpec((1, 1, D), lambda b, i: (b, 0, 0)),
            pl.BlockSpec((1, 1, D), lambda b, i: (b, 0, 0)),
            pl.BlockSpec((1, D), lambda b, i: (0, 0)),
            pl.BlockSpec((D, 2 * P_HEADS * P_DKH), lambda b, i: (0, 0)),
            pl.BlockSpec((2 * P_HEADS, N_KEYS, P_DKH), lambda b, i: (0, 0, 0)),
        ],
        out_specs=[pl.BlockSpec((tm, D), lambda b, i: (b * nb + i, 0))] + [tok_spec] * 4,
        out_shape=[jax.ShapeDtypeStruct((T, D), BF16), tok_shape(BF16), tok_shape(BF16), tok_shape(F32),
                   tok_shape(F32)],
        scratch_shapes=[pltpu.VMEM((2, PEER_A_PAD, tm), F32), pltpu.VMEM((PEER_CAND_ROWS, tm), F32)],
        compiler_params=_cparams(("parallel", "parallel")),
        name="peer_query",
    )(x, shift, scale, g, wq, keys)


PEER_EC = 1024
PEER_TT = 512
PEER_SUB = 512
PEER_GATE_ROWS = 64


def _gelu(x):
    return 0.5 * x * (1.0 + lax.erf(x * math.sqrt(0.5)))


def _peer_kernel(xn_ref, u_ref, vt_ref, r2_ref, e2_ref, n1_ref, e1_ref, x_ref, gt_ref, fg_ref,
                 o_ref, acc_ref, act0_ref, act1_ref, pt0_ref, pt1_ref, *, final_norm):
    c = pl.program_id(1)
    act_refs, pt_refs = (act0_ref, act1_ref), (pt0_ref, pt1_ref)

    @pl.when(c == 0)
    def _():
        acc_ref[...] = jnp.zeros_like(acc_ref)

    TT = xn_ref.shape[0]
    n_sub = PEER_EC // PEER_SUB

    def hidden(k):
        rows = slice(k * PEER_SUB, (k + 1) * PEER_SUB)
        act_refs[k % 2][...] = _gelu(_dot_nt(u_ref[rows, :], xn_ref[...])).astype(BF16)

    def gated(k):
        act_ref, pt_ref = act_refs[k % 2], pt_refs[k % 2]
        for a_loc in range(PEER_SUB // N_KEYS):
            a = k * (PEER_SUB // N_KEYS) + a_loc
            for tc in range(TT // LANES):
                cols = slice(tc * LANES, (tc + 1) * LANES)
                n1s = [n1_ref[tc, h, a:a + 1, :].astype(BF16) for h in range(P_HEADS)]
                e1s = [e1_ref[tc, h, a:a + 1, :].astype(BF16) for h in range(P_HEADS)]
                for b0 in range(0, N_KEYS, PEER_GATE_ROWS):
                    brows = slice(b0, b0 + PEER_GATE_ROWS)
                    rows = slice(a_loc * N_KEYS + b0, a_loc * N_KEYS + b0 + PEER_GATE_ROWS)
                    gate = None
                    for h in range(P_HEADS):
                        ind = jnp.clip(n1s[h] - r2_ref[tc, h, brows, :], 0.0, 1.0)
                        t = (e1s[h] * ind) * e2_ref[tc, h, brows, :]
                        gate = t if gate is None else gate + t
                    pt_ref[rows, cols] = gate * act_ref[rows, cols]

    def project(k):
        rows = slice(k * PEER_SUB, (k + 1) * PEER_SUB)
        acc_ref[...] += _dot(vt_ref[:, rows], pt_refs[k % 2][...])

    hidden(0)
    for k in range(n_sub):
        if k + 1 < n_sub:
            hidden(k + 1)
        gated(k)
        project(k)

    @pl.when(c == pl.num_programs(1) - 1)
    def _():
        y = x_ref[...] + gt_ref[0] * acc_ref[...].T
        if final_norm:
            y = y * lax.rsqrt(jnp.mean(y * y, axis=-1, keepdims=True) + EPS) * fg_ref[...]
        o_ref[...] = y


def _peer(xn, r2, e2, n1, e1, u, vt, x, gate, final_g, final_norm):
    B, L, D = x.shape
    T = B * L
    TT = min(PEER_TT, L)
    per_b = L // TT
    tok_spec = pl.BlockSpec((TT // LANES, P_HEADS, N_KEYS, LANES), lambda i, c: (i, 0, 0, 0))
    key_spec = pl.BlockSpec((TT // LANES, P_HEADS, PEER_EC // N_KEYS, LANES), lambda i, c: (i, 0, c, 0))
    out = pl.pallas_call(
        functools.partial(_peer_kernel, final_norm=final_norm),
        grid=(T // TT, N_EXPERTS // PEER_EC),
        in_specs=[
            pl.BlockSpec((TT, D), lambda i, c: (i, 0)),
            pl.BlockSpec((PEER_EC, D), lambda i, c: (c, 0)),
            pl.BlockSpec((D, PEER_EC), lambda i, c: (0, c)),
            tok_spec, tok_spec, key_spec, key_spec,
            pl.BlockSpec((TT, D), lambda i, c: (i, 0)),
            pl.BlockSpec((1, 1, D), lambda i, c: (i // per_b, 0, 0)),
            pl.BlockSpec((1, D), lambda i, c: (0, 0)),
        ],
        out_specs=pl.BlockSpec((TT, D), lambda i, c: (i, 0)),
        out_shape=jax.ShapeDtypeStruct((T, D), F32),
        scratch_shapes=[pltpu.VMEM((D, TT), F32),
                        pltpu.VMEM((PEER_SUB, TT), BF16), pltpu.VMEM((PEER_SUB, TT), BF16),
                        pltpu.VMEM((PEER_SUB, TT), BF16), pltpu.VMEM((PEER_SUB, TT), BF16)],
        compiler_params=_cparams(("parallel", "arbitrary")),
        name="peer_experts",
    )(xn, u, vt, r2, e2, n1, e1, x.reshape(T, D), gate, final_g)
    return out.reshape(B, L, D)


GDN_CONV_TILE = 512
GDN_PAD = 8


def _gdn_prep_kernel(q_ref, k_ref, v_ref, cwq_ref, cwk_ref, cwv_ref, qo_ref, ko_ref, vo_ref, pad_ref):
    Ls = q_ref.shape[1]
    T = min(GDN_CONV_TILE, Ls)
    zeros = jnp.zeros((GDN_PAD, LANES), F32)
    pad_ref[0:GDN_PAD, :] = zeros
    pad_ref[GDN_PAD + Ls:2 * GDN_PAD + Ls, :] = zeros
    for which, (x_ref, cw_ref, o_ref) in enumerate(((q_ref, cwq_ref, qo_ref), (k_ref, cwk_ref, ko_ref),
                                                    (v_ref, cwv_ref, vo_ref))):
        pad_ref[GDN_PAD:GDN_PAD + Ls, :] = x_ref[0]
        for t0 in range(0, Ls, T):
            acc = None
            for j in range(CONV_K):
                off = GDN_PAD - CONV_K // 2 + j + t0
                t = cw_ref[j:j + 1, :] * pad_ref[off:off + T, :]
                acc = t if acc is None else acc + t
            y = acc * jax.nn.sigmoid(acc)
            if which < 2:
                y = y * lax.rsqrt(jnp.sum(y * y, axis=-1, keepdims=True) + EPS)
            if which == 0:
                y = y * (A_DK ** -0.5)
            o_ref[0, t0:t0 + T, :] = y


def _gdn_prep(proj, conv_w):
    B, Ls, _ = proj.shape
    seq = lambda col: pl.BlockSpec((1, Ls, LANES), lambda b, h: (b, 0, col + h))
    cw = lambda col: pl.BlockSpec((CONV_K, LANES), lambda b, h: (0, col + h))
    out = pl.BlockSpec((1, Ls, LANES), lambda b, h: (b, 0, h))
    shape = jax.ShapeDtypeStruct((B, Ls, A_HEADS * LANES), F32)
    return pl.pallas_call(
        _gdn_prep_kernel,
        grid=(B, A_HEADS),
        in_specs=[seq(COL_A_Q), seq(COL_A_K), seq(COL_A_V), cw(0), cw(A_HEADS), cw(2 * A_HEADS)],
        out_specs=[out] * 3,
        out_shape=[shape] * 3,
        scratch_shapes=[pltpu.VMEM((Ls + 2 * GDN_PAD, LANES), F32)],
        compiler_params=_cparams(("parallel", "parallel")),
        name="gdn_prep",
    )(proj, proj, proj, conv_w, conv_w, conv_w)


def _softplus(x):
    return jnp.maximum(x, 0.0) + jnp.log1p(jnp.exp(-jnp.abs(x)))


GDN_GROUP = 4


def _gdn_prepass_group(h, chunks, nega_ref, dtb_ref, scr):
    mneg_scr, c_scr, qp_scr, dl_scr, o_scr = scr
    C = CHUNK
    row = lax.broadcasted_iota(jnp.int32, (C, C), 0)
    col = lax.broadcasted_iota(jnp.int32, (C, C), 1)
    lane = lax.broadcasted_iota(jnp.int32, (1, LANES), 1)
    eye = row == col
    chains = []
    for q, k, v, ba, row0, chunk_id in chunks:
        kb16 = k.astype(BF16)
        qk_raw = _dot_nt(q.astype(BF16), kb16)
        for d in range(2):
            before_eq_rc = (col <= row) if d == 0 else (col >= row)
            before_rc = (col < row) if d == 0 else (col > row)
            before_eq_cr = (row <= col) if d == 0 else (row >= col)
            bcol = jnp.sum(jnp.where(lane == d * A_HEADS + h, ba, 0.0), axis=-1, keepdims=True)
            acol = jnp.sum(jnp.where(lane == (2 + d) * A_HEADS + h, ba, 0.0), axis=-1, keepdims=True)
            beta = jax.nn.sigmoid(bcol)
            g = nega_ref[d, h] * _softplus(acol + dtb_ref[d, h])
            g_cols = jnp.broadcast_to(g, (C, C))
            gc_row = jnp.sum(jnp.where(before_eq_cr, g_cols, 0.0), axis=0, keepdims=True)
            g_row = jnp.sum(jnp.where(eye, g_cols, 0.0), axis=0, keepdims=True)
            gc_col = jnp.sum(jnp.where(before_eq_rc, jnp.broadcast_to(g_row, (C, C)), 0.0), axis=-1, keepdims=True)
            g_total = jnp.sum(g_row, axis=-1, keepdims=True)
            decay = jnp.exp(jnp.where(before_eq_rc, gc_col - gc_row, NEG_INF))
            kbeta = k * beta
            e_gc = jnp.exp(gc_col)
            chains.append(dict(
                d=d, row0=row0, chunk_id=chunk_id,
                n_pow=-jnp.where(before_rc, _dot_nt(kbeta.astype(BF16), kb16) * decay, 0.0),
                qk=jnp.where(before_eq_rc, qk_raw * decay, 0.0).astype(BF16),
                rhs=jnp.concatenate([v * beta, kbeta * e_gc], axis=-1).astype(BF16),
                kd=(k * jnp.exp(g_total - gc_col)).astype(BF16),
                qd=q * e_gc,
                dl=jnp.exp(g_total)))
    eye_f = jnp.where(eye, 1.0, 0.0)
    n_pows = [ch['n_pow'] for ch in chains]
    invs = [eye_f + n for n in n_pows]
    for _ in range(5):
        n16 = [n.astype(BF16) for n in n_pows]
        n_pows = [_dot(n, n) for n in n16]
        n16 = [n.astype(BF16) for n in n_pows]
        invs = [inv + _dot(inv.astype(BF16), n) for inv, n in zip(invs, n16)]
    uws = [_dot(inv.astype(BF16), ch['rhs']).astype(BF16) for inv, ch in zip(invs, chains)]
    kts = [_dot_tn(ch['kd'], uw) for ch, uw in zip(chains, uws)]
    qqs = [_dot(ch['qk'], uw) for ch, uw in zip(chains, uws)]
    for ch, kt, qq in zip(chains, kts, qqs):
        d, cid = ch['d'], ch['chunk_id']
        rows = pl.ds(ch['row0'], C)
        c_scr[d, cid] = kt[:, :A_DV]
        mneg_scr[d, cid] = (-kt[:, A_DV:]).astype(BF16)
        o_scr[d, rows, :] = qq[:, :A_DV]
        qp_scr[d, rows, :] = (ch['qd'] - qq[:, A_DV:]).astype(BF16)
        dl_scr[d, pl.ds(cid * 8, 8), :] = jnp.broadcast_to(ch['dl'], (8, LANES))


def _gdn_scan_step(states, scr, row0s, chunk_ids):
    mneg_scr, c_scr, qp_scr, dl_scr, o_scr = scr
    s16 = [S.astype(BF16) for S in states]
    upd = [_dot(mneg_scr[d, chunk_ids[d]], s16[d]) for d in range(2)]
    for d in range(2):
        rows = pl.ds(row0s[d], CHUNK)
        o_scr[d, rows, :] += _dot(qp_scr[d, rows, :], s16[d])
    return tuple(states[d] * dl_scr[d, pl.ds(chunk_ids[d] * 8, 1), :] + c_scr[d, chunk_ids[d]] + upd[d]
                 for d in range(2))


def _gdn_scan_kernel(nega_ref, dtb_ref, qc_ref, kc_ref, vc_ref, bac_ref, zc_ref, ql_ref, kl_ref, vl_ref, bal_ref,
                     zl_ref, ng_ref, yl_ref, yc_ref, mneg_scr, c_scr, qp_scr, dl_scr, o_scr):
    h = pl.program_id(1)
    Lc, L = qc_ref.shape[1], ql_ref.shape[1]
    nc, nl = Lc // CHUNK, L // CHUNK
    scr = (mneg_scr, c_scr, qp_scr, dl_scr, o_scr)

    for c0 in range(0, nc, GDN_GROUP):
        _gdn_prepass_group(h, [(qc_ref[0, c * CHUNK:(c + 1) * CHUNK, :], kc_ref[0, c * CHUNK:(c + 1) * CHUNK, :],
                                vc_ref[0, c * CHUNK:(c + 1) * CHUNK, :], bac_ref[0, c * CHUNK:(c + 1) * CHUNK, :],
                                c * CHUNK, c) for c in range(c0, min(c0 + GDN_GROUP, nc))],
                           nega_ref, dtb_ref, scr)

    def pre_body(grp, carry):
        chunks = []
        for j in range(GDN_GROUP):
            c = grp * GDN_GROUP + j
            r = pl.multiple_of(c * CHUNK, CHUNK)
            rows = pl.ds(r, CHUNK)
            chunks.append((ql_ref[0, rows, :], kl_ref[0, rows, :], vl_ref[0, rows, :], bal_ref[0, rows, :],
                           Lc + r, nc + c))
        _gdn_prepass_group(h, chunks, nega_ref, dtb_ref, scr)
        return carry

    lax.fori_loop(0, nl // GDN_GROUP, pre_body, 0)

    states = (jnp.zeros((A_DK, A_DV), F32), jnp.zeros((A_DK, A_DV), F32))
    for s in range(nc):
        states = _gdn_scan_step(states, scr, (s * CHUNK, (nc - 1 - s) * CHUNK), (s, nc - 1 - s))

    def scan_body(s, states):
        cf = s
        cb = nl - 1 - s
        return _gdn_scan_step(states, scr,
                              (pl.multiple_of(Lc + cf * CHUNK, CHUNK), pl.multiple_of(Lc + cb * CHUNK, CHUNK)),
                              (nc + cf, nc + cb))

    lax.fori_loop(0, nl, scan_body, states)

    def finish(z_ref, y_ref, base, n):
        T = min(GDN_CONV_TILE, n)
        for t0 in range(0, n, T):
            o = o_scr[0, base + t0:base + t0 + T, :] + o_scr[1, base + t0:base + t0 + T, :]
            y = o * lax.rsqrt(jnp.mean(o * o, axis=-1, keepdims=True) + EPS) * ng_ref[...]
            z = z_ref[0, t0:t0 + T, :]
            y_ref[0, t0:t0 + T, :] = y * (z * jax.nn.sigmoid(z))

    finish(zl_ref, yl_ref, Lc, L)
    finish(zc_ref, yc_ref, 0, Lc)


def _gdn_scan(nega, dtb, qkv_ctx, ba_ctx, proj_ctx, qkv_lat, ba_lat, proj_lat, norm_g):
    B, L, _ = proj_lat.shape
    Lc = proj_ctx.shape[1]
    Lt = L + Lc
    head = lambda n: pl.BlockSpec((1, n, LANES), lambda b, h: (b, 0, h))
    full = lambda n: pl.BlockSpec((1, n, LANES), lambda b, h: (b, 0, 0))
    zcol = lambda n: pl.BlockSpec((1, n, LANES), lambda b, h: (b, 0, COL_A_Z + h))
    smem = pl.BlockSpec(memory_space=pltpu.SMEM)
    return pl.pallas_call(
        _gdn_scan_kernel,
        grid=(B, A_HEADS),
        in_specs=[smem, smem,
                  head(Lc), head(Lc), head(Lc), full(Lc), zcol(Lc),
                  head(L), head(L), head(L), full(L), zcol(L),
                  pl.BlockSpec((1, LANES), lambda b, h: (0, 0))],
        out_specs=[head(L), head(Lc)],
        out_shape=[jax.ShapeDtypeStruct((B, L, A_HEADS * LANES), F32),
                   jax.ShapeDtypeStruct((B, Lc, A_HEADS * LANES), F32)],
        scratch_shapes=[
            pltpu.VMEM((2, Lt // CHUNK, A_DK, A_DV), BF16),
            pltpu.VMEM((2, Lt // CHUNK, A_DK, A_DV), F32),
            pltpu.VMEM((2, Lt, A_DK), BF16),
            pltpu.VMEM((2, Lt // CHUNK * 8, LANES), F32),
            pltpu.VMEM((2, Lt, A_DV), F32),
        ],
        compiler_params=_cparams(("parallel", "parallel")),
        name="gdn_scan",
    )(nega, dtb, *qkv_ctx, ba_ctx, proj_ctx, *qkv_lat, ba_lat, proj_lat, norm_g)


def _gdn_pallas(proj_lat, ba_lat, proj_ctx, ba_ctx, conv_w, a_log, dt_bias, norm_g):
    qkv_lat = _gdn_prep(proj_lat, conv_w)
    qkv_ctx = _gdn_prep(proj_ctx, conv_w)
    return _gdn_scan(-jnp.exp(a_log), dt_bias, qkv_ctx, ba_ctx, proj_ctx, qkv_lat, ba_lat, proj_lat, norm_g[None, :])


def _layer(i, x, xc, c, c_ctx, p, cos_t, sin_t, ctx_out, final_g, final_norm):
    B, L, D = x.shape
    Lc = xc.shape[1]
    mod_lat = (jax.nn.silu(c) @ p['ada_w'] + p['ada_b'])[:, None, :]
    mod_ctx = jnp.broadcast_to((jax.nn.silu(c_ctx) @ p['ada_w'] + p['ada_b'])[None, None, :], (B, 1, 6 * D))
    sh1, sc1, gt1, sh2, sc2, gt2 = jnp.split(mod_lat, 6, axis=-1)
    csh1, csc1, cgt1, csh2, csc2, cgt2 = jnp.split(mod_ctx, 6, axis=-1)
    lam_init = 0.8 - 0.6 * math.exp(-0.3 * i)

    w_in = p['w_in']
    n_a = 4 * A_HEADS * A_DK
    w_main = jnp.concatenate([w_in[:, :n_a], w_in[:, n_a + 4 * A_HEADS:]], axis=1).astype(BF16)
    w_ba = jnp.pad(w_in[:, n_a:n_a + 4 * A_HEADS], ((0, 0), (0, LANES - 4 * A_HEADS))).astype(BF16)
    n1 = p['norm1_g'][None, :]
    proj_lat, ba_lat = _inproj(x, sh1, sc1, n1, w_main, w_ba, 512)
    proj_ctx, ba_ctx = _inproj(xc, csh1, csc1, n1, w_main, w_ba, Lc)

    ya, ya_c = _gdn_pallas(proj_lat, ba_lat, proj_ctx, ba_ctx, p['gdn_conv'], p['gdn_a_log'], p['gdn_dt_bias'],
                           p['gdn_norm_g'])

    lp = p['diff_lambda']
    lam = (jnp.exp(jnp.sum(lp[0] * lp[1])) - jnp.exp(jnp.sum(lp[2] * lp[3])) + lam_init).reshape(1)
    q_r, k_r = _rope(proj_lat, cos_t, sin_t, 512)
    subln = p['diff_subln_g'][None, :]
    yb = _diff_lat(lam, q_r, k_r, proj_lat, proj_ctx, subln, 1.0 - lam_init, 256)
    yb_c = _diff_ctx(lam, proj_ctx, subln, 1.0 - lam_init) if ctx_out else None

    yc = _natten_lat(proj_lat, proj_ctx, _natten_bias(p['na_rpb']))
    yc_c = _natten_ctx(proj_ctx) if ctx_out else None

    w_up = p['w_up'].astype(BF16)
    w_out = p['w_out'].astype(BF16)
    wq = p['peer_wq'].astype(BF16)
    keys = p['peer_keys'].reshape(2 * P_HEADS, N_KEYS, P_DKH).astype(BF16)
    u = p['peer_u'].astype(BF16)
    vt = p['peer_v'].T.astype(BF16)
    n2 = p['norm2_g'][None, :]
    fg = final_g[None, :]

    x = _merge(ya, yb, yc, proj_lat, w_up, w_out, x, gt1, 512)
    pq = _peerq(x, sh2, sc2, n2, wq, keys, 256)
    x = _peer(*pq, u, vt, x, gt2, fg, final_norm)
    if ctx_out:
        xc = _merge(ya_c, yb_c, yc_c, proj_ctx, w_up, w_out, xc, cgt1, Lc)
        pq = _peerq(xc, csh2, csc2, n2, wq, keys, Lc)
        xc = _peer(*pq, u, vt, xc, cgt2, fg, False)
    return x, xc


def kernel(x, c, ctx, c_ctx, norm1_g, norm2_g, ada_w, ada_b, w_in, gdn_conv, gdn_a_log, gdn_dt_bias, gdn_norm_g,
           diff_lambda, diff_subln_g, na_rpb, w_up, w_out, peer_wq, peer_keys, peer_u, peer_v, final_g):
    cos_t, sin_t = _rope_tables(x.shape[1])
    xc = ctx
    for i in range(DEPTH):
        p = dict(norm1_g=norm1_g[i], norm2_g=norm2_g[i], ada_w=ada_w[i], ada_b=ada_b[i], w_in=w_in[i],
                 gdn_conv=gdn_conv[i], gdn_a_log=gdn_a_log[i], gdn_dt_bias=gdn_dt_bias[i], gdn_norm_g=gdn_norm_g[i],
                 diff_lambda=diff_lambda[i], diff_subln_g=diff_subln_g[i], na_rpb=na_rpb[i], w_up=w_up[i],
                 w_out=w_out[i], peer_wq=peer_wq[i], peer_keys=peer_keys[i], peer_u=peer_u[i], peer_v=peer_v[i])
        x, xc = _layer(i, x, xc, c, c_ctx, p, cos_t, sin_t, i < DEPTH - 1, final_g, i == DEPTH - 1)
    return x
```

```python
import functools
import math

import numpy as np
import jax
import jax.numpy as jnp
from jax import lax
from jax.experimental import pallas as pl
from jax.experimental.pallas import tpu as pltpu

F32 = jnp.float32
BF16 = jnp.bfloat16

D_MODEL = 1024
DEPTH = 2
GRID_W = 64
EPS = 1e-6
NEG_INF = -1e30

A_HEADS = 4
A_DK = 128
A_DV = 128
CONV_K = 5
CHUNK = 64
B_HEADS = 4
B_DH = 64
ROPE_BASE = 10000.0
C_HEADS = 8
C_DH = 64
WIN_R = 8
WIN_C = 16
N_BRANCH = 3
BRANCH_W = 512
P_HEADS = 8
N_KEYS = 128
N_EXPERTS = N_KEYS * N_KEYS
P_DKH = 128
P_TOPK = 16

LANES = 128
VMEM_LIMIT = 56 * 1024 * 1024

COL_A_Q, COL_A_K, COL_A_V, COL_A_Z = 0, 4, 8, 12
COL_B_Q, COL_B_K, COL_B_V = 16, 20, 24
COL_C_Q, COL_C_K, COL_C_V = 28, 32, 36
COL_GATE = 40
MAIN_COLS = 64 * LANES


def _cparams(sem):
    return pltpu.CompilerParams(dimension_semantics=sem, vmem_limit_bytes=VMEM_LIMIT)


def _dot(a, b):
    return jnp.dot(a, b, preferred_element_type=F32)


def _dot_nt(a, b):
    return lax.dot_general(a, b, (((1,), (1,)), ((), ())), preferred_element_type=F32)


def _dot_tn(a, b):
    return lax.dot_general(a, b, (((0,), (0,)), ((), ())), preferred_element_type=F32)


def _inproj_kernel(x_ref, sh_ref, sc_ref, g_ref, w_ref, wba_ref, o_ref, oba_ref, hn_ref):
    @pl.when(pl.program_id(2) == 0)
    def _():
        x = x_ref[0]
        y = x * lax.rsqrt(jnp.mean(x * x, axis=-1, keepdims=True) + EPS) * g_ref[...]
        h = (y * (1.0 + sc_ref[0]) + sh_ref[0]).astype(BF16)
        hn_ref[...] = h
        oba_ref[0] = _dot(h, wba_ref[...])

    o_ref[0] = _dot(hn_ref[...], w_ref[...])


def _inproj(x, shift, scale, g, w_main, w_ba, tm):
    B, L, D = x.shape
    tn = 1024
    return pl.pallas_call(
        _inproj_kernel,
        grid=(B, L // tm, MAIN_COLS // tn),
        in_specs=[
            pl.BlockSpec((1, tm, D), lambda b, i, j: (b, i, 0)),
            pl.BlockSpec((1, 1, D), lambda b, i, j: (b, 0, 0)),
            pl.BlockSpec((1, 1, D), lambda b, i, j: (b, 0, 0)),
            pl.BlockSpec((1, D), lambda b, i, j: (0, 0)),
            pl.BlockSpec((D, tn), lambda b, i, j: (0, j)),
            pl.BlockSpec((D, LANES), lambda b, i, j: (0, 0)),
        ],
        out_specs=[
            pl.BlockSpec((1, tm, tn), lambda b, i, j: (b, i, j)),
            pl.BlockSpec((1, tm, LANES), lambda b, i, j: (b, i, 0)),
        ],
        out_shape=[
            jax.ShapeDtypeStruct((B, L, MAIN_COLS), F32),
            jax.ShapeDtypeStruct((B, L, LANES), F32),
        ],
        scratch_shapes=[pltpu.VMEM((tm, D), BF16)],
        compiler_params=_cparams(("parallel", "parallel", "arbitrary")),
        name="inproj",
    )(x, shift, scale, g, w_main, w_ba)


def _rope_tables(L):
    t = np.arange(L)
    row_pos, col_pos = t // GRID_W, t % GRID_W
    lane = np.arange(LANES)
    axis = (lane % 64) // 32
    f = lane % 16
    inv = 1.0 / (ROPE_BASE ** (f.astype(np.float32) / 16.0))
    pos = np.where(axis[None, :] == 0, row_pos[:, None], col_pos[:, None]).astype(np.float32)
    ang = jnp.asarray(pos) * jnp.asarray(inv.astype(np.float32))[None, :]
    first = jnp.asarray(((lane % 32) < 16)[None, :])
    return jnp.cos(ang), jnp.where(first, -jnp.sin(ang), jnp.sin(ang))


def _rope_kernel(q_ref, k_ref, cos_ref, sin_ref, qo_ref, ko_ref):
    lane = lax.broadcasted_iota(jnp.int32, (1, LANES), 1)
    first = (lane % 32) < 16
    c, s = cos_ref[...], sin_ref[...]

    def rope(x):
        partner = jnp.where(first, pltpu.roll(x, LANES - 16, 1), pltpu.roll(x, 16, 1))
        return x * c + partner * s

    for h in range(B_HEADS):
        sl = slice(h * LANES, (h + 1) * LANES)
        qo_ref[0, :, sl] = (rope(q_ref[0, :, sl]) * (B_DH ** -0.5)).astype(BF16)
        ko_ref[0, :, sl] = rope(k_ref[0, :, sl]).astype(BF16)


def _rope(proj, cos_t, sin_t, tr):
    B, L, _ = proj.shape
    W = B_HEADS * LANES
    return pl.pallas_call(
        _rope_kernel,
        grid=(B, L // tr),
        in_specs=[
            pl.BlockSpec((1, tr, W), lambda b, i: (b, i, COL_B_Q * LANES // W)),
            pl.BlockSpec((1, tr, W), lambda b, i: (b, i, COL_B_K * LANES // W)),
            pl.BlockSpec((tr, LANES), lambda b, i: (i, 0)),
            pl.BlockSpec((tr, LANES), lambda b, i: (i, 0)),
        ],
        out_specs=[pl.BlockSpec((1, tr, W), lambda b, i: (b, i, 0))] * 2,
        out_shape=[jax.ShapeDtypeStruct((B, L, W), BF16)] * 2,
        compiler_params=_cparams(("parallel", "parallel")),
        name="rope",
    )(proj, proj, cos_t, sin_t)


def _softmax_pv(q, key_vals):
    scores = [_dot_nt(q, k) for k, _ in key_vals]
    m = functools.reduce(jnp.maximum, [s.max(axis=-1, keepdims=True) for s in scores])
    es = [jnp.exp(s - m) for s in scores]
    denom = functools.reduce(jnp.add, [e.sum(axis=-1, keepdims=True) for e in es])
    o = functools.reduce(jnp.add, [_dot(e.astype(BF16), v) for e, (_, v) in zip(es, key_vals)])
    return o / denom


DIFF_Q_ROWS = 256


def _diff_finish(o0, o1, lam, g, out_scale):
    o = o0 - lam * o1
    return o * lax.rsqrt(jnp.mean(o * o, axis=-1, keepdims=True) + EPS) * g * out_scale


def _diff_lat_kernel(lam_ref, q_ref, kl_ref, vl_ref, kc_ref, vc_ref, g_ref, o_ref, *, out_scale):
    lane = lax.broadcasted_iota(jnp.int32, (1, LANES), 1)
    low = lane < B_DH
    tq = q_ref.shape[1]
    kl, vl = kl_ref[0], vl_ref[0].astype(BF16)
    kc, vc = kc_ref[0].astype(BF16), vc_ref[0].astype(BF16)
    units = [dict(r0=r0, mp=mp) for r0 in range(0, tq, DIFF_Q_ROWS) for mp in range(2)]

    def scores(un):
        q = q_ref[0, un['r0']:un['r0'] + DIFF_Q_ROWS, :]
        q = (jnp.where(low, q, 0) if un['mp'] == 0 else jnp.where(low, 0, q)).astype(BF16)
        un['s'] = (_dot_nt(q, kl), _dot_nt(q, kc))

    def softmax(un):
        s_l, s_c = un.pop('s')
        m = jnp.maximum(s_l.max(axis=-1, keepdims=True), s_c.max(axis=-1, keepdims=True))
        e_l, e_c = jnp.exp(s_l - m), jnp.exp(s_c - m)
        un['denom'] = e_l.sum(axis=-1, keepdims=True) + e_c.sum(axis=-1, keepdims=True)
        un['e'] = (e_l.astype(BF16), e_c.astype(BF16))

    def values(un):
        e_l, e_c = un.pop('e')
        un['o'] = (_dot(e_l, vl) + _dot(e_c, vc)) / un['denom']

    scores(units[0])
    scores(units[1])
    for k, un in enumerate(units):
        softmax(un)
        if k + 2 < len(units):
            scores(units[k + 2])
        values(un)
    for k in range(0, len(units), 2):
        r0 = units[k]['r0']
        o_ref[0, r0:r0 + DIFF_Q_ROWS, :] = _diff_finish(units[k]['o'], units[k + 1]['o'], lam_ref[0], g_ref[...],
                                                        out_scale)


def _diff_lat(lam, q_r, k_r, proj_lat, proj_ctx, subln_g, out_scale, tq):
    B, L, _ = proj_lat.shape
    Lc = proj_ctx.shape[1]
    return pl.pallas_call(
        functools.partial(_diff_lat_kernel, out_scale=out_scale),
        grid=(B, B_HEADS, L // tq),
        in_specs=[
            pl.BlockSpec(memory_space=pltpu.SMEM),
            pl.BlockSpec((1, tq, LANES), lambda b, h, i: (b, i, h)),
            pl.BlockSpec((1, L, LANES), lambda b, h, i: (b, 0, h)),
            pl.BlockSpec((1, L, LANES), lambda b, h, i: (b, 0, COL_B_V + h)),
            pl.BlockSpec((1, Lc, LANES), lambda b, h, i: (b, 0, COL_B_K + h)),
            pl.BlockSpec((1, Lc, LANES), lambda b, h, i: (b, 0, COL_B_V + h)),
            pl.BlockSpec((1, LANES), lambda b, h, i: (0, 0)),
        ],
        out_specs=pl.BlockSpec((1, tq, LANES), lambda b, h, i: (b, i, h)),
        out_shape=jax.ShapeDtypeStruct((B, L, B_HEADS * LANES), F32),
        compiler_params=_cparams(("parallel", "parallel", "parallel")),
        name="diff_lat",
    )(lam, q_r, k_r, proj_lat, proj_ctx, proj_ctx, subln_g)


def _diff_ctx_kernel(lam_ref, q_ref, k_ref, v_ref, g_ref, o_ref, *, out_scale):
    lane = lax.broadcasted_iota(jnp.int32, (1, LANES), 1)
    low = lane < B_DH
    q = q_ref[0] * (B_DH ** -0.5)
    kv = [(k_ref[0].astype(BF16), v_ref[0].astype(BF16))]
    o0 = _softmax_pv(jnp.where(low, q, 0).astype(BF16), kv)
    o1 = _softmax_pv(jnp.where(low, 0, q).astype(BF16), kv)
    o_ref[0] = _diff_finish(o0, o1, lam_ref[0], g_ref[...], out_scale)


def _diff_ctx(lam, proj_ctx, subln_g, out_scale):
    B, Lc, _ = proj_ctx.shape
    return pl.pallas_call(
        functools.partial(_diff_ctx_kernel, out_scale=out_scale),
        grid=(B, B_HEADS),
        in_specs=[
            pl.BlockSpec(memory_space=pltpu.SMEM),
            pl.BlockSpec((1, Lc, LANES), lambda b, h: (b, 0, COL_B_Q + h)),
            pl.BlockSpec((1, Lc, LANES), lambda b, h: (b, 0, COL_B_K + h)),
            pl.BlockSpec((1, Lc, LANES), lambda b, h: (b, 0, COL_B_V + h)),
            pl.BlockSpec((1, LANES), lambda b, h: (0, 0)),
        ],
        out_specs=pl.BlockSpec((1, Lc, LANES), lambda b, h: (b, 0, h)),
        out_shape=jax.ShapeDtypeStruct((B, Lc, B_HEADS * LANES), F32),
        compiler_params=_cparams(("parallel", "parallel")),
        name="diff_ctx",
    )(lam, proj_ctx, proj_ctx, proj_ctx, subln_g)


NA_ROWS_PER_STEP = 8
NA_KEYS = WIN_R * GRID_W


def _natten_bias(rpb):
    col = np.arange(GRID_W)
    col_start = np.clip(col - WIN_C // 2, 0, GRID_W - WIN_C)
    col_mask = (col[None, :] >= col_start[:, None]) & (col[None, :] < col_start[:, None] + WIN_C)
    dc = np.clip(col[None, :] - col[:, None], -(WIN_C - 1), WIN_C - 1) + WIN_C - 1
    bias = jnp.where(jnp.asarray(col_mask), rpb.astype(F32)[:, :, dc], NEG_INF)
    return jnp.concatenate([bias[:, :-1], bias[:, 1:]], axis=-1)


def _natten_kernel(q_ref, k_ref, v_ref, kc_ref, vc_ref, bias_ref, o_ref, *, rows):
    lane = lax.broadcasted_iota(jnp.int32, (1, LANES), 1)
    low = lane < C_DH
    kc = kc_ref[0].astype(BF16)
    vc = vc_ref[0].astype(BF16)
    units = []
    for rr in range(NA_ROWS_PER_STEP):
        r = pl.program_id(2) * NA_ROWS_PER_STEP + rr
        rs = jnp.clip(r - WIN_R // 2, 0, rows - WIN_R)
        cfg = r - rs
        start = pl.multiple_of(rs * GRID_W, GRID_W)
        q = q_ref[0, rr * GRID_W:(rr + 1) * GRID_W, :] * (C_DH ** -0.5)
        q2 = jnp.concatenate([jnp.where(low, q, 0), jnp.where(low, 0, q)], axis=0).astype(BF16)
        bias = jnp.concatenate(
            [jnp.concatenate([bias_ref[hh, WIN_R - 1 - cfg + j] for j in range(0, WIN_R, 2)], axis=-1)
             for hh in range(2)], axis=0)
        units.append(dict(q=q2, bias=bias, start=start))
    for un in units:
        kw = k_ref[0, pl.ds(un['start'], NA_KEYS), :].astype(BF16)
        un['s_lat'] = _dot_nt(un['q'], kw) + un['bias']
        un['s_ctx'] = _dot_nt(un['q'], kc)
    for un in units:
        s_lat, s_ctx = un['s_lat'], un['s_ctx']
        m = jnp.maximum(s_lat.max(axis=-1, keepdims=True), s_ctx.max(axis=-1, keepdims=True))
        e_lat, e_ctx = jnp.exp(s_lat - m), jnp.exp(s_ctx - m)
        un['denom'] = e_lat.sum(axis=-1, keepdims=True) + e_ctx.sum(axis=-1, keepdims=True)
        un['e_lat'], un['e_ctx'] = e_lat.astype(BF16), e_ctx.astype(BF16)
    for rr, un in enumerate(units):
        vw = v_ref[0, pl.ds(un['start'], NA_KEYS), :].astype(BF16)
        o = (_dot(un['e_lat'], vw) + _dot(un['e_ctx'], vc)) / un['denom']
        o_ref[0, rr * GRID_W:(rr + 1) * GRID_W, :] = jnp.where(low, o[:GRID_W], o[GRID_W:])


def _natten_lat(proj_lat, proj_ctx, bias):
    B, L, _ = proj_lat.shape
    Lc = proj_ctx.shape[1]
    rows = L // GRID_W
    HP = C_HEADS // 2
    tq = NA_ROWS_PER_STEP * GRID_W
    return pl.pallas_call(
        functools.partial(_natten_kernel, rows=rows),
        grid=(B, HP, rows // NA_ROWS_PER_STEP),
        in_specs=[
            pl.BlockSpec((1, tq, LANES), lambda b, h, i: (b, i, COL_C_Q + h)),
            pl.BlockSpec((1, L, LANES), lambda b, h, i: (b, 0, COL_C_K + h)),
            pl.BlockSpec((1, L, LANES), lambda b, h, i: (b, 0, COL_C_V + h)),
            pl.BlockSpec((1, Lc, LANES), lambda b, h, i: (b, 0, COL_C_K + h)),
            pl.BlockSpec((1, Lc, LANES), lambda b, h, i: (b, 0, COL_C_V + h)),
            pl.BlockSpec((2, 2 * WIN_R - 2, GRID_W, 2 * GRID_W), lambda b, h, i: (h, 0, 0, 0)),
        ],
        out_specs=pl.BlockSpec((1, tq, LANES), lambda b, h, i: (b, i, h)),
        out_shape=jax.ShapeDtypeStruct((B, L, HP * LANES), F32),
        compiler_params=_cparams(("parallel", "parallel", "parallel")),
        name="natten_lat",
    )(proj_lat, proj_lat, proj_lat, proj_ctx, proj_ctx, bias)


def _natten_ctx_kernel(q_ref, k_ref, v_ref, o_ref):
    lane = lax.broadcasted_iota(jnp.int32, (1, LANES), 1)
    low = lane < C_DH
    q = q_ref[0] * (C_DH ** -0.5)
    kv = [(k_ref[0].astype(BF16), v_ref[0].astype(BF16))]
    o0 = _softmax_pv(jnp.where(low, q, 0).astype(BF16), kv)
    o1 = _softmax_pv(jnp.where(low, 0, q).astype(BF16), kv)
    o_ref[0] = jnp.where(low, o0, o1)


def _natten_ctx(proj_ctx):
    B, Lc, _ = proj_ctx.shape
    HP = C_HEADS // 2
    return pl.pallas_call(
        _natten_ctx_kernel,
        grid=(B, HP),
        in_specs=[
            pl.BlockSpec((1, Lc, LANES), lambda b, h: (b, 0, COL_C_Q + h)),
            pl.BlockSpec((1, Lc, LANES), lambda b, h: (b, 0, COL_C_K + h)),
            pl.BlockSpec((1, Lc, LANES), lambda b, h: (b, 0, COL_C_V + h)),
        ],
        out_specs=pl.BlockSpec((1, Lc, LANES), lambda b, h: (b, 0, h)),
        out_shape=jax.ShapeDtypeStruct((B, Lc, HP * LANES), F32),
        compiler_params=_cparams(("parallel", "parallel")),
        name="natten_ctx",
    )(proj_ctx, proj_ctx, proj_ctx)


def _merge_kernel(ya_ref, yb_ref, yc_ref, g0_ref, g1_ref, g2_ref, wup_ref, wout_ref, x_ref, gt_ref, o_ref):
    acc = None
    for n, (y_ref, g_ref) in enumerate(((ya_ref, g0_ref), (yb_ref, g1_ref), (yc_ref, g2_ref))):
        up = _dot(y_ref[0].astype(BF16), wup_ref[n])
        t = jax.nn.sigmoid(g_ref[0]) * up
        acc = t if acc is None else acc + t
    r = _dot(acc.astype(BF16), wout_ref[...])
    o_ref[0] = x_ref[0] + gt_ref[0] * r


def _merge(ya, yb, yc, proj, w_up, w_out, x, gate, tm):
    B, L, D = x.shape
    gcol = COL_GATE * LANES // D
    yspec = pl.BlockSpec((1, tm, BRANCH_W), lambda b, i: (b, i, 0))
    return pl.pallas_call(
        _merge_kernel,
        grid=(B, L // tm),
        in_specs=[
            yspec, yspec, yspec,
            pl.BlockSpec((1, tm, D), lambda b, i: (b, i, gcol)),
            pl.BlockSpec((1, tm, D), lambda b, i: (b, i, gcol + 1)),
            pl.BlockSpec((1, tm, D), lambda b, i: (b, i, gcol + 2)),
            pl.BlockSpec((N_BRANCH, BRANCH_W, D), lambda b, i: (0, 0, 0)),
            pl.BlockSpec((D, D), lambda b, i: (0, 0)),
            pl.BlockSpec((1, tm, D), lambda b, i: (b, i, 0)),
            pl.BlockSpec((1, 1, D), lambda b, i: (b, 0, 0)),
        ],
        out_specs=pl.BlockSpec((1, tm, D), lambda b, i: (b, i, 0)),
        out_shape=jax.ShapeDtypeStruct((B, L, D), F32),
        compiler_params=_cparams(("parallel", "parallel")),
        name="merge",
    )(ya, yb, yc, proj, proj, proj, w_up, w_out, x, gate)


PEER_CAND = P_TOPK + 1
PEER_A_PAD = 24
PEER_CAND_ROWS = PEER_A_PAD + 7 * 8 + 16


PEER_NO_RANK = float(N_KEYS)


def _extract_top(s, n, with_rank=False):
    vals = []
    rank = jnp.full(s.shape, PEER_NO_RANK, F32) if with_rank else None
    for r in range(n):
        m = jnp.max(s, axis=0, keepdims=True)
        vals.append(m)
        hit = s == m
        if with_rank:
            rank = jnp.where(hit, float(r), rank)
        s = jnp.where(hit, NEG_INF, s)
    return (vals, rank) if with_rank else vals


def _peerq_kernel(x_ref, sh_ref, sc_ref, g_ref, wq_ref, keys_ref,
                  xn_ref, r2_ref, e2_ref, n1_ref, e1_ref, ab_ref, cand_ref):
    x = x_ref[0]
    tm = x.shape[0]
    y = x * lax.rsqrt(jnp.mean(x * x, axis=-1, keepdims=True) + EPS) * g_ref[...]
    xn = (y * (1.0 + sc_ref[0]) + sh_ref[0]).astype(BF16)
    xn_ref[...] = xn
    q = _dot(xn, wq_ref[...]).astype(BF16)
    row8 = lax.broadcasted_iota(jnp.int32, (8, 1), 0)
    row16 = lax.broadcasted_iota(jnp.int32, (16, 1), 0)
    for h in range(P_HEADS):
        for tc in range(tm // LANES):
            qt = q[tc * LANES:(tc + 1) * LANES, :]
            s, ranks = [], []
            for p in range(2):
                hp = 2 * h + p
                st = _dot_nt(keys_ref[hp], qt[:, hp * P_DKH:(hp + 1) * P_DKH])
                s.append(st)
                ab_ref[p] = jnp.full((PEER_A_PAD, LANES), NEG_INF, F32)
                vals, rank = _extract_top(st, PEER_CAND, with_rank=True)
                ranks.append(rank)
                for r, m in enumerate(vals):
                    ab_ref[p, r:r + 1, :] = m
            a_all, b_all = ab_ref[0], ab_ref[1]
            cand_ref[0:PEER_A_PAD, :] = a_all[0:1, :] + b_all
            for i in range(1, 8):
                n_i = PEER_CAND // (i + 1)
                cand_ref[PEER_A_PAD + 8 * (i - 1):PEER_A_PAD + 8 * i, :] = jnp.where(
                    row8 < n_i, a_all[i:i + 1, :] + b_all[0:8, :], NEG_INF)
            cand_ref[PEER_A_PAD + 56:PEER_A_PAD + 72, :] = jnp.where(
                row16 + 8 < PEER_CAND, a_all[8:24, :] + b_all[0:1, :], NEG_INF)
            top = _extract_top(cand_ref[...], PEER_CAND)
            z = functools.reduce(jnp.add, [jnp.exp(v - top[0]) for v in top[:P_TOPK]])
            thr = 0.5 * (top[P_TOPK - 1] + top[P_TOPK])
            n1 = jnp.zeros((N_KEYS, LANES), F32)
            for i in range(P_TOPK):
                cnt = jnp.sum(jnp.where(a_all[i:i + 1, :] + b_all >= thr, 1.0, 0.0), axis=0, keepdims=True)
                n1 = jnp.where(ranks[0] == float(i), cnt, n1)
            r2_ref[tc, h] = ranks[1].astype(BF16)
            e2_ref[tc, h] = (jnp.exp(s[1] - b_all[0:1, :]) / z).astype(BF16)
            n1_ref[tc, h] = n1
            e1_ref[tc, h] = jnp.exp(s[0] - a_all[0:1, :])


def _peerq(x, shift, scale, g, wq, keys, tm):
    B, L, D = x.shape
    T = B * L
    nb = L // tm
    tok_spec = pl.BlockSpec((tm // LANES, P_HEADS, N_KEYS, LANES), lambda b, i: (b * nb + i, 0, 0, 0))
    tok_shape = lambda dt: jax.ShapeDtypeStruct((T // LANES, P_HEADS, N_KEYS, LANES), dt)
    return pl.pallas_call(
        _peerq_kernel,
        grid=(B, nb),
        in_specs=[
            pl.BlockSpec((1, tm, D), lambda b, i: (b, i, 0)),
            pl.BlockSpec((1, 1, D), lambda b, i: (b, 0, 0)),
            pl.BlockSpec((1, 1, D), lambda b, i: (b, 0, 0)),
            pl.BlockSpec((1, D), lambda b, i: (0, 0)),
            pl.BlockSpec((D, 2 * P_HEADS * P_DKH), lambda b, i: (0, 0)),
            pl.BlockSpec((2 * P_HEADS, N_KEYS, P_DKH), lambda b, i: (0, 0, 0)),
        ],
        out_specs=[pl.BlockSpec((tm, D), lambda b, i: (b * nb + i, 0))] + [tok_spec] * 4,
        out_shape=[jax.ShapeDtypeStruct((T, D), BF16), tok_shape(BF16), tok_shape(BF16), tok_shape(F32),
                   tok_shape(F32)],
        scratch_shapes=[pltpu.VMEM((2, PEER_A_PAD, LANES), F32), pltpu.VMEM((PEER_CAND_ROWS, LANES), F32)],
        compiler_params=_cparams(("parallel", "parallel")),
        name="peer_query",
    )(x, shift, scale, g, wq, keys)


PEER_EC = 1024
PEER_TT = 1024
PEER_TSUB = 256
PEER_GATE_ROWS = 64


def _gelu(x):
    return 0.5 * x * (1.0 + lax.erf(x * math.sqrt(0.5)))


def _peer_kernel(xn_ref, u_ref, vt_ref, r2_ref, e2_ref, n1_ref, e1_ref, x_ref, gt_ref, fg_ref,
                 o_ref, acc_ref, act0_ref, act1_ref, pt0_ref, pt1_ref, *, final_norm):
    c = pl.program_id(1)
    act_refs, pt_refs = (act0_ref, act1_ref), (pt0_ref, pt1_ref)

    @pl.when(c == 0)
    def _():
        acc_ref[...] = jnp.zeros_like(acc_ref)

    TT = xn_ref.shape[0]
    tsub = min(PEER_TSUB, TT)
    n_sub = TT // tsub
    n_a = PEER_EC // N_KEYS

    def hidden(j):
        act_refs[j % 2][...] = _gelu(_dot_nt(u_ref[...], xn_ref[j * tsub:(j + 1) * tsub, :])).astype(BF16)

    def gated(j):
        act_ref, pt_ref = act_refs[j % 2], pt_refs[j % 2]
        for tl in range(tsub // LANES):
            tc = j * (tsub // LANES) + tl
            cols = slice(tl * LANES, (tl + 1) * LANES)
            n1s = [[n1_ref[tc, h, a:a + 1, :].astype(BF16) for h in range(P_HEADS)] for a in range(n_a)]
            e1s = [[e1_ref[tc, h, a:a + 1, :].astype(BF16) for h in range(P_HEADS)] for a in range(n_a)]
            for b0 in range(0, N_KEYS, PEER_GATE_ROWS):
                brows = slice(b0, b0 + PEER_GATE_ROWS)
                gates = [None] * n_a
                for h in range(P_HEADS):
                    r2 = r2_ref[tc, h, brows, :]
                    e2 = e2_ref[tc, h, brows, :]
                    for a in range(n_a):
                        t = (e1s[a][h] * jnp.clip(n1s[a][h] - r2, 0.0, 1.0)) * e2
                        gates[a] = t if gates[a] is None else gates[a] + t
                for a in range(n_a):
                    rows = slice(a * N_KEYS + b0, a * N_KEYS + b0 + PEER_GATE_ROWS)
                    pt_ref[rows, cols] = gates[a] * act_ref[rows, cols]

    def project(j):
        acc_ref[:, j * tsub:(j + 1) * tsub] += _dot(vt_ref[...], pt_refs[j % 2][...])

    hidden(0)
    for j in range(n_sub):
        if j + 1 < n_sub:
            hidden(j + 1)
        gated(j)
        project(j)

    @pl.when(c == pl.num_programs(1) - 1)
    def _():
        y = x_ref[...] + gt_ref[0] * acc_ref[...].T
        if final_norm:
            y = y * lax.rsqrt(jnp.mean(y * y, axis=-1, keepdims=True) + EPS) * fg_ref[...]
        o_ref[...] = y


def _peer(xn, r2, e2, n1, e1, u, vt, x, gate, final_g, final_norm):
    B, L, D = x.shape
    T = B * L
    TT = min(PEER_TT, L)
    tsub = min(PEER_TSUB, TT)
    per_b = L // TT
    tok_spec = pl.BlockSpec((TT // LANES, P_HEADS, N_KEYS, LANES), lambda i, c: (i, 0, 0, 0))
    key_spec = pl.BlockSpec((TT // LANES, P_HEADS, PEER_EC // N_KEYS, LANES), lambda i, c: (i, 0, c, 0))
    out = pl.pallas_call(
        functools.partial(_peer_kernel, final_norm=final_norm),
        grid=(T // TT, N_EXPERTS // PEER_EC),
        in_specs=[
            pl.BlockSpec((TT, D), lambda i, c: (i, 0)),
            pl.BlockSpec((PEER_EC, D), lambda i, c: (c, 0)),
            pl.BlockSpec((D, PEER_EC), lambda i, c: (0, c)),
            tok_spec, tok_spec, key_spec, key_spec,
            pl.BlockSpec((TT, D), lambda i, c: (i, 0)),
            pl.BlockSpec((1, 1, D), lambda i, c: (i // per_b, 0, 0)),
            pl.BlockSpec((1, D), lambda i, c: (0, 0)),
        ],
        out_specs=pl.BlockSpec((TT, D), lambda i, c: (i, 0)),
        out_shape=jax.ShapeDtypeStruct((T, D), F32),
        scratch_shapes=[pltpu.VMEM((D, TT), F32),
                        pltpu.VMEM((PEER_EC, tsub), BF16), pltpu.VMEM((PEER_EC, tsub), BF16),
                        pltpu.VMEM((PEER_EC, tsub), BF16), pltpu.VMEM((PEER_EC, tsub), BF16)],
        compiler_params=_cparams(("parallel", "arbitrary")),
        name="peer_experts",
    )(xn, u, vt, r2, e2, n1, e1, x.reshape(T, D), gate, final_g)
    return out.reshape(B, L, D)


GDN_CONV_TILE = 512
GDN_PAD = 8


def _gdn_prep_kernel(q_ref, k_ref, v_ref, cwq_ref, cwk_ref, cwv_ref, qo_ref, ko_ref, vo_ref, pad_ref):
    Ls = q_ref.shape[1]
    T = min(GDN_CONV_TILE, Ls)
    zeros = jnp.zeros((GDN_PAD, LANES), F32)
    pad_ref[0:GDN_PAD, :] = zeros
    pad_ref[GDN_PAD + Ls:2 * GDN_PAD + Ls, :] = zeros
    for which, (x_ref, cw_ref, o_ref) in enumerate(((q_ref, cwq_ref, qo_ref), (k_ref, cwk_ref, ko_ref),
                                                    (v_ref, cwv_ref, vo_ref))):
        pad_ref[GDN_PAD:GDN_PAD + Ls, :] = x_ref[0]
        for t0 in range(0, Ls, T):
            acc = None
            for j in range(CONV_K):
                off = GDN_PAD - CONV_K // 2 + j + t0
                t = cw_ref[j:j + 1, :] * pad_ref[off:off + T, :]
                acc = t if acc is None else acc + t
            y = acc * jax.nn.sigmoid(acc)
            if which < 2:
                y = y * lax.rsqrt(jnp.sum(y * y, axis=-1, keepdims=True) + EPS)
            if which == 0:
                y = y * (A_DK ** -0.5)
            o_ref[0, t0:t0 + T, :] = y


def _gdn_prep(proj, conv_w):
    B, Ls, _ = proj.shape
    seq = lambda col: pl.BlockSpec((1, Ls, LANES), lambda b, h: (b, 0, col + h))
    cw = lambda col: pl.BlockSpec((CONV_K, LANES), lambda b, h: (0, col + h))
    out = pl.BlockSpec((1, Ls, LANES), lambda b, h: (b, 0, h))
    shape = jax.ShapeDtypeStruct((B, Ls, A_HEADS * LANES), F32)
    return pl.pallas_call(
        _gdn_prep_kernel,
        grid=(B, A_HEADS),
        in_specs=[seq(COL_A_Q), seq(COL_A_K), seq(COL_A_V), cw(0), cw(A_HEADS), cw(2 * A_HEADS)],
        out_specs=[out] * 3,
        out_shape=[shape] * 3,
        scratch_shapes=[pltpu.VMEM((Ls + 2 * GDN_PAD, LANES), F32)],
        compiler_params=_cparams(("parallel", "parallel")),
        name="gdn_prep",
    )(proj, proj, proj, conv_w, conv_w, conv_w)


def _softplus(x):
    return jnp.maximum(x, 0.0) + jnp.log1p(jnp.exp(-jnp.abs(x)))


GDN_GROUP = 4


def _gdn_prepass_group(h, chunks, nega_ref, dtb_ref, scr):
    mneg_scr, c_scr, qp_scr, dl_scr, o_scr = scr
    C = CHUNK
    row = lax.broadcasted_iota(jnp.int32, (C, C), 0)
    col = lax.broadcasted_iota(jnp.int32, (C, C), 1)
    lane = lax.broadcasted_iota(jnp.int32, (1, LANES), 1)
    eye = row == col
    chains = []
    for q, k, v, ba, row0, chunk_id in chunks:
        kb16 = k.astype(BF16)
        qk_raw = _dot_nt(q.astype(BF16), kb16)
        for d in range(2):
            before_eq_rc = (col <= row) if d == 0 else (col >= row)
            before_rc = (col < row) if d == 0 else (col > row)
            before_eq_cr = (row <= col) if d == 0 else (row >= col)
            bcol = jnp.sum(jnp.where(lane == d * A_HEADS + h, ba, 0.0), axis=-1, keepdims=True)
            acol = jnp.sum(jnp.where(lane == (2 + d) * A_HEADS + h, ba, 0.0), axis=-1, keepdims=True)
            beta = jax.nn.sigmoid(bcol)
            g = nega_ref[d, h] * _softplus(acol + dtb_ref[d, h])
            g_cols = jnp.broadcast_to(g, (C, C))
            gc_row = jnp.sum(jnp.where(before_eq_cr, g_cols, 0.0), axis=0, keepdims=True)
            g_row = jnp.sum(jnp.where(eye, g_cols, 0.0), axis=0, keepdims=True)
            gc_col = jnp.sum(jnp.where(before_eq_rc, jnp.broadcast_to(g_row, (C, C)), 0.0), axis=-1, keepdims=True)
            g_total = jnp.sum(g_row, axis=-1, keepdims=True)
            decay = jnp.exp(jnp.where(before_eq_rc, gc_col - gc_row, NEG_INF))
            kbeta = k * beta
            e_gc = jnp.exp(gc_col)
            chains.append(dict(
                d=d, row0=row0, chunk_id=chunk_id,
                n_pow=-jnp.where(before_rc, _dot_nt(kbeta.astype(BF16), kb16) * decay, 0.0),
                qk=jnp.where(before_eq_rc, qk_raw * decay, 0.0).astype(BF16),
                rhs=jnp.concatenate([v * beta, kbeta * e_gc], axis=-1).astype(BF16),
                kd=(k * jnp.exp(g_total - gc_col)).astype(BF16),
                qd=q * e_gc,
                dl=jnp.exp(g_total)))
    eye_f = jnp.where(eye, 1.0, 0.0)
    n_pows = [ch['n_pow'] for ch in chains]
    invs = [eye_f + n for n in n_pows]
    for _ in range(5):
        n16 = [n.astype(BF16) for n in n_pows]
        n_pows = [_dot(n, n) for n in n16]
        n16 = [n.astype(BF16) for n in n_pows]
        invs = [inv + _dot(inv.astype(BF16), n) for inv, n in zip(invs, n16)]
    uws = [_dot(inv.astype(BF16), ch['rhs']).astype(BF16) for inv, ch in zip(invs, chains)]
    kts = [_dot_tn(ch['kd'], uw) for ch, uw in zip(chains, uws)]
    qqs = [_dot(ch['qk'], uw) for ch, uw in zip(chains, uws)]
    for ch, kt, qq in zip(chains, kts, qqs):
        d, cid = ch['d'], ch['chunk_id']
        rows = pl.ds(ch['row0'], C)
        c_scr[d, cid] = kt[:, :A_DV]
        mneg_scr[d, cid] = (-kt[:, A_DV:]).astype(BF16)
        o_scr[d, rows, :] = qq[:, :A_DV]
        qp_scr[d, rows, :] = (ch['qd'] - qq[:, A_DV:]).astype(BF16)
        dl_scr[d, pl.ds(cid * 8, 8), :] = jnp.broadcast_to(ch['dl'], (8, LANES))


def _gdn_scan_step(states, scr, row0s, chunk_ids):
    mneg_scr, c_scr, qp_scr, dl_scr, o_scr = scr
    s16 = [S.astype(BF16) for S in states]
    upd = [_dot(mneg_scr[d, chunk_ids[d]], s16[d]) for d in range(2)]
    for d in range(2):
        rows = pl.ds(row0s[d], CHUNK)
        o_scr[d, rows, :] += _dot(qp_scr[d, rows, :], s16[d])
    return tuple(states[d] * dl_scr[d, pl.ds(chunk_ids[d] * 8, 1), :] + c_scr[d, chunk_ids[d]] + upd[d]
                 for d in range(2))


def _gdn_scan_kernel(nega_ref, dtb_ref, qc_ref, kc_ref, vc_ref, bac_ref, zc_ref, ql_ref, kl_ref, vl_ref, bal_ref,
                     zl_ref, ng_ref, yl_ref, yc_ref, mneg_scr, c_scr, qp_scr, dl_scr, o_scr):
    h = pl.program_id(1)
    Lc, L = qc_ref.shape[1], ql_ref.shape[1]
    nc, nl = Lc // CHUNK, L // CHUNK
    scr = (mneg_scr, c_scr, qp_scr, dl_scr, o_scr)

    for c0 in range(0, nc, GDN_GROUP):
        _gdn_prepass_group(h, [(qc_ref[0, c * CHUNK:(c + 1) * CHUNK, :], kc_ref[0, c * CHUNK:(c + 1) * CHUNK, :],
                                vc_ref[0, c * CHUNK:(c + 1) * CHUNK, :], bac_ref[0, c * CHUNK:(c + 1) * CHUNK, :],
                                c * CHUNK, c) for c in range(c0, min(c0 + GDN_GROUP, nc))],
                           nega_ref, dtb_ref, scr)

    def pre_body(grp, carry):
        chunks = []
        for j in range(GDN_GROUP):
            c = grp * GDN_GROUP + j
            r = pl.multiple_of(c * CHUNK, CHUNK)
            rows = pl.ds(r, CHUNK)
            chunks.append((ql_ref[0, rows, :], kl_ref[0, rows, :], vl_ref[0, rows, :], bal_ref[0, rows, :],
                           Lc + r, nc + c))
        _gdn_prepass_group(h, chunks, nega_ref, dtb_ref, scr)
        return carry

    lax.fori_loop(0, nl // GDN_GROUP, pre_body, 0)

    states = (jnp.zeros((A_DK, A_DV), F32), jnp.zeros((A_DK, A_DV), F32))
    for s in range(nc):
        states = _gdn_scan_step(states, scr, (s * CHUNK, (nc - 1 - s) * CHUNK), (s, nc - 1 - s))

    def scan_body(s, states):
        cf = s
        cb = nl - 1 - s
        return _gdn_scan_step(states, scr,
                              (pl.multiple_of(Lc + cf * CHUNK, CHUNK), pl.multiple_of(Lc + cb * CHUNK, CHUNK)),
                              (nc + cf, nc + cb))

    lax.fori_loop(0, nl, scan_body, states)

    def finish(z_ref, y_ref, base, n):
        T = min(GDN_CONV_TILE, n)
        for t0 in range(0, n, T):
            o = o_scr[0, base + t0:base + t0 + T, :] + o_scr[1, base + t0:base + t0 + T, :]
            y = o * lax.rsqrt(jnp.mean(o * o, axis=-1, keepdims=True) + EPS) * ng_ref[...]
            z = z_ref[0, t0:t0 + T, :]
            y_ref[0, t0:t0 + T, :] = y * (z * jax.nn.sigmoid(z))

    finish(zl_ref, yl_ref, Lc, L)
    finish(zc_ref, yc_ref, 0, Lc)


def _gdn_scan(nega, dtb, qkv_ctx, ba_ctx, proj_ctx, qkv_lat, ba_lat, proj_lat, norm_g):
    B, L, _ = proj_lat.shape
    Lc = proj_ctx.shape[1]
    Lt = L + Lc
    head = lambda n: pl.BlockSpec((1, n, LANES), lambda b, h: (b, 0, h))
    full = lambda n: pl.BlockSpec((1, n, LANES), lambda b, h: (b, 0, 0))
    zcol = lambda n: pl.BlockSpec((1, n, LANES), lambda b, h: (b, 0, COL_A_Z + h))
    smem = pl.BlockSpec(memory_space=pltpu.SMEM)
    return pl.pallas_call(
        _gdn_scan_kernel,
        grid=(B, A_HEADS),
        in_specs=[smem, smem,
                  head(Lc), head(Lc), head(Lc), full(Lc), zcol(Lc),
                  head(L), head(L), head(L), full(L), zcol(L),
                  pl.BlockSpec((1, LANES), lambda b, h: (0, 0))],
        out_specs=[head(L), head(Lc)],
        out_shape=[jax.ShapeDtypeStruct((B, L, A_HEADS * LANES), F32),
                   jax.ShapeDtypeStruct((B, Lc, A_HEADS * LANES), F32)],
        scratch_shapes=[
            pltpu.VMEM((2, Lt // CHUNK, A_DK, A_DV), BF16),
            pltpu.VMEM((2, Lt // CHUNK, A_DK, A_DV), F32),
            pltpu.VMEM((2, Lt, A_DK), BF16),
            pltpu.VMEM((2, Lt // CHUNK * 8, LANES), F32),
            pltpu.VMEM((2, Lt, A_DV), F32),
        ],
        compiler_params=_cparams(("parallel", "parallel")),
        name="gdn_scan",
    )(nega, dtb, *qkv_ctx, ba_ctx, proj_ctx, *qkv_lat, ba_lat, proj_lat, norm_g)


def _gdn_pallas(proj_lat, ba_lat, proj_ctx, ba_ctx, conv_w, a_log, dt_bias, norm_g):
    qkv_lat = _gdn_prep(proj_lat, conv_w)
    qkv_ctx = _gdn_prep(proj_ctx, conv_w)
    return _gdn_scan(-jnp.exp(a_log), dt_bias, qkv_ctx, ba_ctx, proj_ctx, qkv_lat, ba_lat, proj_lat, norm_g[None, :])


def _layer(i, x, xc, c, c_ctx, p, cos_t, sin_t, ctx_out, final_g, final_norm):
    B, L, D = x.shape
    Lc = xc.shape[1]
    mod_lat = (jax.nn.silu(c) @ p['ada_w'] + p['ada_b'])[:, None, :]
    mod_ctx = jnp.broadcast_to((jax.nn.silu(c_ctx) @ p['ada_w'] + p['ada_b'])[None, None, :], (B, 1, 6 * D))
    sh1, sc1, gt1, sh2, sc2, gt2 = jnp.split(mod_lat, 6, axis=-1)
    csh1, csc1, cgt1, csh2, csc2, cgt2 = jnp.split(mod_ctx, 6, axis=-1)
    lam_init = 0.8 - 0.6 * math.exp(-0.3 * i)

    w_in = p['w_in']
    n_a = 4 * A_HEADS * A_DK
    w_main = jnp.concatenate([w_in[:, :n_a], w_in[:, n_a + 4 * A_HEADS:]], axis=1).astype(BF16)
    w_ba = jnp.pad(w_in[:, n_a:n_a + 4 * A_HEADS], ((0, 0), (0, LANES - 4 * A_HEADS))).astype(BF16)
    n1 = p['norm1_g'][None, :]
    proj_lat, ba_lat = _inproj(x, sh1, sc1, n1, w_main, w_ba, 512)
    proj_ctx, ba_ctx = _inproj(xc, csh1, csc1, n1, w_main, w_ba, Lc)

    ya, ya_c = _gdn_pallas(proj_lat, ba_lat, proj_ctx, ba_ctx, p['gdn_conv'], p['gdn_a_log'], p['gdn_dt_bias'],
                           p['gdn_norm_g'])

    lp = p['diff_lambda']
    lam = (jnp.exp(jnp.sum(lp[0] * lp[1])) - jnp.exp(jnp.sum(lp[2] * lp[3])) + lam_init).reshape(1)
    q_r, k_r = _rope(proj_lat, cos_t, sin_t, 512)
    subln = p['diff_subln_g'][None, :]
    yb = _diff_lat(lam, q_r, k_r, proj_lat, proj_ctx, subln, 1.0 - lam_init, 512)
    yb_c = _diff_ctx(lam, proj_ctx, subln, 1.0 - lam_init) if ctx_out else None

    yc = _natten_lat(proj_lat, proj_ctx, _natten_bias(p['na_rpb']))
    yc_c = _natten_ctx(proj_ctx) if ctx_out else None

    w_up = p['w_up'].astype(BF16)
    w_out = p['w_out'].astype(BF16)
    wq = p['peer_wq'].astype(BF16)
    keys = p['peer_keys'].reshape(2 * P_HEADS, N_KEYS, P_DKH).astype(BF16)
    u = p['peer_u'].astype(BF16)
    vt = p['peer_v'].T.astype(BF16)
    n2 = p['norm2_g'][None, :]
    fg = final_g[None, :]

    x = _merge(ya, yb, yc, proj_lat, w_up, w_out, x, gt1, 512)
    pq = _peerq(x, sh2, sc2, n2, wq, keys, 256)
    x = _peer(*pq, u, vt, x, gt2, fg, final_norm)
    if ctx_out:
        xc = _merge(ya_c, yb_c, yc_c, proj_ctx, w_up, w_out, xc, cgt1, Lc)
        pq = _peerq(xc, csh2, csc2, n2, wq, keys, Lc)
        xc = _peer(*pq, u, vt, xc, cgt2, fg, False)
    return x, xc


def kernel(x, c, ctx, c_ctx, norm1_g, norm2_g, ada_w, ada_b, w_in, gdn_conv, gdn_a_log, gdn_dt_bias, gdn_norm_g,
           diff_lambda, diff_subln_g, na_rpb, w_up, w_out, peer_wq, peer_keys, peer_u, peer_v, final_g):
    cos_t, sin_t = _rope_tables(x.shape[1])
    xc = ctx
    for i in range(DEPTH):
        p = dict(norm1_g=norm1_g[i], norm2_g=norm2_g[i], ada_w=ada_w[i], ada_b=ada_b[i], w_in=w_in[i],
                 gdn_conv=gdn_conv[i], gdn_a_log=gdn_a_log[i], gdn_dt_bias=gdn_dt_bias[i], gdn_norm_g=gdn_norm_g[i],
                 diff_lambda=diff_lambda[i], diff_subln_g=diff_subln_g[i], na_rpb=na_rpb[i], w_up=w_up[i],
                 w_out=w_out[i], peer_wq=peer_wq[i], peer_keys=peer_keys[i], peer_u=peer_u[i], peer_v=peer_v[i])
        x, xc = _layer(i, x, xc, c, c_ctx, p, cos_t, sin_t, i < DEPTH - 1, final_g, i == DEPTH - 1)
    return x
```

```python
import functools
import math

import numpy as np
import jax
import jax.numpy as jnp
from jax import lax
from jax.experimental import pallas as pl
from jax.experimental.pallas import tpu as pltpu

F32 = jnp.float32
BF16 = jnp.bfloat16

D_MODEL = 1024
DEPTH = 2
GRID_W = 64
EPS = 1e-6
NEG_INF = -1e30

A_HEADS = 4
A_DK = 128
A_DV = 128
CONV_K = 5
CHUNK = 64
B_HEADS = 4
B_DH = 64
ROPE_BASE = 10000.0
C_HEADS = 8
C_DH = 64
WIN_R = 8
WIN_C = 16
N_BRANCH = 3
BRANCH_W = 512
P_HEADS = 8
N_KEYS = 128
N_EXPERTS = N_KEYS * N_KEYS
P_DKH = 128
P_TOPK = 16

LANES = 128
VMEM_LIMIT = 56 * 1024 * 1024

COL_A_Q, COL_A_K, COL_A_V, COL_A_Z = 0, 4, 8, 12
COL_B_Q, COL_B_K, COL_B_V = 16, 20, 24
COL_C_Q, COL_C_K, COL_C_V = 28, 32, 36
COL_GATE = 40
MAIN_COLS = 64 * LANES


def _cparams(sem):
    return pltpu.CompilerParams(dimension_semantics=sem, vmem_limit_bytes=VMEM_LIMIT)


def _dot(a, b):
    return jnp.dot(a, b, preferred_element_type=F32)


def _dot_nt(a, b):
    return lax.dot_general(a, b, (((1,), (1,)), ((), ())), preferred_element_type=F32)


def _dot_tn(a, b):
    return lax.dot_general(a, b, (((0,), (0,)), ((), ())), preferred_element_type=F32)


def _inproj_kernel(x_ref, sh_ref, sc_ref, g_ref, w_ref, wba_ref, o_ref, oba_ref, hn_ref):
    @pl.when(pl.program_id(2) == 0)
    def _():
        x = x_ref[0]
        y = x * lax.rsqrt(jnp.mean(x * x, axis=-1, keepdims=True) + EPS) * g_ref[...]
        h = (y * (1.0 + sc_ref[0]) + sh_ref[0]).astype(BF16)
        hn_ref[...] = h
        oba_ref[0] = _dot(h, wba_ref[...])

    o_ref[0] = _dot(hn_ref[...], w_ref[...])


def _inproj(x, shift, scale, g, w_main, w_ba, tm):
    B, L, D = x.shape
    tn = 1024
    return pl.pallas_call(
        _inproj_kernel,
        grid=(B, L // tm, MAIN_COLS // tn),
        in_specs=[
            pl.BlockSpec((1, tm, D), lambda b, i, j: (b, i, 0)),
            pl.BlockSpec((1, 1, D), lambda b, i, j: (b, 0, 0)),
            pl.BlockSpec((1, 1, D), lambda b, i, j: (b, 0, 0)),
            pl.BlockSpec((1, D), lambda b, i, j: (0, 0)),
            pl.BlockSpec((D, tn), lambda b, i, j: (0, j)),
            pl.BlockSpec((D, LANES), lambda b, i, j: (0, 0)),
        ],
        out_specs=[
            pl.BlockSpec((1, tm, tn), lambda b, i, j: (b, i, j)),
            pl.BlockSpec((1, tm, LANES), lambda b, i, j: (b, i, 0)),
        ],
        out_shape=[
            jax.ShapeDtypeStruct((B, L, MAIN_COLS), F32),
            jax.ShapeDtypeStruct((B, L, LANES), F32),
        ],
        scratch_shapes=[pltpu.VMEM((tm, D), BF16)],
        compiler_params=_cparams(("parallel", "parallel", "arbitrary")),
        name="inproj",
    )(x, shift, scale, g, w_main, w_ba)


def _rope_tables(L):
    t = np.arange(L)
    row_pos, col_pos = t // GRID_W, t % GRID_W
    lane = np.arange(LANES)
    axis = (lane % 64) // 32
    f = lane % 16
    inv = 1.0 / (ROPE_BASE ** (f.astype(np.float32) / 16.0))
    pos = np.where(axis[None, :] == 0, row_pos[:, None], col_pos[:, None]).astype(np.float32)
    ang = jnp.asarray(pos) * jnp.asarray(inv.astype(np.float32))[None, :]
    first = jnp.asarray(((lane % 32) < 16)[None, :])
    return jnp.cos(ang), jnp.where(first, -jnp.sin(ang), jnp.sin(ang))


def _rope_kernel(q_ref, k_ref, cos_ref, sin_ref, qo_ref, ko_ref):
    lane = lax.broadcasted_iota(jnp.int32, (1, LANES), 1)
    first = (lane % 32) < 16
    c, s = cos_ref[...], sin_ref[...]

    def rope(x):
        partner = jnp.where(first, pltpu.roll(x, LANES - 16, 1), pltpu.roll(x, 16, 1))
        return x * c + partner * s

    for h in range(B_HEADS):
        sl = slice(h * LANES, (h + 1) * LANES)
        qo_ref[0, :, sl] = (rope(q_ref[0, :, sl]) * (B_DH ** -0.5)).astype(BF16)
        ko_ref[0, :, sl] = rope(k_ref[0, :, sl]).astype(BF16)


def _rope(proj, cos_t, sin_t, tr):
    B, L, _ = proj.shape
    W = B_HEADS * LANES
    return pl.pallas_call(
        _rope_kernel,
        grid=(B, L // tr),
        in_specs=[
            pl.BlockSpec((1, tr, W), lambda b, i: (b, i, COL_B_Q * LANES // W)),
            pl.BlockSpec((1, tr, W), lambda b, i: (b, i, COL_B_K * LANES // W)),
            pl.BlockSpec((tr, LANES), lambda b, i: (i, 0)),
            pl.BlockSpec((tr, LANES), lambda b, i: (i, 0)),
        ],
        out_specs=[pl.BlockSpec((1, tr, W), lambda b, i: (b, i, 0))] * 2,
        out_shape=[jax.ShapeDtypeStruct((B, L, W), BF16)] * 2,
        compiler_params=_cparams(("parallel", "parallel")),
        name="rope",
    )(proj, proj, cos_t, sin_t)


def _softmax_pv(q, key_vals):
    scores = [_dot_nt(q, k) for k, _ in key_vals]
    m = functools.reduce(jnp.maximum, [s.max(axis=-1, keepdims=True) for s in scores])
    es = [jnp.exp(s - m) for s in scores]
    denom = functools.reduce(jnp.add, [e.sum(axis=-1, keepdims=True) for e in es])
    o = functools.reduce(jnp.add, [_dot(e.astype(BF16), v) for e, (_, v) in zip(es, key_vals)])
    return o / denom


DIFF_Q_ROWS = 256


def _diff_finish(o0, o1, lam, g, out_scale):
    o = o0 - lam * o1
    return o * lax.rsqrt(jnp.mean(o * o, axis=-1, keepdims=True) + EPS) * g * out_scale


def _diff_lat_kernel(lam_ref, q_ref, kl_ref, vl_ref, kc_ref, vc_ref, g_ref, o_ref, *, out_scale):
    lane = lax.broadcasted_iota(jnp.int32, (1, LANES), 1)
    low = lane < B_DH
    tq = q_ref.shape[1]
    kl, vl = kl_ref[0], vl_ref[0].astype(BF16)
    kc, vc = kc_ref[0].astype(BF16), vc_ref[0].astype(BF16)
    units = [dict(r0=r0, mp=mp) for r0 in range(0, tq, DIFF_Q_ROWS) for mp in range(2)]

    def scores(un):
        q = q_ref[0, un['r0']:un['r0'] + DIFF_Q_ROWS, :]
        q = (jnp.where(low, q, 0) if un['mp'] == 0 else jnp.where(low, 0, q)).astype(BF16)
        un['s'] = (_dot_nt(q, kl), _dot_nt(q, kc))

    def softmax(un):
        s_l, s_c = un.pop('s')
        m = jnp.maximum(s_l.max(axis=-1, keepdims=True), s_c.max(axis=-1, keepdims=True))
        e_l, e_c = jnp.exp(s_l - m), jnp.exp(s_c - m)
        un['denom'] = e_l.sum(axis=-1, keepdims=True) + e_c.sum(axis=-1, keepdims=True)
        un['e'] = (e_l.astype(BF16), e_c.astype(BF16))

    def values(un):
        e_l, e_c = un.pop('e')
        un['o'] = (_dot(e_l, vl) + _dot(e_c, vc)) / un['denom']

    scores(units[0])
    scores(units[1])
    for k, un in enumerate(units):
        softmax(un)
        if k + 2 < len(units):
            scores(units[k + 2])
        values(un)
    for k in range(0, len(units), 2):
        r0 = units[k]['r0']
        o_ref[0, r0:r0 + DIFF_Q_ROWS, :] = _diff_finish(units[k]['o'], units[k + 1]['o'], lam_ref[0], g_ref[...],
                                                        out_scale)


def _diff_lat(lam, q_r, k_r, proj_lat, proj_ctx, subln_g, out_scale, tq):
    B, L, _ = proj_lat.shape
    Lc = proj_ctx.shape[1]
    return pl.pallas_call(
        functools.partial(_diff_lat_kernel, out_scale=out_scale),
        grid=(B, B_HEADS, L // tq),
        in_specs=[
            pl.BlockSpec(memory_space=pltpu.SMEM),
            pl.BlockSpec((1, tq, LANES), lambda b, h, i: (b, i, h)),
            pl.BlockSpec((1, L, LANES), lambda b, h, i: (b, 0, h)),
            pl.BlockSpec((1, L, LANES), lambda b, h, i: (b, 0, COL_B_V + h)),
            pl.BlockSpec((1, Lc, LANES), lambda b, h, i: (b, 0, COL_B_K + h)),
            pl.BlockSpec((1, Lc, LANES), lambda b, h, i: (b, 0, COL_B_V + h)),
            pl.BlockSpec((1, LANES), lambda b, h, i: (0, 0)),
        ],
        out_specs=pl.BlockSpec((1, tq, LANES), lambda b, h, i: (b, i, h)),
        out_shape=jax.ShapeDtypeStruct((B, L, B_HEADS * LANES), F32),
        compiler_params=_cparams(("parallel", "parallel", "parallel")),
        name="diff_lat",
    )(lam, q_r, k_r, proj_lat, proj_ctx, proj_ctx, subln_g)


def _diff_ctx_kernel(lam_ref, q_ref, k_ref, v_ref, g_ref, o_ref, *, out_scale):
    lane = lax.broadcasted_iota(jnp.int32, (1, LANES), 1)
    low = lane < B_DH
    q = q_ref[0] * (B_DH ** -0.5)
    kv = [(k_ref[0].astype(BF16), v_ref[0].astype(BF16))]
    o0 = _softmax_pv(jnp.where(low, q, 0).astype(BF16), kv)
    o1 = _softmax_pv(jnp.where(low, 0, q).astype(BF16), kv)
    o_ref[0] = _diff_finish(o0, o1, lam_ref[0], g_ref[...], out_scale)


def _diff_ctx(lam, proj_ctx, subln_g, out_scale):
    B, Lc, _ = proj_ctx.shape
    return pl.pallas_call(
        functools.partial(_diff_ctx_kernel, out_scale=out_scale),
        grid=(B, B_HEADS),
        in_specs=[
            pl.BlockSpec(memory_space=pltpu.SMEM),
            pl.BlockSpec((1, Lc, LANES), lambda b, h: (b, 0, COL_B_Q + h)),
            pl.BlockSpec((1, Lc, LANES), lambda b, h: (b, 0, COL_B_K + h)),
            pl.BlockSpec((1, Lc, LANES), lambda b, h: (b, 0, COL_B_V + h)),
            pl.BlockSpec((1, LANES), lambda b, h: (0, 0)),
        ],
        out_specs=pl.BlockSpec((1, Lc, LANES), lambda b, h: (b, 0, h)),
        out_shape=jax.ShapeDtypeStruct((B, Lc, B_HEADS * LANES), F32),
        compiler_params=_cparams(("parallel", "parallel")),
        name="diff_ctx",
    )(lam, proj_ctx, proj_ctx, proj_ctx, subln_g)


NA_ROWS_PER_STEP = 8
NA_KEYS = WIN_R * GRID_W


def _natten_bias(rpb):
    col = np.arange(GRID_W)
    col_start = np.clip(col - WIN_C // 2, 0, GRID_W - WIN_C)
    col_mask = (col[None, :] >= col_start[:, None]) & (col[None, :] < col_start[:, None] + WIN_C)
    dc = np.clip(col[None, :] - col[:, None], -(WIN_C - 1), WIN_C - 1) + WIN_C - 1
    bias = jnp.where(jnp.asarray(col_mask), rpb.astype(F32)[:, :, dc], NEG_INF)
    return jnp.concatenate([bias[:, :-1], bias[:, 1:]], axis=-1)


def _natten_kernel(q_ref, k_ref, v_ref, kc_ref, vc_ref, bias_ref, o_ref, *, rows):
    lane = lax.broadcasted_iota(jnp.int32, (1, LANES), 1)
    low = lane < C_DH
    kc = kc_ref[0].astype(BF16)
    vc = vc_ref[0].astype(BF16)
    units = []
    for rr in range(NA_ROWS_PER_STEP):
        r = pl.program_id(2) * NA_ROWS_PER_STEP + rr
        rs = jnp.clip(r - WIN_R // 2, 0, rows - WIN_R)
        cfg = r - rs
        start = pl.multiple_of(rs * GRID_W, GRID_W)
        q = q_ref[0, rr * GRID_W:(rr + 1) * GRID_W, :] * (C_DH ** -0.5)
        q2 = jnp.concatenate([jnp.where(low, q, 0), jnp.where(low, 0, q)], axis=0).astype(BF16)
        bias = jnp.concatenate(
            [jnp.concatenate([bias_ref[hh, WIN_R - 1 - cfg + j] for j in range(0, WIN_R, 2)], axis=-1)
             for hh in range(2)], axis=0)
        units.append(dict(q=q2, bias=bias, start=start))
    for un in units:
        kw = k_ref[0, pl.ds(un['start'], NA_KEYS), :].astype(BF16)
        un['s_lat'] = _dot_nt(un['q'], kw) + un['bias']
        un['s_ctx'] = _dot_nt(un['q'], kc)
    for un in units:
        s_lat, s_ctx = un['s_lat'], un['s_ctx']
        m = jnp.maximum(s_lat.max(axis=-1, keepdims=True), s_ctx.max(axis=-1, keepdims=True))
        e_lat, e_ctx = jnp.exp(s_lat - m), jnp.exp(s_ctx - m)
        un['denom'] = e_lat.sum(axis=-1, keepdims=True) + e_ctx.sum(axis=-1, keepdims=True)
        un['e_lat'], un['e_ctx'] = e_lat.astype(BF16), e_ctx.astype(BF16)
    for rr, un in enumerate(units):
        vw = v_ref[0, pl.ds(un['start'], NA_KEYS), :].astype(BF16)
        o = (_dot(un['e_lat'], vw) + _dot(un['e_ctx'], vc)) / un['denom']
        o_ref[0, rr * GRID_W:(rr + 1) * GRID_W, :] = jnp.where(low, o[:GRID_W], o[GRID_W:])


def _natten_lat(proj_lat, proj_ctx, bias):
    B, L, _ = proj_lat.shape
    Lc = proj_ctx.shape[1]
    rows = L // GRID_W
    HP = C_HEADS // 2
    tq = NA_ROWS_PER_STEP * GRID_W
    return pl.pallas_call(
        functools.partial(_natten_kernel, rows=rows),
        grid=(B, HP, rows // NA_ROWS_PER_STEP),
        in_specs=[
            pl.BlockSpec((1, tq, LANES), lambda b, h, i: (b, i, COL_C_Q + h)),
            pl.BlockSpec((1, L, LANES), lambda b, h, i: (b, 0, COL_C_K + h)),
            pl.BlockSpec((1, L, LANES), lambda b, h, i: (b, 0, COL_C_V + h)),
            pl.BlockSpec((1, Lc, LANES), lambda b, h, i: (b, 0, COL_C_K + h)),
            pl.BlockSpec((1, Lc, LANES), lambda b, h, i: (b, 0, COL_C_V + h)),
            pl.BlockSpec((2, 2 * WIN_R - 2, GRID_W, 2 * GRID_W), lambda b, h, i: (h, 0, 0, 0)),
        ],
        out_specs=pl.BlockSpec((1, tq, LANES), lambda b, h, i: (b, i, h)),
        out_shape=jax.ShapeDtypeStruct((B, L, HP * LANES), F32),
        compiler_params=_cparams(("parallel", "parallel", "parallel")),
        name="natten_lat",
    )(proj_lat, proj_lat, proj_lat, proj_ctx, proj_ctx, bias)


def _natten_ctx_kernel(q_ref, k_ref, v_ref, o_ref):
    lane = lax.broadcasted_iota(jnp.int32, (1, LANES), 1)
    low = lane < C_DH
    q = q_ref[0] * (C_DH ** -0.5)
    kv = [(k_ref[0].astype(BF16), v_ref[0].astype(BF16))]
    o0 = _softmax_pv(jnp.where(low, q, 0).astype(BF16), kv)
    o1 = _softmax_pv(jnp.where(low, 0, q).astype(BF16), kv)
    o_ref[0] = jnp.where(low, o0, o1)


def _natten_ctx(proj_ctx):
    B, Lc, _ = proj_ctx.shape
    HP = C_HEADS // 2
    return pl.pallas_call(
        _natten_ctx_kernel,
        grid=(B, HP),
        in_specs=[
            pl.BlockSpec((1, Lc, LANES), lambda b, h: (b, 0, COL_C_Q + h)),
            pl.BlockSpec((1, Lc, LANES), lambda b, h: (b, 0, COL_C_K + h)),
            pl.BlockSpec((1, Lc, LANES), lambda b, h: (b, 0, COL_C_V + h)),
        ],
        out_specs=pl.BlockSpec((1, Lc, LANES), lambda b, h: (b, 0, h)),
        out_shape=jax.ShapeDtypeStruct((B, Lc, HP * LANES), F32),
        compiler_params=_cparams(("parallel", "parallel")),
        name="natten_ctx",
    )(proj_ctx, proj_ctx, proj_ctx)


def _merge_kernel(ya_ref, yb_ref, yc_ref, g0_ref, g1_ref, g2_ref, wup_ref, wout_ref, x_ref, gt_ref, o_ref):
    acc = None
    for n, (y_ref, g_ref) in enumerate(((ya_ref, g0_ref), (yb_ref, g1_ref), (yc_ref, g2_ref))):
        up = _dot(y_ref[0].astype(BF16), wup_ref[n])
        t = jax.nn.sigmoid(g_ref[0]) * up
        acc = t if acc is None else acc + t
    r = _dot(acc.astype(BF16), wout_ref[...])
    o_ref[0] = x_ref[0] + gt_ref[0] * r


def _merge(ya, yb, yc, proj, w_up, w_out, x, gate, tm):
    B, L, D = x.shape
    gcol = COL_GATE * LANES // D
    yspec = pl.BlockSpec((1, tm, BRANCH_W), lambda b, i: (b, i, 0))
    return pl.pallas_call(
        _merge_kernel,
        grid=(B, L // tm),
        in_specs=[
            yspec, yspec, yspec,
            pl.BlockSpec((1, tm, D), lambda b, i: (b, i, gcol)),
            pl.BlockSpec((1, tm, D), lambda b, i: (b, i, gcol + 1)),
            pl.BlockSpec((1, tm, D), lambda b, i: (b, i, gcol + 2)),
            pl.BlockSpec((N_BRANCH, BRANCH_W, D), lambda b, i: (0, 0, 0)),
            pl.BlockSpec((D, D), lambda b, i: (0, 0)),
            pl.BlockSpec((1, tm, D), lambda b, i: (b, i, 0)),
            pl.BlockSpec((1, 1, D), lambda b, i: (b, 0, 0)),
        ],
        out_specs=pl.BlockSpec((1, tm, D), lambda b, i: (b, i, 0)),
        out_shape=jax.ShapeDtypeStruct((B, L, D), F32),
        compiler_params=_cparams(("parallel", "parallel")),
        name="merge",
    )(ya, yb, yc, proj, proj, proj, w_up, w_out, x, gate)


PEER_CAND = P_TOPK + 1
PEER_A_PAD = 24
PEER_CAND_ROWS = PEER_A_PAD + 7 * 8 + 16


PEER_NO_RANK = float(N_KEYS)


def _extract_top(s, n, with_rank=False):
    vals = []
    rank = jnp.full(s.shape, PEER_NO_RANK, F32) if with_rank else None
    for r in range(n):
        m = jnp.max(s, axis=0, keepdims=True)
        vals.append(m)
        hit = s == m
        if with_rank:
            rank = jnp.where(hit, float(r), rank)
        s = jnp.where(hit, NEG_INF, s)
    return (vals, rank) if with_rank else vals


def _peerq_kernel(x_ref, sh_ref, sc_ref, g_ref, wq_ref, keys_ref,
                  xn_ref, r2_ref, e2_ref, n1_ref, e1_ref, ab_ref, cand_ref):
    x = x_ref[0]
    tm = x.shape[0]
    y = x * lax.rsqrt(jnp.mean(x * x, axis=-1, keepdims=True) + EPS) * g_ref[...]
    xn = (y * (1.0 + sc_ref[0]) + sh_ref[0]).astype(BF16)
    xn_ref[...] = xn
    q = _dot(xn, wq_ref[...]).astype(BF16)
    row8 = lax.broadcasted_iota(jnp.int32, (8, 1), 0)
    row16 = lax.broadcasted_iota(jnp.int32, (16, 1), 0)
    for h in range(P_HEADS):
        for tc in range(tm // LANES):
            qt = q[tc * LANES:(tc + 1) * LANES, :]
            s = []
            for p in range(2):
                hp = 2 * h + p
                st = _dot_nt(keys_ref[hp], qt[:, hp * P_DKH:(hp + 1) * P_DKH])
                s.append(st)
                ab_ref[p] = jnp.full((PEER_A_PAD, LANES), NEG_INF, F32)
                if p == 0:
                    vals = _extract_top(st, PEER_CAND)
                else:
                    vals, rank2 = _extract_top(st, PEER_CAND, with_rank=True)
                for r, m in enumerate(vals):
                    ab_ref[p, r:r + 1, :] = m
            a_all, b_all = ab_ref[0], ab_ref[1]
            cand_ref[0:PEER_A_PAD, :] = a_all[0:1, :] + b_all
            for i in range(1, 8):
                n_i = PEER_CAND // (i + 1)
                cand_ref[PEER_A_PAD + 8 * (i - 1):PEER_A_PAD + 8 * i, :] = jnp.where(
                    row8 < n_i, a_all[i:i + 1, :] + b_all[0:8, :], NEG_INF)
            cand_ref[PEER_A_PAD + 56:PEER_A_PAD + 72, :] = jnp.where(
                row16 + 8 < PEER_CAND, a_all[8:24, :] + b_all[0:1, :], NEG_INF)
            top = _extract_top(cand_ref[...], PEER_CAND)
            z = functools.reduce(jnp.add, [jnp.exp(v - top[0]) for v in top[:P_TOPK]])
            thr = 0.5 * (top[P_TOPK - 1] + top[P_TOPK])
            n1 = jnp.zeros((N_KEYS, LANES), F32)
            for i in range(P_TOPK):
                a_i = a_all[i:i + 1, :]
                cnt = jnp.sum(jnp.where(a_i + b_all >= thr, 1.0, 0.0), axis=0, keepdims=True)
                n1 = jnp.where(s[0] == a_i, cnt, n1)
            r2_ref[tc, h] = rank2.astype(BF16)
            e2_ref[tc, h] = (jnp.exp(s[1] - b_all[0:1, :]) / z).astype(BF16)
            n1_ref[tc, h] = n1
            e1_ref[tc, h] = jnp.exp(s[0] - a_all[0:1, :])


def _peerq(x, shift, scale, g, wq, keys, tm):
    B, L, D = x.shape
    T = B * L
    nb = L // tm
    tok_spec = pl.BlockSpec((tm // LANES, P_HEADS, N_KEYS, LANES), lambda b, i: (b * nb + i, 0, 0, 0))
    tok_shape = lambda dt: jax.ShapeDtypeStruct((T // LANES, P_HEADS, N_KEYS, LANES), dt)
    return pl.pallas_call(
        _peerq_kernel,
        grid=(B, nb),
        in_specs=[
            pl.BlockSpec((1, tm, D), lambda b, i: (b, i, 0)),
            pl.BlockSpec((1, 1, D), lambda b, i: (b, 0, 0)),
            pl.BlockSpec((1, 1, D), lambda b, i: (b, 0, 0)),
            pl.BlockSpec((1, D), lambda b, i: (0, 0)),
            pl.BlockSpec((D, 2 * P_HEADS * P_DKH), lambda b, i: (0, 0)),
            pl.BlockSpec((2 * P_HEADS, N_KEYS, P_DKH), lambda b, i: (0, 0, 0)),
        ],
        out_specs=[pl.BlockSpec((tm, D), lambda b, i: (b * nb + i, 0))] + [tok_spec] * 4,
        out_shape=[jax.ShapeDtypeStruct((T, D), BF16), tok_shape(BF16), tok_shape(BF16), tok_shape(F32),
                   tok_shape(F32)],
        scratch_shapes=[pltpu.VMEM((2, PEER_A_PAD, LANES), F32), pltpu.VMEM((PEER_CAND_ROWS, LANES), F32)],
        compiler_params=_cparams(("parallel", "parallel")),
        name="peer_query",
    )(x, shift, scale, g, wq, keys)


PEER_EC = 1024
PEER_TT = 1024
PEER_TSUB = 256
PEER_GATE_ROWS = 64
PEER_A_GROUP = 4
BF16_SUBLANES = 16


def _gelu(x):
    return 0.5 * x * (1.0 + lax.erf(x * math.sqrt(0.5)))


def _peer_kernel(xn_ref, u_ref, vt_ref, r2_ref, e2_ref, n1_ref, e1_ref, x_ref, gt_ref, fg_ref,
                 o_ref, acc_ref, act0_ref, act1_ref, pt0_ref, pt1_ref, n1b_ref, e1b_ref, *, final_norm):
    c = pl.program_id(1)
    act_refs, pt_refs = (act0_ref, act1_ref), (pt0_ref, pt1_ref)

    @pl.when(c == 0)
    def _():
        acc_ref[...] = jnp.zeros_like(acc_ref)

    TT = xn_ref.shape[0]
    tsub = min(PEER_TSUB, TT)
    n_sub = TT // tsub
    n_a = PEER_EC // N_KEYS

    groups = [range(g, g + PEER_A_GROUP) for g in range(0, n_a, PEER_A_GROUP)]

    def expert_rows(group):
        return slice(group[0] * N_KEYS, (group[-1] + 1) * N_KEYS)

    def key_rows():
        for tc in range(TT // LANES):
            for h in range(P_HEADS):
                for a in range(n_a):
                    n1b_ref[tc, h, a] = jnp.broadcast_to(n1_ref[tc, h, a:a + 1, :].astype(BF16),
                                                         (BF16_SUBLANES, LANES))
                    e1b_ref[tc, h, a] = jnp.broadcast_to(e1_ref[tc, h, a:a + 1, :].astype(BF16),
                                                         (BF16_SUBLANES, LANES))

    def hidden(j, rows=slice(None)):
        act_refs[j % 2][rows, :] = _gelu(_dot_nt(u_ref[rows, :], xn_ref[j * tsub:(j + 1) * tsub, :])).astype(BF16)

    def gated(j, group):
        act_ref, pt_ref = act_refs[j % 2], pt_refs[j % 2]
        rep = PEER_GATE_ROWS // BF16_SUBLANES
        for tl in range(tsub // LANES):
            tc = j * (tsub // LANES) + tl
            cols = slice(tl * LANES, (tl + 1) * LANES)
            for b0 in range(0, N_KEYS, PEER_GATE_ROWS):
                brows = slice(b0, b0 + PEER_GATE_ROWS)
                gates = {a: None for a in group}
                for h in range(P_HEADS):
                    r2 = r2_ref[tc, h, brows, :]
                    e2 = e2_ref[tc, h, brows, :]
                    for a in group:
                        n1 = jnp.concatenate([n1b_ref[tc, h, a]] * rep, axis=0)
                        e1 = jnp.concatenate([e1b_ref[tc, h, a]] * rep, axis=0)
                        t = e1 * jnp.minimum(jnp.maximum(n1 - r2, 0.0), e2)
                        gates[a] = t if gates[a] is None else gates[a] + t
                for a in group:
                    rows = slice(a * N_KEYS + b0, a * N_KEYS + b0 + PEER_GATE_ROWS)
                    pt_ref[rows, cols] = gates[a] * act_ref[rows, cols]

    def project(j, rows=slice(None)):
        acc_ref[:, j * tsub:(j + 1) * tsub] += _dot(vt_ref[:, rows], pt_refs[j % 2][rows, :])

    for group in groups:
        hidden(0, expert_rows(group))
    key_rows()
    for j in range(n_sub):
        last = j + 1 == n_sub
        if not last:
            hidden(j + 1)
        for group in groups:
            gated(j, group)
            if last:
                project(j, expert_rows(group))
        if not last:
            project(j)

    @pl.when(c == pl.num_programs(1) - 1)
    def _():
        y = x_ref[...] + gt_ref[0] * acc_ref[...].T
        if final_norm:
            y = y * lax.rsqrt(jnp.mean(y * y, axis=-1, keepdims=True) + EPS) * fg_ref[...]
        o_ref[...] = y


def _peer(xn, r2, e2, n1, e1, u, vt, x, gate, final_g, final_norm):
    B, L, D = x.shape
    T = B * L
    TT = min(PEER_TT, L)
    tsub = min(PEER_TSUB, TT)
    per_b = L // TT
    tok_spec = pl.BlockSpec((TT // LANES, P_HEADS, N_KEYS, LANES), lambda i, c: (i, 0, 0, 0))
    key_spec = pl.BlockSpec((TT // LANES, P_HEADS, PEER_EC // N_KEYS, LANES), lambda i, c: (i, 0, c, 0))
    out = pl.pallas_call(
        functools.partial(_peer_kernel, final_norm=final_norm),
        grid=(T // TT, N_EXPERTS // PEER_EC),
        in_specs=[
            pl.BlockSpec((TT, D), lambda i, c: (i, 0)),
            pl.BlockSpec((PEER_EC, D), lambda i, c: (c, 0)),
            pl.BlockSpec((D, PEER_EC), lambda i, c: (0, c)),
            tok_spec, tok_spec, key_spec, key_spec,
            pl.BlockSpec((TT, D), lambda i, c: (i, 0)),
            pl.BlockSpec((1, 1, D), lambda i, c: (i // per_b, 0, 0)),
            pl.BlockSpec((1, D), lambda i, c: (0, 0)),
        ],
        out_specs=pl.BlockSpec((TT, D), lambda i, c: (i, 0)),
        out_shape=jax.ShapeDtypeStruct((T, D), F32),
        scratch_shapes=[pltpu.VMEM((D, TT), F32),
                        pltpu.VMEM((PEER_EC, tsub), BF16), pltpu.VMEM((PEER_EC, tsub), BF16),
                        pltpu.VMEM((PEER_EC, tsub), BF16), pltpu.VMEM((PEER_EC, tsub), BF16),
                        pltpu.VMEM((TT // LANES, P_HEADS, PEER_EC // N_KEYS, BF16_SUBLANES, LANES), BF16),
                        pltpu.VMEM((TT // LANES, P_HEADS, PEER_EC // N_KEYS, BF16_SUBLANES, LANES), BF16)],
        compiler_params=_cparams(("parallel", "arbitrary")),
        name="peer_experts",
    )(xn, u, vt, r2, e2, n1, e1, x.reshape(T, D), gate, final_g)
    return out.reshape(B, L, D)


GDN_CONV_TILE = 512
GDN_PAD = 8


def _gdn_prep_kernel(q_ref, k_ref, v_ref, cwq_ref, cwk_ref, cwv_ref, qo_ref, ko_ref, vo_ref, pad_ref):
    Ls = q_ref.shape[1]
    T = min(GDN_CONV_TILE, Ls)
    zeros = jnp.zeros((GDN_PAD, LANES), F32)
    pad_ref[0:GDN_PAD, :] = zeros
    pad_ref[GDN_PAD + Ls:2 * GDN_PAD + Ls, :] = zeros
    for which, (x_ref, cw_ref, o_ref) in enumerate(((q_ref, cwq_ref, qo_ref), (k_ref, cwk_ref, ko_ref),
                                                    (v_ref, cwv_ref, vo_ref))):
        pad_ref[GDN_PAD:GDN_PAD + Ls, :] = x_ref[0]
        for t0 in range(0, Ls, T):
            acc = None
            for j in range(CONV_K):
                off = GDN_PAD - CONV_K // 2 + j + t0
                t = cw_ref[j:j + 1, :] * pad_ref[off:off + T, :]
                acc = t if acc is None else acc + t
            y = acc * jax.nn.sigmoid(acc)
            if which < 2:
                y = y * lax.rsqrt(jnp.sum(y * y, axis=-1, keepdims=True) + EPS)
            if which == 0:
                y = y * (A_DK ** -0.5)
            o_ref[0, t0:t0 + T, :] = y


def _gdn_prep(proj, conv_w):
    B, Ls, _ = proj.shape
    seq = lambda col: pl.BlockSpec((1, Ls, LANES), lambda b, h: (b, 0, col + h))
    cw = lambda col: pl.BlockSpec((CONV_K, LANES), lambda b, h: (0, col + h))
    out = pl.BlockSpec((1, Ls, LANES), lambda b, h: (b, 0, h))
    shape = jax.ShapeDtypeStruct((B, Ls, A_HEADS * LANES), F32)
    return pl.pallas_call(
        _gdn_prep_kernel,
        grid=(B, A_HEADS),
        in_specs=[seq(COL_A_Q), seq(COL_A_K), seq(COL_A_V), cw(0), cw(A_HEADS), cw(2 * A_HEADS)],
        out_specs=[out] * 3,
        out_shape=[shape] * 3,
        scratch_shapes=[pltpu.VMEM((Ls + 2 * GDN_PAD, LANES), F32)],
        compiler_params=_cparams(("parallel", "parallel")),
        name="gdn_prep",
    )(proj, proj, proj, conv_w, conv_w, conv_w)


def _softplus(x):
    return jnp.maximum(x, 0.0) + jnp.log1p(jnp.exp(-jnp.abs(x)))


GDN_GROUP = 8


def _gdn_prepass_group(h, chunks, nega_ref, dtb_ref, scr):
    mneg_scr, c_scr, qp_scr, dl_scr, o_scr = scr
    C = CHUNK
    row = lax.broadcasted_iota(jnp.int32, (C, C), 0)
    col = lax.broadcasted_iota(jnp.int32, (C, C), 1)
    lane = lax.broadcasted_iota(jnp.int32, (1, LANES), 1)
    eye = row == col
    chains = []
    for q, k, v, ba, row0, chunk_id in chunks:
        kb16 = k.astype(BF16)
        qk_raw = _dot_nt(q.astype(BF16), kb16)
        for d in range(2):
            before_eq_rc = (col <= row) if d == 0 else (col >= row)
            before_rc = (col < row) if d == 0 else (col > row)
            before_eq_cr = (row <= col) if d == 0 else (row >= col)
            bcol = jnp.sum(jnp.where(lane == d * A_HEADS + h, ba, 0.0), axis=-1, keepdims=True)
            acol = jnp.sum(jnp.where(lane == (2 + d) * A_HEADS + h, ba, 0.0), axis=-1, keepdims=True)
            beta = jax.nn.sigmoid(bcol)
            g = nega_ref[d, h] * _softplus(acol + dtb_ref[d, h])
            g_cols = jnp.broadcast_to(g, (C, C))
            gc_row = jnp.sum(jnp.where(before_eq_cr, g_cols, 0.0), axis=0, keepdims=True)
            g_row = jnp.sum(jnp.where(eye, g_cols, 0.0), axis=0, keepdims=True)
            gc_col = jnp.sum(jnp.where(before_eq_rc, jnp.broadcast_to(g_row, (C, C)), 0.0), axis=-1, keepdims=True)
            g_total = jnp.sum(g_row, axis=-1, keepdims=True)
            decay = jnp.exp(jnp.where(before_eq_rc, gc_col - gc_row, NEG_INF))
            kbeta = k * beta
            e_gc = jnp.exp(gc_col)
            chains.append(dict(
                d=d, row0=row0, chunk_id=chunk_id,
                n_pow=-jnp.where(before_rc, _dot_nt(kbeta.astype(BF16), kb16) * decay, 0.0),
                qk=jnp.where(before_eq_rc, qk_raw * decay, 0.0).astype(BF16),
                rhs=jnp.concatenate([v * beta, kbeta * e_gc], axis=-1).astype(BF16),
                kd=(k * jnp.exp(g_total - gc_col)).astype(BF16),
                qd=q * e_gc,
                dl=jnp.exp(g_total)))
    eye_f = jnp.where(eye, 1.0, 0.0)
    n_pows = [ch['n_pow'] for ch in chains]
    invs = [eye_f + n for n in n_pows]
    for _ in range(5):
        n16 = [n.astype(BF16) for n in n_pows]
        n_pows = [_dot(n, n) for n in n16]
        n16 = [n.astype(BF16) for n in n_pows]
        invs = [inv + _dot(inv.astype(BF16), n) for inv, n in zip(invs, n16)]
    uws = [_dot(inv.astype(BF16), ch['rhs']).astype(BF16) for inv, ch in zip(invs, chains)]
    kts = [_dot_tn(ch['kd'], uw) for ch, uw in zip(chains, uws)]
    qqs = [_dot(ch['qk'], uw) for ch, uw in zip(chains, uws)]
    for ch, kt, qq in zip(chains, kts, qqs):
        d, cid = ch['d'], ch['chunk_id']
        rows = pl.ds(ch['row0'], C)
        c_scr[d, cid] = kt[:, :A_DV]
        mneg_scr[d, cid] = (-kt[:, A_DV:]).astype(BF16)
        o_scr[d, rows, :] = qq[:, :A_DV]
        qp_scr[d, rows, :] = (ch['qd'] - qq[:, A_DV:]).astype(BF16)
        dl_scr[d, pl.ds(cid * 8, 8), :] = jnp.broadcast_to(ch['dl'], (8, LANES))


def _gdn_scan_step(states, scr, row0s, chunk_ids):
    mneg_scr, c_scr, qp_scr, dl_scr, o_scr = scr
    s16 = [S.astype(BF16) for S in states]
    upd = [_dot(mneg_scr[d, chunk_ids[d]], s16[d]) for d in range(2)]
    for d in range(2):
        rows = pl.ds(row0s[d], CHUNK)
        o_scr[d, rows, :] += _dot(qp_scr[d, rows, :], s16[d])
    return tuple(states[d] * dl_scr[d, pl.ds(chunk_ids[d] * 8, 1), :] + c_scr[d, chunk_ids[d]] + upd[d]
                 for d in range(2))


def _gdn_scan_kernel(nega_ref, dtb_ref, qc_ref, kc_ref, vc_ref, bac_ref, zc_ref, ql_ref, kl_ref, vl_ref, bal_ref,
                     zl_ref, ng_ref, yl_ref, yc_ref, mneg_scr, c_scr, qp_scr, dl_scr, o_scr):
    h = pl.program_id(1)
    Lc, L = qc_ref.shape[1], ql_ref.shape[1]
    nc, nl = Lc // CHUNK, L // CHUNK
    scr = (mneg_scr, c_scr, qp_scr, dl_scr, o_scr)

    for c0 in range(0, nc, GDN_GROUP):
        _gdn_prepass_group(h, [(qc_ref[0, c * CHUNK:(c + 1) * CHUNK, :], kc_ref[0, c * CHUNK:(c + 1) * CHUNK, :],
                                vc_ref[0, c * CHUNK:(c + 1) * CHUNK, :], bac_ref[0, c * CHUNK:(c + 1) * CHUNK, :],
                                c * CHUNK, c) for c in range(c0, min(c0 + GDN_GROUP, nc))],
                           nega_ref, dtb_ref, scr)

    def pre_body(grp, carry):
        chunks = []
        for j in range(GDN_GROUP):
            c = grp * GDN_GROUP + j
            r = pl.multiple_of(c * CHUNK, CHUNK)
            rows = pl.ds(r, CHUNK)
            chunks.append((ql_ref[0, rows, :], kl_ref[0, rows, :], vl_ref[0, rows, :], bal_ref[0, rows, :],
                           Lc + r, nc + c))
        _gdn_prepass_group(h, chunks, nega_ref, dtb_ref, scr)
        return carry

    lax.fori_loop(0, nl // GDN_GROUP, pre_body, 0)

    states = (jnp.zeros((A_DK, A_DV), F32), jnp.zeros((A_DK, A_DV), F32))
    for s in range(nc):
        states = _gdn_scan_step(states, scr, (s * CHUNK, (nc - 1 - s) * CHUNK), (s, nc - 1 - s))

    def scan_body(s, states):
        cf = s
        cb = nl - 1 - s
        return _gdn_scan_step(states, scr,
                              (pl.multiple_of(Lc + cf * CHUNK, CHUNK), pl.multiple_of(Lc + cb * CHUNK, CHUNK)),
                              (nc + cf, nc + cb))

    lax.fori_loop(0, nl, scan_body, states)

    def finish(z_ref, y_ref, base, n):
        T = min(GDN_CONV_TILE, n)
        for t0 in range(0, n, T):
            o = o_scr[0, base + t0:base + t0 + T, :] + o_scr[1, base + t0:base + t0 + T, :]
            y = o * lax.rsqrt(jnp.mean(o * o, axis=-1, keepdims=True) + EPS) * ng_ref[...]
            z = z_ref[0, t0:t0 + T, :]
            y_ref[0, t0:t0 + T, :] = y * (z * jax.nn.sigmoid(z))

    finish(zl_ref, yl_ref, Lc, L)
    finish(zc_ref, yc_ref, 0, Lc)


def _gdn_scan(nega, dtb, qkv_ctx, ba_ctx, proj_ctx, qkv_lat, ba_lat, proj_lat, norm_g):
    B, L, _ = proj_lat.shape
    Lc = proj_ctx.shape[1]
    Lt = L + Lc
    head = lambda n: pl.BlockSpec((1, n, LANES), lambda b, h: (b, 0, h))
    full = lambda n: pl.BlockSpec((1, n, LANES), lambda b, h: (b, 0, 0))
    zcol = lambda n: pl.BlockSpec((1, n, LANES), lambda b, h: (b, 0, COL_A_Z + h))
    smem = pl.BlockSpec(memory_space=pltpu.SMEM)
    return pl.pallas_call(
        _gdn_scan_kernel,
        grid=(B, A_HEADS),
        in_specs=[smem, smem,
                  head(Lc), head(Lc), head(Lc), full(Lc), zcol(Lc),
                  head(L), head(L), head(L), full(L), zcol(L),
                  pl.BlockSpec((1, LANES), lambda b, h: (0, 0))],
        out_specs=[head(L), head(Lc)],
        out_shape=[jax.ShapeDtypeStruct((B, L, A_HEADS * LANES), F32),
                   jax.ShapeDtypeStruct((B, Lc, A_HEADS * LANES), F32)],
        scratch_shapes=[
            pltpu.VMEM((2, Lt // CHUNK, A_DK, A_DV), BF16),
            pltpu.VMEM((2, Lt // CHUNK, A_DK, A_DV), F32),
            pltpu.VMEM((2, Lt, A_DK), BF16),
            pltpu.VMEM((2, Lt // CHUNK * 8, LANES), F32),
            pltpu.VMEM((2, Lt, A_DV), F32),
        ],
        compiler_params=_cparams(("parallel", "parallel")),
        name="gdn_scan",
    )(nega, dtb, *qkv_ctx, ba_ctx, proj_ctx, *qkv_lat, ba_lat, proj_lat, norm_g)


def _gdn_pallas(proj_lat, ba_lat, proj_ctx, ba_ctx, conv_w, a_log, dt_bias, norm_g):
    qkv_lat = _gdn_prep(proj_lat, conv_w)
    qkv_ctx = _gdn_prep(proj_ctx, conv_w)
    return _gdn_scan(-jnp.exp(a_log), dt_bias, qkv_ctx, ba_ctx, proj_ctx, qkv_lat, ba_lat, proj_lat, norm_g[None, :])


def _layer(i, x, xc, c, c_ctx, p, cos_t, sin_t, ctx_out, final_g, final_norm):
    B, L, D = x.shape
    Lc = xc.shape[1]
    mod_lat = (jax.nn.silu(c) @ p['ada_w'] + p['ada_b'])[:, None, :]
    mod_ctx = jnp.broadcast_to((jax.nn.silu(c_ctx) @ p['ada_w'] + p['ada_b'])[None, None, :], (B, 1, 6 * D))
    sh1, sc1, gt1, sh2, sc2, gt2 = jnp.split(mod_lat, 6, axis=-1)
    csh1, csc1, cgt1, csh2, csc2, cgt2 = jnp.split(mod_ctx, 6, axis=-1)
    lam_init = 0.8 - 0.6 * math.exp(-0.3 * i)

    w_in = p['w_in']
    n_a = 4 * A_HEADS * A_DK
    w_main = jnp.concatenate([w_in[:, :n_a], w_in[:, n_a + 4 * A_HEADS:]], axis=1).astype(BF16)
    w_ba = jnp.pad(w_in[:, n_a:n_a + 4 * A_HEADS], ((0, 0), (0, LANES - 4 * A_HEADS))).astype(BF16)
    n1 = p['norm1_g'][None, :]
    proj_lat, ba_lat = _inproj(x, sh1, sc1, n1, w_main, w_ba, 1024)
    proj_ctx, ba_ctx = _inproj(xc, csh1, csc1, n1, w_main, w_ba, Lc)

    ya, ya_c = _gdn_pallas(proj_lat, ba_lat, proj_ctx, ba_ctx, p['gdn_conv'], p['gdn_a_log'], p['gdn_dt_bias'],
                           p['gdn_norm_g'])

    lp = p['diff_lambda']
    lam = (jnp.exp(jnp.sum(lp[0] * lp[1])) - jnp.exp(jnp.sum(lp[2] * lp[3])) + lam_init).reshape(1)
    q_r, k_r = _rope(proj_lat, cos_t, sin_t, 512)
    subln = p['diff_subln_g'][None, :]
    yb = _diff_lat(lam, q_r, k_r, proj_lat, proj_ctx, subln, 1.0 - lam_init, 512)
    yb_c = _diff_ctx(lam, proj_ctx, subln, 1.0 - lam_init) if ctx_out else None

    yc = _natten_lat(proj_lat, proj_ctx, _natten_bias(p['na_rpb']))
    yc_c = _natten_ctx(proj_ctx) if ctx_out else None

    w_up = p['w_up'].astype(BF16)
    w_out = p['w_out'].astype(BF16)
    wq = p['peer_wq'].astype(BF16)
    keys = p['peer_keys'].reshape(2 * P_HEADS, N_KEYS, P_DKH).astype(BF16)
    u = p['peer_u'].astype(BF16)
    vt = p['peer_v'].T.astype(BF16)
    n2 = p['norm2_g'][None, :]
    fg = final_g[None, :]

    x = _merge(ya, yb, yc, proj_lat, w_up, w_out, x, gt1, 512)
    pq = _peerq(x, sh2, sc2, n2, wq, keys, 256)
    x = _peer(*pq, u, vt, x, gt2, fg, final_norm)
    if ctx_out:
        xc = _merge(ya_c, yb_c, yc_c, proj_ctx, w_up, w_out, xc, cgt1, Lc)
        pq = _peerq(xc, csh2, csc2, n2, wq, keys, Lc)
        xc = _peer(*pq, u, vt, xc, cgt2, fg, False)
    return x, xc


def kernel(x, c, ctx, c_ctx, norm1_g, norm2_g, ada_w, ada_b, w_in, gdn_conv, gdn_a_log, gdn_dt_bias, gdn_norm_g,
           diff_lambda, diff_subln_g, na_rpb, w_up, w_out, peer_wq, peer_keys, peer_u, peer_v, final_g):
    cos_t, sin_t = _rope_tables(x.shape[1])
    xc = ctx
    for i in range(DEPTH):
        p = dict(norm1_g=norm1_g[i], norm2_g=norm2_g[i], ada_w=ada_w[i], ada_b=ada_b[i], w_in=w_in[i],
                 gdn_conv=gdn_conv[i], gdn_a_log=gdn_a_log[i], gdn_dt_bias=gdn_dt_bias[i], gdn_norm_g=gdn_norm_g[i],
                 diff_lambda=diff_lambda[i], diff_subln_g=diff_subln_g[i], na_rpb=na_rpb[i], w_up=w_up[i],
                 w_out=w_out[i], peer_wq=peer_wq[i], peer_keys=peer_keys[i], peer_u=peer_u[i], peer_v=peer_v[i])
        x, xc = _layer(i, x, xc, c, c_ctx, p, cos_t, sin_t, i < DEPTH - 1, final_g, i == DEPTH - 1)
    return x
```

```python
import functools
import math

import numpy as np
import jax
import jax.numpy as jnp
from jax import lax
from jax.experimental import pallas as pl
from jax.experimental.pallas import tpu as pltpu

F32 = jnp.float32
BF16 = jnp.bfloat16

D_MODEL = 1024
DEPTH = 2
GRID_W = 64
EPS = 1e-6
NEG_INF = -1e30

A_HEADS = 4
A_DK = 128
A_DV = 128
CONV_K = 5
CHUNK = 64
B_HEADS = 4
B_DH = 64
ROPE_BASE = 10000.0
C_HEADS = 8
C_DH = 64
WIN_R = 8
WIN_C = 16
N_BRANCH = 3
BRANCH_W = 512
P_HEADS = 8
N_KEYS = 128
N_EXPERTS = N_KEYS * N_KEYS
P_DKH = 128
P_TOPK = 16

LANES = 128
VMEM_LIMIT = 56 * 1024 * 1024

COL_A_Q, COL_A_K, COL_A_V, COL_A_Z = 0, 4, 8, 12
COL_B_Q, COL_B_K, COL_B_V = 16, 20, 24
COL_C_Q, COL_C_K, COL_C_V = 28, 32, 36
COL_GATE = 40
MAIN_COLS = 64 * LANES


def _cparams(sem):
    return pltpu.CompilerParams(dimension_semantics=sem, vmem_limit_bytes=VMEM_LIMIT)


def _dot(a, b):
    return jnp.dot(a, b, preferred_element_type=F32)


def _dot_nt(a, b):
    return lax.dot_general(a, b, (((1,), (1,)), ((), ())), preferred_element_type=F32)


def _dot_tn(a, b):
    return lax.dot_general(a, b, (((0,), (0,)), ((), ())), preferred_element_type=F32)


def _inproj_kernel(x_ref, sh_ref, sc_ref, g_ref, w_ref, wba_ref, o_ref, oba_ref, hn_ref):
    @pl.when(pl.program_id(2) == 0)
    def _():
        x = x_ref[0]
        y = x * lax.rsqrt(jnp.mean(x * x, axis=-1, keepdims=True) + EPS) * g_ref[...]
        h = (y * (1.0 + sc_ref[0]) + sh_ref[0]).astype(BF16)
        hn_ref[...] = h
        oba_ref[0] = _dot(h, wba_ref[...])

    o_ref[0] = _dot(hn_ref[...], w_ref[...])


def _inproj(x, shift, scale, g, w_main, w_ba, tm):
    B, L, D = x.shape
    tn = 1024
    return pl.pallas_call(
        _inproj_kernel,
        grid=(B, L // tm, MAIN_COLS // tn),
        in_specs=[
            pl.BlockSpec((1, tm, D), lambda b, i, j: (b, i, 0)),
            pl.BlockSpec((1, 1, D), lambda b, i, j: (b, 0, 0)),
            pl.BlockSpec((1, 1, D), lambda b, i, j: (b, 0, 0)),
            pl.BlockSpec((1, D), lambda b, i, j: (0, 0)),
            pl.BlockSpec((D, tn), lambda b, i, j: (0, j)),
            pl.BlockSpec((D, LANES), lambda b, i, j: (0, 0)),
        ],
        out_specs=[
            pl.BlockSpec((1, tm, tn), lambda b, i, j: (b, i, j)),
            pl.BlockSpec((1, tm, LANES), lambda b, i, j: (b, i, 0)),
        ],
        out_shape=[
            jax.ShapeDtypeStruct((B, L, MAIN_COLS), F32),
            jax.ShapeDtypeStruct((B, L, LANES), F32),
        ],
        scratch_shapes=[pltpu.VMEM((tm, D), BF16)],
        compiler_params=_cparams(("parallel", "parallel", "arbitrary")),
        name="inproj",
    )(x, shift, scale, g, w_main, w_ba)


def _rope_tables(L):
    t = np.arange(L)
    row_pos, col_pos = t // GRID_W, t % GRID_W
    lane = np.arange(LANES)
    axis = (lane % 64) // 32
    f = lane % 16
    inv = 1.0 / (ROPE_BASE ** (f.astype(np.float32) / 16.0))
    pos = np.where(axis[None, :] == 0, row_pos[:, None], col_pos[:, None]).astype(np.float32)
    ang = jnp.asarray(pos) * jnp.asarray(inv.astype(np.float32))[None, :]
    first = jnp.asarray(((lane % 32) < 16)[None, :])
    return jnp.cos(ang), jnp.where(first, -jnp.sin(ang), jnp.sin(ang))


def _rope_kernel(q_ref, k_ref, cos_ref, sin_ref, qo_ref, ko_ref):
    lane = lax.broadcasted_iota(jnp.int32, (1, LANES), 1)
    first = (lane % 32) < 16
    c, s = cos_ref[...], sin_ref[...]

    def rope(x):
        partner = jnp.where(first, pltpu.roll(x, LANES - 16, 1), pltpu.roll(x, 16, 1))
        return x * c + partner * s

    for h in range(B_HEADS):
        sl = slice(h * LANES, (h + 1) * LANES)
        qo_ref[0, :, sl] = (rope(q_ref[0, :, sl]) * (B_DH ** -0.5)).astype(BF16)
        ko_ref[0, :, sl] = rope(k_ref[0, :, sl]).astype(BF16)


def _rope(proj, cos_t, sin_t, tr):
    B, L, _ = proj.shape
    W = B_HEADS * LANES
    return pl.pallas_call(
        _rope_kernel,
        grid=(B, L // tr),
        in_specs=[
            pl.BlockSpec((1, tr, W), lambda b, i: (b, i, COL_B_Q * LANES // W)),
            pl.BlockSpec((1, tr, W), lambda b, i: (b, i, COL_B_K * LANES // W)),
            pl.BlockSpec((tr, LANES), lambda b, i: (i, 0)),
            pl.BlockSpec((tr, LANES), lambda b, i: (i, 0)),
        ],
        out_specs=[pl.BlockSpec((1, tr, W), lambda b, i: (b, i, 0))] * 2,
        out_shape=[jax.ShapeDtypeStruct((B, L, W), BF16)] * 2,
        compiler_params=_cparams(("parallel", "parallel")),
        name="rope",
    )(proj, proj, cos_t, sin_t)


def _softmax_pv(q, key_vals):
    scores = [_dot_nt(q, k) for k, _ in key_vals]
    m = functools.reduce(jnp.maximum, [s.max(axis=-1, keepdims=True) for s in scores])
    es = [jnp.exp(s - m) for s in scores]
    denom = functools.reduce(jnp.add, [e.sum(axis=-1, keepdims=True) for e in es])
    o = functools.reduce(jnp.add, [_dot(e.astype(BF16), v) for e, (_, v) in zip(es, key_vals)])
    return o / denom


DIFF_Q_ROWS = 256


def _diff_finish(o0, o1, lam, g, out_scale):
    o = o0 - lam * o1
    return o * lax.rsqrt(jnp.mean(o * o, axis=-1, keepdims=True) + EPS) * g * out_scale


def _diff_lat_kernel(lam_ref, q_ref, kl_ref, vl_ref, kc_ref, vc_ref, g_ref, o_ref, *, out_scale):
    lane = lax.broadcasted_iota(jnp.int32, (1, LANES), 1)
    low = lane < B_DH
    tq = q_ref.shape[1]
    kl, vl = kl_ref[0], vl_ref[0].astype(BF16)
    kc, vc = kc_ref[0].astype(BF16), vc_ref[0].astype(BF16)
    units = [dict(r0=r0, mp=mp) for r0 in range(0, tq, DIFF_Q_ROWS) for mp in range(2)]

    def scores(un):
        q = q_ref[0, un['r0']:un['r0'] + DIFF_Q_ROWS, :]
        q = (jnp.where(low, q, 0) if un['mp'] == 0 else jnp.where(low, 0, q)).astype(BF16)
        un['s'] = (_dot_nt(q, kl), _dot_nt(q, kc))

    def softmax(un):
        s_l, s_c = un.pop('s')
        m = jnp.maximum(s_l.max(axis=-1, keepdims=True), s_c.max(axis=-1, keepdims=True))
        e_l, e_c = jnp.exp(s_l - m), jnp.exp(s_c - m)
        un['denom'] = e_l.sum(axis=-1, keepdims=True) + e_c.sum(axis=-1, keepdims=True)
        un['e'] = (e_l.astype(BF16), e_c.astype(BF16))

    def values(un):
        e_l, e_c = un.pop('e')
        un['o'] = (_dot(e_l, vl) + _dot(e_c, vc)) / un['denom']

    scores(units[0])
    scores(units[1])
    for k, un in enumerate(units):
        softmax(un)
        if k + 2 < len(units):
            scores(units[k + 2])
        values(un)
    for k in range(0, len(units), 2):
        r0 = units[k]['r0']
        o_ref[0, r0:r0 + DIFF_Q_ROWS, :] = _diff_finish(units[k]['o'], units[k + 1]['o'], lam_ref[0], g_ref[...],
                                                        out_scale)


def _diff_lat(lam, q_r, k_r, proj_lat, proj_ctx, subln_g, out_scale, tq):
    B, L, _ = proj_lat.shape
    Lc = proj_ctx.shape[1]
    return pl.pallas_call(
        functools.partial(_diff_lat_kernel, out_scale=out_scale),
        grid=(B, B_HEADS, L // tq),
        in_specs=[
            pl.BlockSpec(memory_space=pltpu.SMEM),
            pl.BlockSpec((1, tq, LANES), lambda b, h, i: (b, i, h)),
            pl.BlockSpec((1, L, LANES), lambda b, h, i: (b, 0, h)),
            pl.BlockSpec((1, L, LANES), lambda b, h, i: (b, 0, COL_B_V + h)),
            pl.BlockSpec((1, Lc, LANES), lambda b, h, i: (b, 0, COL_B_K + h)),
            pl.BlockSpec((1, Lc, LANES), lambda b, h, i: (b, 0, COL_B_V + h)),
            pl.BlockSpec((1, LANES), lambda b, h, i: (0, 0)),
        ],
        out_specs=pl.BlockSpec((1, tq, LANES), lambda b, h, i: (b, i, h)),
        out_shape=jax.ShapeDtypeStruct((B, L, B_HEADS * LANES), F32),
        compiler_params=_cparams(("parallel", "parallel", "parallel")),
        name="diff_lat",
    )(lam, q_r, k_r, proj_lat, proj_ctx, proj_ctx, subln_g)


def _diff_ctx_kernel(lam_ref, q_ref, k_ref, v_ref, g_ref, o_ref, *, out_scale):
    lane = lax.broadcasted_iota(jnp.int32, (1, LANES), 1)
    low = lane < B_DH
    q = q_ref[0] * (B_DH ** -0.5)
    kv = [(k_ref[0].astype(BF16), v_ref[0].astype(BF16))]
    o0 = _softmax_pv(jnp.where(low, q, 0).astype(BF16), kv)
    o1 = _softmax_pv(jnp.where(low, 0, q).astype(BF16), kv)
    o_ref[0] = _diff_finish(o0, o1, lam_ref[0], g_ref[...], out_scale)


def _diff_ctx(lam, proj_ctx, subln_g, out_scale):
    B, Lc, _ = proj_ctx.shape
    return pl.pallas_call(
        functools.partial(_diff_ctx_kernel, out_scale=out_scale),
        grid=(B, B_HEADS),
        in_specs=[
            pl.BlockSpec(memory_space=pltpu.SMEM),
            pl.BlockSpec((1, Lc, LANES), lambda b, h: (b, 0, COL_B_Q + h)),
            pl.BlockSpec((1, Lc, LANES), lambda b, h: (b, 0, COL_B_K + h)),
            pl.BlockSpec((1, Lc, LANES), lambda b, h: (b, 0, COL_B_V + h)),
            pl.BlockSpec((1, LANES), lambda b, h: (0, 0)),
        ],
        out_specs=pl.BlockSpec((1, Lc, LANES), lambda b, h: (b, 0, h)),
        out_shape=jax.ShapeDtypeStruct((B, Lc, B_HEADS * LANES), F32),
        compiler_params=_cparams(("parallel", "parallel")),
        name="diff_ctx",
    )(lam, proj_ctx, proj_ctx, proj_ctx, subln_g)


NA_ROWS_PER_STEP = 8
NA_KEYS = WIN_R * GRID_W


def _natten_bias(rpb):
    col = np.arange(GRID_W)
    col_start = np.clip(col - WIN_C // 2, 0, GRID_W - WIN_C)
    col_mask = (col[None, :] >= col_start[:, None]) & (col[None, :] < col_start[:, None] + WIN_C)
    dc = np.clip(col[None, :] - col[:, None], -(WIN_C - 1), WIN_C - 1) + WIN_C - 1
    bias = jnp.where(jnp.asarray(col_mask), rpb.astype(F32)[:, :, dc], NEG_INF)
    return jnp.concatenate([bias[:, :-1], bias[:, 1:]], axis=-1)


def _natten_kernel(q_ref, k_ref, v_ref, kc_ref, vc_ref, bias_ref, o_ref, *, rows):
    lane = lax.broadcasted_iota(jnp.int32, (1, LANES), 1)
    low = lane < C_DH
    kc = kc_ref[0].astype(BF16)
    vc = vc_ref[0].astype(BF16)
    units = []
    for rr in range(NA_ROWS_PER_STEP):
        r = pl.program_id(2) * NA_ROWS_PER_STEP + rr
        rs = jnp.clip(r - WIN_R // 2, 0, rows - WIN_R)
        cfg = r - rs
        start = pl.multiple_of(rs * GRID_W, GRID_W)
        q = q_ref[0, rr * GRID_W:(rr + 1) * GRID_W, :] * (C_DH ** -0.5)
        q2 = jnp.concatenate([jnp.where(low, q, 0), jnp.where(low, 0, q)], axis=0).astype(BF16)
        bias = jnp.concatenate(
            [jnp.concatenate([bias_ref[hh, WIN_R - 1 - cfg + j] for j in range(0, WIN_R, 2)], axis=-1)
             for hh in range(2)], axis=0)
        units.append(dict(q=q2, bias=bias, start=start))
    for un in units:
        kw = k_ref[0, pl.ds(un['start'], NA_KEYS), :].astype(BF16)
        un['s_lat'] = _dot_nt(un['q'], kw) + un['bias']
        un['s_ctx'] = _dot_nt(un['q'], kc)
    for un in units:
        s_lat, s_ctx = un['s_lat'], un['s_ctx']
        m = jnp.maximum(s_lat.max(axis=-1, keepdims=True), s_ctx.max(axis=-1, keepdims=True))
        e_lat, e_ctx = jnp.exp(s_lat - m), jnp.exp(s_ctx - m)
        un['denom'] = e_lat.sum(axis=-1, keepdims=True) + e_ctx.sum(axis=-1, keepdims=True)
        un['e_lat'], un['e_ctx'] = e_lat.astype(BF16), e_ctx.astype(BF16)
    for rr, un in enumerate(units):
        vw = v_ref[0, pl.ds(un['start'], NA_KEYS), :].astype(BF16)
        o = (_dot(un['e_lat'], vw) + _dot(un['e_ctx'], vc)) / un['denom']
        o_ref[0, rr * GRID_W:(rr + 1) * GRID_W, :] = jnp.where(low, o[:GRID_W], o[GRID_W:])


def _natten_lat(proj_lat, proj_ctx, bias):
    B, L, _ = proj_lat.shape
    Lc = proj_ctx.shape[1]
    rows = L // GRID_W
    HP = C_HEADS // 2
    tq = NA_ROWS_PER_STEP * GRID_W
    return pl.pallas_call(
        functools.partial(_natten_kernel, rows=rows),
        grid=(B, HP, rows // NA_ROWS_PER_STEP),
        in_specs=[
            pl.BlockSpec((1, tq, LANES), lambda b, h, i: (b, i, COL_C_Q + h)),
            pl.BlockSpec((1, L, LANES), lambda b, h, i: (b, 0, COL_C_K + h)),
            pl.BlockSpec((1, L, LANES), lambda b, h, i: (b, 0, COL_C_V + h)),
            pl.BlockSpec((1, Lc, LANES), lambda b, h, i: (b, 0, COL_C_K + h)),
            pl.BlockSpec((1, Lc, LANES), lambda b, h, i: (b, 0, COL_C_V + h)),
            pl.BlockSpec((2, 2 * WIN_R - 2, GRID_W, 2 * GRID_W), lambda b, h, i: (h, 0, 0, 0)),
        ],
        out_specs=pl.BlockSpec((1, tq, LANES), lambda b, h, i: (b, i, h)),
        out_shape=jax.ShapeDtypeStruct((B, L, HP * LANES), F32),
        compiler_params=_cparams(("parallel", "parallel", "parallel")),
        name="natten_lat",
    )(proj_lat, proj_lat, proj_lat, proj_ctx, proj_ctx, bias)


def _natten_ctx_kernel(q_ref, k_ref, v_ref, o_ref):
    lane = lax.broadcasted_iota(jnp.int32, (1, LANES), 1)
    low = lane < C_DH
    q = q_ref[0] * (C_DH ** -0.5)
    kv = [(k_ref[0].astype(BF16), v_ref[0].astype(BF16))]
    o0 = _softmax_pv(jnp.where(low, q, 0).astype(BF16), kv)
    o1 = _softmax_pv(jnp.where(low, 0, q).astype(BF16), kv)
    o_ref[0] = jnp.where(low, o0, o1)


def _natten_ctx(proj_ctx):
    B, Lc, _ = proj_ctx.shape
    HP = C_HEADS // 2
    return pl.pallas_call(
        _natten_ctx_kernel,
        grid=(B, HP),
        in_specs=[
            pl.BlockSpec((1, Lc, LANES), lambda b, h: (b, 0, COL_C_Q + h)),
            pl.BlockSpec((1, Lc, LANES), lambda b, h: (b, 0, COL_C_K + h)),
            pl.BlockSpec((1, Lc, LANES), lambda b, h: (b, 0, COL_C_V + h)),
        ],
        out_specs=pl.BlockSpec((1, Lc, LANES), lambda b, h: (b, 0, h)),
        out_shape=jax.ShapeDtypeStruct((B, Lc, HP * LANES), F32),
        compiler_params=_cparams(("parallel", "parallel")),
        name="natten_ctx",
    )(proj_ctx, proj_ctx, proj_ctx)


def _merge_kernel(ya_ref, yb_ref, yc_ref, g0_ref, g1_ref, g2_ref, wup_ref, wout_ref, x_ref, gt_ref, o_ref):
    acc = None
    for n, (y_ref, g_ref) in enumerate(((ya_ref, g0_ref), (yb_ref, g1_ref), (yc_ref, g2_ref))):
        up = _dot(y_ref[0].astype(BF16), wup_ref[n])
        t = jax.nn.sigmoid(g_ref[0]) * up
        acc = t if acc is None else acc + t
    r = _dot(acc.astype(BF16), wout_ref[...])
    o_ref[0] = x_ref[0] + gt_ref[0] * r


def _merge(ya, yb, yc, proj, w_up, w_out, x, gate, tm):
    B, L, D = x.shape
    gcol = COL_GATE * LANES // D
    yspec = pl.BlockSpec((1, tm, BRANCH_W), lambda b, i: (b, i, 0))
    return pl.pallas_call(
        _merge_kernel,
        grid=(B, L // tm),
        in_specs=[
            yspec, yspec, yspec,
            pl.BlockSpec((1, tm, D), lambda b, i: (b, i, gcol)),
            pl.BlockSpec((1, tm, D), lambda b, i: (b, i, gcol + 1)),
            pl.BlockSpec((1, tm, D), lambda b, i: (b, i, gcol + 2)),
            pl.BlockSpec((N_BRANCH, BRANCH_W, D), lambda b, i: (0, 0, 0)),
            pl.BlockSpec((D, D), lambda b, i: (0, 0)),
            pl.BlockSpec((1, tm, D), lambda b, i: (b, i, 0)),
            pl.BlockSpec((1, 1, D), lambda b, i: (b, 0, 0)),
        ],
        out_specs=pl.BlockSpec((1, tm, D), lambda b, i: (b, i, 0)),
        out_shape=jax.ShapeDtypeStruct((B, L, D), F32),
        compiler_params=_cparams(("parallel", "parallel")),
        name="merge",
    )(ya, yb, yc, proj, proj, proj, w_up, w_out, x, gate)


PEER_CAND = P_TOPK + 1
PEER_A_PAD = 24
PEER_CAND_ROWS = PEER_A_PAD + 7 * 8 + 16


PEER_NO_RANK = float(N_KEYS)


def _extract_top(s, n, with_rank=False):
    vals = []
    rank = jnp.full(s.shape, PEER_NO_RANK, F32) if with_rank else None
    for r in range(n):
        m = jnp.max(s, axis=0, keepdims=True)
        vals.append(m)
        hit = s == m
        if with_rank:
            rank = jnp.where(hit, float(r), rank)
        s = jnp.where(hit, NEG_INF, s)
    return (vals, rank) if with_rank else vals


def _peerq_kernel(x_ref, sh_ref, sc_ref, g_ref, wq_ref, keys_ref,
                  xn_ref, r2_ref, e2_ref, n1_ref, e1_ref, ab_ref, cand_ref):
    x = x_ref[0]
    tm = x.shape[0]
    y = x * lax.rsqrt(jnp.mean(x * x, axis=-1, keepdims=True) + EPS) * g_ref[...]
    xn = (y * (1.0 + sc_ref[0]) + sh_ref[0]).astype(BF16)
    xn_ref[...] = xn
    q = _dot(xn, wq_ref[...]).astype(BF16)
    row8 = lax.broadcasted_iota(jnp.int32, (8, 1), 0)
    row16 = lax.broadcasted_iota(jnp.int32, (16, 1), 0)
    for h in range(P_HEADS):
        for tc in range(tm // LANES):
            qt = q[tc * LANES:(tc + 1) * LANES, :]
            s = []
            for p in range(2):
                hp = 2 * h + p
                st = _dot_nt(keys_ref[hp], qt[:, hp * P_DKH:(hp + 1) * P_DKH])
                s.append(st)
                ab_ref[p] = jnp.full((PEER_A_PAD, LANES), NEG_INF, F32)
                if p == 0:
                    vals = _extract_top(st, PEER_CAND)
                else:
                    vals, rank2 = _extract_top(st, PEER_CAND, with_rank=True)
                for r, m in enumerate(vals):
                    ab_ref[p, r:r + 1, :] = m
            a_all, b_all = ab_ref[0], ab_ref[1]
            cand_ref[0:PEER_A_PAD, :] = a_all[0:1, :] + b_all
            for i in range(1, 8):
                n_i = PEER_CAND // (i + 1)
                cand_ref[PEER_A_PAD + 8 * (i - 1):PEER_A_PAD + 8 * i, :] = jnp.where(
                    row8 < n_i, a_all[i:i + 1, :] + b_all[0:8, :], NEG_INF)
            cand_ref[PEER_A_PAD + 56:PEER_A_PAD + 72, :] = jnp.where(
                row16 + 8 < PEER_CAND, a_all[8:24, :] + b_all[0:1, :], NEG_INF)
            top = _extract_top(cand_ref[...], PEER_CAND)
            z = functools.reduce(jnp.add, [jnp.exp(v - top[0]) for v in top[:P_TOPK]])
            thr = 0.5 * (top[P_TOPK - 1] + top[P_TOPK])
            n1 = jnp.zeros((N_KEYS, LANES), F32)
            for i in range(P_TOPK):
                a_i = a_all[i:i + 1, :]
                cnt = jnp.sum(jnp.where(a_i + b_all >= thr, 1.0, 0.0), axis=0, keepdims=True)
                n1 = jnp.where(s[0] == a_i, cnt, n1)
            r2_ref[tc, h] = rank2.astype(BF16)
            e2_ref[tc, h] = (jnp.exp(s[1] - b_all[0:1, :]) / z).astype(BF16)
            n1_ref[tc, h] = n1
            e1_ref[tc, h] = jnp.exp(s[0] - a_all[0:1, :])


def _peerq(x, shift, scale, g, wq, keys, tm):
    B, L, D = x.shape
    T = B * L
    nb = L // tm
    tok_spec = pl.BlockSpec((tm // LANES, P_HEADS, N_KEYS, LANES), lambda b, i: (b * nb + i, 0, 0, 0))
    tok_shape = lambda dt: jax.ShapeDtypeStruct((T // LANES, P_HEADS, N_KEYS, LANES), dt)
    return pl.pallas_call(
        _peerq_kernel,
        grid=(B, nb),
        in_specs=[
            pl.BlockSpec((1, tm, D), lambda b, i: (b, i, 0)),
            pl.BlockSpec((1, 1, D), lambda b, i: (b, 0, 0)),
            pl.BlockSpec((1, 1, D), lambda b, i: (b, 0, 0)),
            pl.BlockSpec((1, D), lambda b, i: (0, 0)),
            pl.BlockSpec((D, 2 * P_HEADS * P_DKH), lambda b, i: (0, 0)),
            pl.BlockSpec((2 * P_HEADS, N_KEYS, P_DKH), lambda b, i: (0, 0, 0)),
        ],
        out_specs=[pl.BlockSpec((tm, D), lambda b, i: (b * nb + i, 0))] + [tok_spec] * 4,
        out_shape=[jax.ShapeDtypeStruct((T, D), BF16), tok_shape(BF16), tok_shape(BF16), tok_shape(F32),
                   tok_shape(F32)],
        scratch_shapes=[pltpu.VMEM((2, PEER_A_PAD, LANES), F32), pltpu.VMEM((PEER_CAND_ROWS, LANES), F32)],
        compiler_params=_cparams(("parallel", "parallel")),
        name="peer_query",
    )(x, shift, scale, g, wq, keys)


PEER_EC = 1024
PEER_TT = 1024
PEER_TSUB = 256
PEER_GATE_ROWS = 64


def _gelu(x):
    return 0.5 * x * (1.0 + lax.erf(x * math.sqrt(0.5)))


def _peer_kernel(xn_ref, u_ref, vt_ref, r2_ref, e2_ref, n1_ref, e1_ref, x_ref, gt_ref, fg_ref,
                 o_ref, acc_ref, act0_ref, act1_ref, pt0_ref, pt1_ref, *, final_norm):
    c = pl.program_id(1)
    act_refs, pt_refs = (act0_ref, act1_ref), (pt0_ref, pt1_ref)

    @pl.when(c == 0)
    def _():
        acc_ref[...] = jnp.zeros_like(acc_ref)

    TT = xn_ref.shape[0]
    tsub = min(PEER_TSUB, TT)
    n_sub = TT // tsub
    n_a = PEER_EC // N_KEYS

    def hidden(j):
        act_refs[j % 2][...] = _gelu(_dot_nt(u_ref[...], xn_ref[j * tsub:(j + 1) * tsub, :])).astype(BF16)

    def gated(j):
        act_ref, pt_ref = act_refs[j % 2], pt_refs[j % 2]
        for tl in range(tsub // LANES):
            tc = j * (tsub // LANES) + tl
            cols = slice(tl * LANES, (tl + 1) * LANES)
            n1s = [[n1_ref[tc, h, a:a + 1, :].astype(BF16) for h in range(P_HEADS)] for a in range(n_a)]
            e1s = [[e1_ref[tc, h, a:a + 1, :].astype(BF16) for h in range(P_HEADS)] for a in range(n_a)]
            for b0 in range(0, N_KEYS, PEER_GATE_ROWS):
                brows = slice(b0, b0 + PEER_GATE_ROWS)
                gates = [None] * n_a
                for h in range(P_HEADS):
                    r2 = r2_ref[tc, h, brows, :]
                    e2 = e2_ref[tc, h, brows, :]
                    for a in range(n_a):
                        t = e1s[a][h] * jnp.minimum(jnp.maximum(n1s[a][h] - r2, 0.0), e2)
                        gates[a] = t if gates[a] is None else gates[a] + t
                for a in range(n_a):
                    rows = slice(a * N_KEYS + b0, a * N_KEYS + b0 + PEER_GATE_ROWS)
                    pt_ref[rows, cols] = gates[a] * act_ref[rows, cols]

    def project(j):
        acc_ref[:, j * tsub:(j + 1) * tsub] += _dot(vt_ref[...], pt_refs[j % 2][...])

    hidden(0)
    for j in range(n_sub):
        if j + 1 < n_sub:
            hidden(j + 1)
        gated(j)
        project(j)

    @pl.when(c == pl.num_programs(1) - 1)
    def _():
        y = x_ref[...] + gt_ref[0] * acc_ref[...].T
        if final_norm:
            y = y * lax.rsqrt(jnp.mean(y * y, axis=-1, keepdims=True) + EPS) * fg_ref[...]
        o_ref[...] = y


def _peer(xn, r2, e2, n1, e1, u, vt, x, gate, final_g, final_norm):
    B, L, D = x.shape
    T = B * L
    TT = min(PEER_TT, L)
    tsub = min(PEER_TSUB, TT)
    per_b = L // TT
    tok_spec = pl.BlockSpec((TT // LANES, P_HEADS, N_KEYS, LANES), lambda i, c: (i, 0, 0, 0))
    key_spec = pl.BlockSpec((TT // LANES, P_HEADS, PEER_EC // N_KEYS, LANES), lambda i, c: (i, 0, c, 0))
    out = pl.pallas_call(
        functools.partial(_peer_kernel, final_norm=final_norm),
        grid=(T // TT, N_EXPERTS // PEER_EC),
        in_specs=[
            pl.BlockSpec((TT, D), lambda i, c: (i, 0)),
            pl.BlockSpec((PEER_EC, D), lambda i, c: (c, 0)),
            pl.BlockSpec((D, PEER_EC), lambda i, c: (0, c)),
            tok_spec, tok_spec, key_spec, key_spec,
            pl.BlockSpec((TT, D), lambda i, c: (i, 0)),
            pl.BlockSpec((1, 1, D), lambda i, c: (i // per_b, 0, 0)),
            pl.BlockSpec((1, D), lambda i, c: (0, 0)),
        ],
        out_specs=pl.BlockSpec((TT, D), lambda i, c: (i, 0)),
        out_shape=jax.ShapeDtypeStruct((T, D), F32),
        scratch_shapes=[pltpu.VMEM((D, TT), F32),
                        pltpu.VMEM((PEER_EC, tsub), BF16), pltpu.VMEM((PEER_EC, tsub), BF16),
                        pltpu.VMEM((PEER_EC, tsub), BF16), pltpu.VMEM((PEER_EC, tsub), BF16)],
        compiler_params=_cparams(("parallel", "arbitrary")),
        name="peer_experts",
    )(xn, u, vt, r2, e2, n1, e1, x.reshape(T, D), gate, final_g)
    return out.reshape(B, L, D)


GDN_CONV_TILE = 512
GDN_PAD = 8


def _gdn_prep_kernel(q_ref, k_ref, v_ref, cwq_ref, cwk_ref, cwv_ref, qo_ref, ko_ref, vo_ref, pad_ref):
    Ls = q_ref.shape[1]
    T = min(GDN_CONV_TILE, Ls)
    zeros = jnp.zeros((GDN_PAD, LANES), F32)
    pad_ref[0:GDN_PAD, :] = zeros
    pad_ref[GDN_PAD + Ls:2 * GDN_PAD + Ls, :] = zeros
    for which, (x_ref, cw_ref, o_ref) in enumerate(((q_ref, cwq_ref, qo_ref), (k_ref, cwk_ref, ko_ref),
                                                    (v_ref, cwv_ref, vo_ref))):
        pad_ref[GDN_PAD:GDN_PAD + Ls, :] = x_ref[0]
        for t0 in range(0, Ls, T):
            acc = None
            for j in range(CONV_K):
                off = GDN_PAD - CONV_K // 2 + j + t0
                t = cw_ref[j:j + 1, :] * pad_ref[off:off + T, :]
                acc = t if acc is None else acc + t
            y = acc * jax.nn.sigmoid(acc)
            if which < 2:
                y = y * lax.rsqrt(jnp.sum(y * y, axis=-1, keepdims=True) + EPS)
            if which == 0:
                y = y * (A_DK ** -0.5)
            o_ref[0, t0:t0 + T, :] = y


def _gdn_prep(proj, conv_w):
    B, Ls, _ = proj.shape
    seq = lambda col: pl.BlockSpec((1, Ls, LANES), lambda b, h: (b, 0, col + h))
    cw = lambda col: pl.BlockSpec((CONV_K, LANES), lambda b, h: (0, col + h))
    out = pl.BlockSpec((1, Ls, LANES), lambda b, h: (b, 0, h))
    shape = jax.ShapeDtypeStruct((B, Ls, A_HEADS * LANES), F32)
    return pl.pallas_call(
        _gdn_prep_kernel,
        grid=(B, A_HEADS),
        in_specs=[seq(COL_A_Q), seq(COL_A_K), seq(COL_A_V), cw(0), cw(A_HEADS), cw(2 * A_HEADS)],
        out_specs=[out] * 3,
        out_shape=[shape] * 3,
        scratch_shapes=[pltpu.VMEM((Ls + 2 * GDN_PAD, LANES), F32)],
        compiler_params=_cparams(("parallel", "parallel")),
        name="gdn_prep",
    )(proj, proj, proj, conv_w, conv_w, conv_w)


def _softplus(x):
    return jnp.maximum(x, 0.0) + jnp.log1p(jnp.exp(-jnp.abs(x)))


GDN_GROUP = 8
GDN_INV_BLOCK = 16


def _gdn_prepass_group(h, chunks, nega_ref, dtb_ref, scr):
    mneg_scr, c_scr, qp_scr, dl_scr, o_scr = scr
    C = CHUNK
    row = lax.broadcasted_iota(jnp.int32, (C, C), 0)
    col = lax.broadcasted_iota(jnp.int32, (C, C), 1)
    lane = lax.broadcasted_iota(jnp.int32, (1, LANES), 1)
    eye = row == col
    chains = []
    for q, k, v, ba, row0, chunk_id in chunks:
        kb16 = k.astype(BF16)
        qk_raw = _dot_nt(q.astype(BF16), kb16)
        for d in range(2):
            before_eq_rc = (col <= row) if d == 0 else (col >= row)
            before_rc = (col < row) if d == 0 else (col > row)
            before_eq_cr = (row <= col) if d == 0 else (row >= col)
            bcol = jnp.sum(jnp.where(lane == d * A_HEADS + h, ba, 0.0), axis=-1, keepdims=True)
            acol = jnp.sum(jnp.where(lane == (2 + d) * A_HEADS + h, ba, 0.0), axis=-1, keepdims=True)
            beta = jax.nn.sigmoid(bcol)
            g = nega_ref[d, h] * _softplus(acol + dtb_ref[d, h])
            g_cols = jnp.broadcast_to(g, (C, C))
            gc_row = jnp.sum(jnp.where(before_eq_cr, g_cols, 0.0), axis=0, keepdims=True)
            g_row = jnp.sum(jnp.where(eye, g_cols, 0.0), axis=0, keepdims=True)
            gc_col = jnp.sum(jnp.where(before_eq_rc, jnp.broadcast_to(g_row, (C, C)), 0.0), axis=-1, keepdims=True)
            g_total = jnp.sum(g_row, axis=-1, keepdims=True)
            decay = jnp.exp(jnp.where(before_eq_rc, gc_col - gc_row, NEG_INF))
            kbeta = k * beta
            e_gc = jnp.exp(gc_col)
            chains.append(dict(
                d=d, row0=row0, chunk_id=chunk_id,
                n_pow=-jnp.where(before_rc, _dot_nt(kbeta.astype(BF16), kb16) * decay, 0.0),
                qk=jnp.where(before_eq_rc, qk_raw * decay, 0.0).astype(BF16),
                rhs=jnp.concatenate([v * beta, kbeta * e_gc], axis=-1).astype(BF16),
                kd=(k * jnp.exp(g_total - gc_col)).astype(BF16),
                qd=q * e_gc,
                dl=jnp.exp(g_total)))
    eye_f = jnp.where(eye, 1.0, 0.0)
    same_blk = (row // GDN_INV_BLOCK) == (col // GDN_INV_BLOCK)
    b16 = lambda t: t.astype(BF16)
    n_d = [jnp.where(same_blk, ch['n_pow'], 0.0) for ch in chains]
    a_o = [jnp.where(same_blk, 0.0, -ch['n_pow']) for ch in chains]
    d_inv = [eye_f + n for n in n_d]
    for _ in range(3):
        n_d = [_dot(b16(n), b16(n)) for n in n_d]
        d_inv = [dd + _dot(b16(dd), b16(n)) for dd, n in zip(d_inv, n_d)]
    ms = [_dot(b16(dd), b16(a)) for dd, a in zip(d_inv, a_o)]
    m2 = [_dot(b16(m), b16(m)) for m in ms]
    x1 = [(eye_f - m) + _dot(b16(eye_f - m), b16(mm)) for m, mm in zip(ms, m2)]
    invs = [_dot(b16(x), b16(dd)) for x, dd in zip(x1, d_inv)]
    uws = [_dot(inv.astype(BF16), ch['rhs']).astype(BF16) for inv, ch in zip(invs, chains)]
    kts = [_dot_tn(ch['kd'], uw) for ch, uw in zip(chains, uws)]
    qqs = [_dot(ch['qk'], uw) for ch, uw in zip(chains, uws)]
    for ch, kt, qq in zip(chains, kts, qqs):
        d, cid = ch['d'], ch['chunk_id']
        rows = pl.ds(ch['row0'], C)
        c_scr[d, cid] = kt[:, :A_DV]
        mneg_scr[d, cid] = (-kt[:, A_DV:]).astype(BF16)
        o_scr[d, rows, :] = qq[:, :A_DV]
        qp_scr[d, rows, :] = (ch['qd'] - qq[:, A_DV:]).astype(BF16)
        dl_scr[d, pl.ds(cid * 8, 8), :] = jnp.broadcast_to(ch['dl'], (8, LANES))


def _gdn_scan_step(states, scr, row0s, chunk_ids):
    mneg_scr, c_scr, qp_scr, dl_scr, o_scr = scr
    s16 = [S.astype(BF16) for S in states]
    upd = [_dot(mneg_scr[d, chunk_ids[d]], s16[d]) for d in range(2)]
    for d in range(2):
        rows = pl.ds(row0s[d], CHUNK)
        o_scr[d, rows, :] += _dot(qp_scr[d, rows, :], s16[d])
    return tuple(states[d] * dl_scr[d, pl.ds(chunk_ids[d] * 8, 1), :] + c_scr[d, chunk_ids[d]] + upd[d]
                 for d in range(2))


def _gdn_scan_kernel(nega_ref, dtb_ref, qc_ref, kc_ref, vc_ref, bac_ref, zc_ref, ql_ref, kl_ref, vl_ref, bal_ref,
                     zl_ref, ng_ref, yl_ref, yc_ref, mneg_scr, c_scr, qp_scr, dl_scr, o_scr):
    h = pl.program_id(1)
    Lc, L = qc_ref.shape[1], ql_ref.shape[1]
    nc, nl = Lc // CHUNK, L // CHUNK
    scr = (mneg_scr, c_scr, qp_scr, dl_scr, o_scr)

    for c0 in range(0, nc, GDN_GROUP):
        _gdn_prepass_group(h, [(qc_ref[0, c * CHUNK:(c + 1) * CHUNK, :], kc_ref[0, c * CHUNK:(c + 1) * CHUNK, :],
                                vc_ref[0, c * CHUNK:(c + 1) * CHUNK, :], bac_ref[0, c * CHUNK:(c + 1) * CHUNK, :],
                                c * CHUNK, c) for c in range(c0, min(c0 + GDN_GROUP, nc))],
                           nega_ref, dtb_ref, scr)

    def pre_body(grp, carry):
        chunks = []
        for j in range(GDN_GROUP):
            c = grp * GDN_GROUP + j
            r = pl.multiple_of(c * CHUNK, CHUNK)
            rows = pl.ds(r, CHUNK)
            chunks.append((ql_ref[0, rows, :], kl_ref[0, rows, :], vl_ref[0, rows, :], bal_ref[0, rows, :],
                           Lc + r, nc + c))
        _gdn_prepass_group(h, chunks, nega_ref, dtb_ref, scr)
        return carry

    lax.fori_loop(0, nl // GDN_GROUP, pre_body, 0)

    states = (jnp.zeros((A_DK, A_DV), F32), jnp.zeros((A_DK, A_DV), F32))
    for s in range(nc):
        states = _gdn_scan_step(states, scr, (s * CHUNK, (nc - 1 - s) * CHUNK), (s, nc - 1 - s))

    def scan_body(s, states):
        cf = s
        cb = nl - 1 - s
        return _gdn_scan_step(states, scr,
                              (pl.multiple_of(Lc + cf * CHUNK, CHUNK), pl.multiple_of(Lc + cb * CHUNK, CHUNK)),
                              (nc + cf, nc + cb))

    lax.fori_loop(0, nl, scan_body, states)

    def finish(z_ref, y_ref, base, n):
        T = min(GDN_CONV_TILE, n)
        for t0 in range(0, n, T):
            o = o_scr[0, base + t0:base + t0 + T, :] + o_scr[1, base + t0:base + t0 + T, :]
            y = o * lax.rsqrt(jnp.mean(o * o, axis=-1, keepdims=True) + EPS) * ng_ref[...]
            z = z_ref[0, t0:t0 + T, :]
            y_ref[0, t0:t0 + T, :] = y * (z * jax.nn.sigmoid(z))

    finish(zl_ref, yl_ref, Lc, L)
    finish(zc_ref, yc_ref, 0, Lc)


def _gdn_scan(nega, dtb, qkv_ctx, ba_ctx, proj_ctx, qkv_lat, ba_lat, proj_lat, norm_g):
    B, L, _ = proj_lat.shape
    Lc = proj_ctx.shape[1]
    Lt = L + Lc
    head = lambda n: pl.BlockSpec((1, n, LANES), lambda b, h: (b, 0, h))
    full = lambda n: pl.BlockSpec((1, n, LANES), lambda b, h: (b, 0, 0))
    zcol = lambda n: pl.BlockSpec((1, n, LANES), lambda b, h: (b, 0, COL_A_Z + h))
    smem = pl.BlockSpec(memory_space=pltpu.SMEM)
    return pl.pallas_call(
        _gdn_scan_kernel,
        grid=(B, A_HEADS),
        in_specs=[smem, smem,
                  head(Lc), head(Lc), head(Lc), full(Lc), zcol(Lc),
                  head(L), head(L), head(L), full(L), zcol(L),
                  pl.BlockSpec((1, LANES), lambda b, h: (0, 0))],
        out_specs=[head(L), head(Lc)],
        out_shape=[jax.ShapeDtypeStruct((B, L, A_HEADS * LANES), F32),
                   jax.ShapeDtypeStruct((B, Lc, A_HEADS * LANES), F32)],
        scratch_shapes=[
            pltpu.VMEM((2, Lt // CHUNK, A_DK, A_DV), BF16),
            pltpu.VMEM((2, Lt // CHUNK, A_DK, A_DV), F32),
            pltpu.VMEM((2, Lt, A_DK), BF16),
            pltpu.VMEM((2, Lt // CHUNK * 8, LANES), F32),
            pltpu.VMEM((2, Lt, A_DV), F32),
        ],
        compiler_params=_cparams(("parallel", "parallel")),
        name="gdn_scan",
    )(nega, dtb, *qkv_ctx, ba_ctx, proj_ctx, *qkv_lat, ba_lat, proj_lat, norm_g)


def _gdn_pallas(proj_lat, ba_lat, proj_ctx, ba_ctx, conv_w, a_log, dt_bias, norm_g):
    qkv_lat = _gdn_prep(proj_lat, conv_w)
    qkv_ctx = _gdn_prep(proj_ctx, conv_w)
    return _gdn_scan(-jnp.exp(a_log), dt_bias, qkv_ctx, ba_ctx, proj_ctx, qkv_lat, ba_lat, proj_lat, norm_g[None, :])


def _layer(i, x, xc, c, c_ctx, p, cos_t, sin_t, ctx_out, final_g, final_norm):
    B, L, D = x.shape
    Lc = xc.shape[1]
    mod_lat = (jax.nn.silu(c) @ p['ada_w'] + p['ada_b'])[:, None, :]
    mod_ctx = jnp.broadcast_to((jax.nn.silu(c_ctx) @ p['ada_w'] + p['ada_b'])[None, None, :], (B, 1, 6 * D))
    sh1, sc1, gt1, sh2, sc2, gt2 = jnp.split(mod_lat, 6, axis=-1)
    csh1, csc1, cgt1, csh2, csc2, cgt2 = jnp.split(mod_ctx, 6, axis=-1)
    lam_init = 0.8 - 0.6 * math.exp(-0.3 * i)

    w_in = p['w_in']
    n_a = 4 * A_HEADS * A_DK
    w_main = jnp.concatenate([w_in[:, :n_a], w_in[:, n_a + 4 * A_HEADS:]], axis=1).astype(BF16)
    w_ba = jnp.pad(w_in[:, n_a:n_a + 4 * A_HEADS], ((0, 0), (0, LANES - 4 * A_HEADS))).astype(BF16)
    n1 = p['norm1_g'][None, :]
    proj_lat, ba_lat = _inproj(x, sh1, sc1, n1, w_main, w_ba, 1024)
    proj_ctx, ba_ctx = _inproj(xc, csh1, csc1, n1, w_main, w_ba, Lc)

    ya, ya_c = _gdn_pallas(proj_lat, ba_lat, proj_ctx, ba_ctx, p['gdn_conv'], p['gdn_a_log'], p['gdn_dt_bias'],
                           p['gdn_norm_g'])

    lp = p['diff_lambda']
    lam = (jnp.exp(jnp.sum(lp[0] * lp[1])) - jnp.exp(jnp.sum(lp[2] * lp[3])) + lam_init).reshape(1)
    q_r, k_r = _rope(proj_lat, cos_t, sin_t, 512)
    subln = p['diff_subln_g'][None, :]
    yb = _diff_lat(lam, q_r, k_r, proj_lat, proj_ctx, subln, 1.0 - lam_init, 512)
    yb_c = _diff_ctx(lam, proj_ctx, subln, 1.0 - lam_init) if ctx_out else None

    yc = _natten_lat(proj_lat, proj_ctx, _natten_bias(p['na_rpb']))
    yc_c = _natten_ctx(proj_ctx) if ctx_out else None

    w_up = p['w_up'].astype(BF16)
    w_out = p['w_out'].astype(BF16)
    wq = p['peer_wq'].astype(BF16)
    keys = p['peer_keys'].reshape(2 * P_HEADS, N_KEYS, P_DKH).astype(BF16)
    u = p['peer_u'].astype(BF16)
    vt = p['peer_v'].T.astype(BF16)
    n2 = p['norm2_g'][None, :]
    fg = final_g[None, :]

    x = _merge(ya, yb, yc, proj_lat, w_up, w_out, x, gt1, 512)
    pq = _peerq(x, sh2, sc2, n2, wq, keys, 256)
    x = _peer(*pq, u, vt, x, gt2, fg, final_norm)
    if ctx_out:
        xc = _merge(ya_c, yb_c, yc_c, proj_ctx, w_up, w_out, xc, cgt1, Lc)
        pq = _peerq(xc, csh2, csc2, n2, wq, keys, Lc)
        xc = _peer(*pq, u, vt, xc, cgt2, fg, False)
    return x, xc


def kernel(x, c, ctx, c_ctx, norm1_g, norm2_g, ada_w, ada_b, w_in, gdn_conv, gdn_a_log, gdn_dt_bias, gdn_norm_g,
           diff_lambda, diff_subln_g, na_rpb, w_up, w_out, peer_wq, peer_keys, peer_u, peer_v, final_g):
    cos_t, sin_t = _rope_tables(x.shape[1])
    xc = ctx
    for i in range(DEPTH):
        p = dict(norm1_g=norm1_g[i], norm2_g=norm2_g[i], ada_w=ada_w[i], ada_b=ada_b[i], w_in=w_in[i],
                 gdn_conv=gdn_conv[i], gdn_a_log=gdn_a_log[i], gdn_dt_bias=gdn_dt_bias[i], gdn_norm_g=gdn_norm_g[i],
                 diff_lambda=diff_lambda[i], diff_subln_g=diff_subln_g[i], na_rpb=na_rpb[i], w_up=w_up[i],
                 w_out=w_out[i], peer_wq=peer_wq[i], peer_keys=peer_keys[i], peer_u=peer_u[i], peer_v=peer_v[i])
        x, xc = _layer(i, x, xc, c, c_ctx, p, cos_t, sin_t, i < DEPTH - 1, final_g, i == DEPTH - 1)
    return x
```

```python
import functools
import math

import numpy as np
import jax
import jax.numpy as jnp
from jax import lax
from jax.experimental import pallas as pl
from jax.experimental.pallas import tpu as pltpu

F32 = jnp.float32
BF16 = jnp.bfloat16

D_MODEL = 1024
DEPTH = 2
GRID_W = 64
EPS = 1e-6
NEG_INF = -1e30

A_HEADS = 4
A_DK = 128
A_DV = 128
CONV_K = 5
CHUNK = 64
B_HEADS = 4
B_DH = 64
ROPE_BASE = 10000.0
C_HEADS = 8
C_DH = 64
WIN_R = 8
WIN_C = 16
N_BRANCH = 3
BRANCH_W = 512
P_HEADS = 8
N_KEYS = 128
N_EXPERTS = N_KEYS * N_KEYS
P_DKH = 128
P_TOPK = 16

LANES = 128
VMEM_LIMIT = 56 * 1024 * 1024

COL_A_Q, COL_A_K, COL_A_V, COL_A_Z = 0, 4, 8, 12
COL_B_Q, COL_B_K, COL_B_V = 16, 20, 24
COL_C_Q, COL_C_K, COL_C_V = 28, 32, 36
COL_GATE = 40
MAIN_COLS = 64 * LANES


def _cparams(sem):
    return pltpu.CompilerParams(dimension_semantics=sem, vmem_limit_bytes=VMEM_LIMIT)


def _dot(a, b):
    return jnp.dot(a, b, preferred_element_type=F32)


def _dot_split(a, b):
    ah = a.astype(BF16)
    al = (a - ah.astype(F32)).astype(BF16)
    bh = b.astype(BF16)
    bl = (b - bh.astype(F32)).astype(BF16)
    return _dot(ah, bh) + (_dot(ah, bl) + _dot(al, bh))


def _dot_nt(a, b):
    return lax.dot_general(a, b, (((1,), (1,)), ((), ())), preferred_element_type=F32)


def _dot_tn(a, b):
    return lax.dot_general(a, b, (((0,), (0,)), ((), ())), preferred_element_type=F32)


INPROJ_ROWS = 1024
ROPE_ROWS = 512
DIFF_Q_TILE = 512
MERGE_ROWS = 512
PEERQ_ROWS = 256
ADALN_COLS = 1024
SUBLANES = 8


def _adaln_kernel(c_ref, w_ref, b_ref, o_ref):
    c = c_ref[...]
    h = (c * jax.nn.sigmoid(c)).astype(BF16)
    o_ref[...] = _dot(h, w_ref[...].astype(BF16)) + b_ref[...]


def _adaln(cond, ada_w, ada_b):
    R, D = cond.shape
    N = ada_w.shape[1]
    return pl.pallas_call(
        _adaln_kernel,
        grid=(N // ADALN_COLS,),
        in_specs=[
            pl.BlockSpec((R, D), lambda j: (0, 0)),
            pl.BlockSpec((D, ADALN_COLS), lambda j: (0, j)),
            pl.BlockSpec((1, ADALN_COLS), lambda j: (0, j)),
        ],
        out_specs=pl.BlockSpec((R, ADALN_COLS), lambda j: (0, j)),
        out_shape=jax.ShapeDtypeStruct((R, N), F32),
        compiler_params=_cparams(("parallel",)),
        name="adaln",
    )(cond, ada_w, ada_b[None, :])


def _inproj_kernel(x_ref, sh_ref, sc_ref, g_ref, w_ref, wba_ref, o_ref, oba_ref, hn_ref):
    @pl.when(pl.program_id(2) == 0)
    def _():
        x = x_ref[0]
        y = x * lax.rsqrt(jnp.mean(x * x, axis=-1, keepdims=True) + EPS) * g_ref[...]
        h = (y * (1.0 + sc_ref[0]) + sh_ref[0]).astype(BF16)
        hn_ref[...] = h
        oba_ref[0] = _dot(h, wba_ref[...])

    o_ref[0] = _dot(hn_ref[...], w_ref[...])


def _inproj(x, shift, scale, g, w_main, w_ba, tm):
    B, L, D = x.shape
    tn = 1024
    return pl.pallas_call(
        _inproj_kernel,
        grid=(B, L // tm, MAIN_COLS // tn),
        in_specs=[
            pl.BlockSpec((1, tm, D), lambda b, i, j: (b, i, 0)),
            pl.BlockSpec((1, 1, D), lambda b, i, j: (b, 0, 0)),
            pl.BlockSpec((1, 1, D), lambda b, i, j: (b, 0, 0)),
            pl.BlockSpec((1, D), lambda b, i, j: (0, 0)),
            pl.BlockSpec((D, tn), lambda b, i, j: (0, j)),
            pl.BlockSpec((D, LANES), lambda b, i, j: (0, 0)),
        ],
        out_specs=[
            pl.BlockSpec((1, tm, tn), lambda b, i, j: (b, i, j)),
            pl.BlockSpec((1, tm, LANES), lambda b, i, j: (b, i, 0)),
        ],
        out_shape=[
            jax.ShapeDtypeStruct((B, L, MAIN_COLS), F32),
            jax.ShapeDtypeStruct((B, L, LANES), F32),
        ],
        scratch_shapes=[pltpu.VMEM((tm, D), BF16)],
        compiler_params=_cparams(("parallel", "parallel", "arbitrary")),
        name="inproj",
    )(x, shift, scale, g, w_main, w_ba)


def _rope_tables(L):
    t = np.arange(L)
    row_pos, col_pos = t // GRID_W, t % GRID_W
    lane = np.arange(LANES)
    axis = (lane % 64) // 32
    f = lane % 16
    inv = 1.0 / (ROPE_BASE ** (f.astype(np.float32) / 16.0))
    pos = np.where(axis[None, :] == 0, row_pos[:, None], col_pos[:, None]).astype(np.float32)
    ang = jnp.asarray(pos) * jnp.asarray(inv.astype(np.float32))[None, :]
    first = jnp.asarray(((lane % 32) < 16)[None, :])
    return jnp.cos(ang), jnp.where(first, -jnp.sin(ang), jnp.sin(ang))


def _rope_kernel(q_ref, k_ref, cos_ref, sin_ref, qo_ref, ko_ref):
    lane = lax.broadcasted_iota(jnp.int32, (1, LANES), 1)
    first = (lane % 32) < 16
    c, s = cos_ref[...], sin_ref[...]

    def rope(x):
        partner = jnp.where(first, pltpu.roll(x, LANES - 16, 1), pltpu.roll(x, 16, 1))
        return x * c + partner * s

    for h in range(B_HEADS):
        sl = slice(h * LANES, (h + 1) * LANES)
        qo_ref[0, :, sl] = (rope(q_ref[0, :, sl]) * (B_DH ** -0.5)).astype(BF16)
        ko_ref[0, :, sl] = rope(k_ref[0, :, sl]).astype(BF16)


def _rope(proj, cos_t, sin_t, tr):
    B, L, _ = proj.shape
    W = B_HEADS * LANES
    return pl.pallas_call(
        _rope_kernel,
        grid=(B, L // tr),
        in_specs=[
            pl.BlockSpec((1, tr, W), lambda b, i: (b, i, COL_B_Q * LANES // W)),
            pl.BlockSpec((1, tr, W), lambda b, i: (b, i, COL_B_K * LANES // W)),
            pl.BlockSpec((tr, LANES), lambda b, i: (i, 0)),
            pl.BlockSpec((tr, LANES), lambda b, i: (i, 0)),
        ],
        out_specs=[pl.BlockSpec((1, tr, W), lambda b, i: (b, i, 0))] * 2,
        out_shape=[jax.ShapeDtypeStruct((B, L, W), BF16)] * 2,
        compiler_params=_cparams(("parallel", "parallel")),
        name="rope",
    )(proj, proj, cos_t, sin_t)


def _softmax_pv(q, key_vals):
    scores = [_dot_nt(q, k) for k, _ in key_vals]
    m = functools.reduce(jnp.maximum, [s.max(axis=-1, keepdims=True) for s in scores])
    es = [jnp.exp(s - m) for s in scores]
    denom = functools.reduce(jnp.add, [e.sum(axis=-1, keepdims=True) for e in es])
    o = functools.reduce(jnp.add, [_dot(e.astype(BF16), v) for e, (_, v) in zip(es, key_vals)])
    return o / denom


DIFF_Q_ROWS = 256


def _diff_finish(o0, o1, lp, g, lam_init):
    lam = (jnp.exp(jnp.sum(lp[0:1] * lp[1:2], axis=-1, keepdims=True))
           - jnp.exp(jnp.sum(lp[2:3] * lp[3:4], axis=-1, keepdims=True)) + lam_init)
    o = o0 - lam * o1
    return o * lax.rsqrt(jnp.mean(o * o, axis=-1, keepdims=True) + EPS) * g * (1.0 - lam_init)


def _diff_lat_kernel(lam_ref, q_ref, kl_ref, vl_ref, kc_ref, vc_ref, g_ref, o_ref, *, lam_init):
    lane = lax.broadcasted_iota(jnp.int32, (1, LANES), 1)
    low = lane < B_DH
    tq = q_ref.shape[1]
    kl, vl = kl_ref[0], vl_ref[0].astype(BF16)
    kc, vc = kc_ref[0].astype(BF16), vc_ref[0].astype(BF16)
    units = [dict(r0=r0, mp=mp) for r0 in range(0, tq, DIFF_Q_ROWS) for mp in range(2)]

    def scores(un):
        q = q_ref[0, un['r0']:un['r0'] + DIFF_Q_ROWS, :]
        q = (jnp.where(low, q, 0) if un['mp'] == 0 else jnp.where(low, 0, q)).astype(BF16)
        un['s'] = (_dot_nt(q, kl), _dot_nt(q, kc))

    def softmax(un):
        s_l, s_c = un.pop('s')
        m = jnp.maximum(s_l.max(axis=-1, keepdims=True), s_c.max(axis=-1, keepdims=True))
        e_l, e_c = jnp.exp(s_l - m), jnp.exp(s_c - m)
        un['denom'] = e_l.sum(axis=-1, keepdims=True) + e_c.sum(axis=-1, keepdims=True)
        un['e'] = (e_l.astype(BF16), e_c.astype(BF16))

    def values(un):
        e_l, e_c = un.pop('e')
        un['o'] = (_dot(e_l, vl) + _dot(e_c, vc)) / un['denom']

    scores(units[0])
    scores(units[1])
    for k, un in enumerate(units):
        softmax(un)
        if k + 2 < len(units):
            scores(units[k + 2])
        values(un)
    for k in range(0, len(units), 2):
        r0 = units[k]['r0']
        o_ref[0, r0:r0 + DIFF_Q_ROWS, :] = _diff_finish(units[k]['o'], units[k + 1]['o'], lam_ref[...], g_ref[...],
                                                        lam_init)


def _diff_lat(lam, q_r, k_r, proj_lat, proj_ctx, subln_g, lam_init, tq):
    B, L, _ = proj_lat.shape
    Lc = proj_ctx.shape[1]
    return pl.pallas_call(
        functools.partial(_diff_lat_kernel, lam_init=lam_init),
        grid=(B, B_HEADS, L // tq),
        in_specs=[
            pl.BlockSpec((4, B_DH), lambda b, h, i: (0, 0)),
            pl.BlockSpec((1, tq, LANES), lambda b, h, i: (b, i, h)),
            pl.BlockSpec((1, L, LANES), lambda b, h, i: (b, 0, h)),
            pl.BlockSpec((1, L, LANES), lambda b, h, i: (b, 0, COL_B_V + h)),
            pl.BlockSpec((1, Lc, LANES), lambda b, h, i: (b, 0, COL_B_K + h)),
            pl.BlockSpec((1, Lc, LANES), lambda b, h, i: (b, 0, COL_B_V + h)),
            pl.BlockSpec((1, LANES), lambda b, h, i: (0, 0)),
        ],
        out_specs=pl.BlockSpec((1, tq, LANES), lambda b, h, i: (b, i, h)),
        out_shape=jax.ShapeDtypeStruct((B, L, B_HEADS * LANES), F32),
        compiler_params=_cparams(("parallel", "parallel", "parallel")),
        name="diff_lat",
    )(lam, q_r, k_r, proj_lat, proj_ctx, proj_ctx, subln_g)


def _diff_ctx_kernel(lam_ref, q_ref, k_ref, v_ref, g_ref, o_ref, *, lam_init):
    lane = lax.broadcasted_iota(jnp.int32, (1, LANES), 1)
    low = lane < B_DH
    q = q_ref[0] * (B_DH ** -0.5)
    kv = [(k_ref[0].astype(BF16), v_ref[0].astype(BF16))]
    o0 = _softmax_pv(jnp.where(low, q, 0).astype(BF16), kv)
    o1 = _softmax_pv(jnp.where(low, 0, q).astype(BF16), kv)
    o_ref[0] = _diff_finish(o0, o1, lam_ref[...], g_ref[...], lam_init)


def _diff_ctx(lam, proj_ctx, subln_g, lam_init):
    B, Lc, _ = proj_ctx.shape
    return pl.pallas_call(
        functools.partial(_diff_ctx_kernel, lam_init=lam_init),
        grid=(B, B_HEADS),
        in_specs=[
            pl.BlockSpec((4, B_DH), lambda b, h: (0, 0)),
            pl.BlockSpec((1, Lc, LANES), lambda b, h: (b, 0, COL_B_Q + h)),
            pl.BlockSpec((1, Lc, LANES), lambda b, h: (b, 0, COL_B_K + h)),
            pl.BlockSpec((1, Lc, LANES), lambda b, h: (b, 0, COL_B_V + h)),
            pl.BlockSpec((1, LANES), lambda b, h: (0, 0)),
        ],
        out_specs=pl.BlockSpec((1, Lc, LANES), lambda b, h: (b, 0, h)),
        out_shape=jax.ShapeDtypeStruct((B, Lc, B_HEADS * LANES), F32),
        compiler_params=_cparams(("parallel", "parallel")),
        name="diff_ctx",
    )(lam, proj_ctx, proj_ctx, proj_ctx, subln_g)


NA_ROWS_PER_STEP = 8
NA_KEYS = WIN_R * GRID_W


def _natten_bias(rpb):
    col = np.arange(GRID_W)
    col_start = np.clip(col - WIN_C // 2, 0, GRID_W - WIN_C)
    col_mask = (col[None, :] >= col_start[:, None]) & (col[None, :] < col_start[:, None] + WIN_C)
    dc = np.clip(col[None, :] - col[:, None], -(WIN_C - 1), WIN_C - 1) + WIN_C - 1
    bias = jnp.where(jnp.asarray(col_mask), rpb.astype(F32)[:, :, dc], NEG_INF)
    return jnp.concatenate([bias[:, :-1], bias[:, 1:]], axis=-1)


def _natten_kernel(q_ref, k_ref, v_ref, kc_ref, vc_ref, bias_ref, o_ref, *, rows):
    lane = lax.broadcasted_iota(jnp.int32, (1, LANES), 1)
    low = lane < C_DH
    kc = kc_ref[0].astype(BF16)
    vc = vc_ref[0].astype(BF16)
    units = []
    for rr in range(NA_ROWS_PER_STEP):
        r = pl.program_id(2) * NA_ROWS_PER_STEP + rr
        rs = jnp.clip(r - WIN_R // 2, 0, rows - WIN_R)
        cfg = r - rs
        start = pl.multiple_of(rs * GRID_W, GRID_W)
        q = q_ref[0, rr * GRID_W:(rr + 1) * GRID_W, :] * (C_DH ** -0.5)
        q2 = jnp.concatenate([jnp.where(low, q, 0), jnp.where(low, 0, q)], axis=0).astype(BF16)
        bias = jnp.concatenate(
            [jnp.concatenate([bias_ref[hh, WIN_R - 1 - cfg + j] for j in range(0, WIN_R, 2)], axis=-1)
             for hh in range(2)], axis=0)
        units.append(dict(q=q2, bias=bias, start=start))
    for un in units:
        kw = k_ref[0, pl.ds(un['start'], NA_KEYS), :].astype(BF16)
        un['s_lat'] = _dot_nt(un['q'], kw) + un['bias']
        un['s_ctx'] = _dot_nt(un['q'], kc)
    for un in units:
        s_lat, s_ctx = un['s_lat'], un['s_ctx']
        m = jnp.maximum(s_lat.max(axis=-1, keepdims=True), s_ctx.max(axis=-1, keepdims=True))
        e_lat, e_ctx = jnp.exp(s_lat - m), jnp.exp(s_ctx - m)
        un['denom'] = e_lat.sum(axis=-1, keepdims=True) + e_ctx.sum(axis=-1, keepdims=True)
        un['e_lat'], un['e_ctx'] = e_lat.astype(BF16), e_ctx.astype(BF16)
    for rr, un in enumerate(units):
        vw = v_ref[0, pl.ds(un['start'], NA_KEYS), :].astype(BF16)
        o = (_dot(un['e_lat'], vw) + _dot(un['e_ctx'], vc)) / un['denom']
        o_ref[0, rr * GRID_W:(rr + 1) * GRID_W, :] = jnp.where(low, o[:GRID_W], o[GRID_W:])


def _natten_lat(proj_lat, proj_ctx, bias):
    B, L, _ = proj_lat.shape
    Lc = proj_ctx.shape[1]
    rows = L // GRID_W
    HP = C_HEADS // 2
    tq = NA_ROWS_PER_STEP * GRID_W
    return pl.pallas_call(
        functools.partial(_natten_kernel, rows=rows),
        grid=(B, HP, rows // NA_ROWS_PER_STEP),
        in_specs=[
            pl.BlockSpec((1, tq, LANES), lambda b, h, i: (b, i, COL_C_Q + h)),
            pl.BlockSpec((1, L, LANES), lambda b, h, i: (b, 0, COL_C_K + h)),
            pl.BlockSpec((1, L, LANES), lambda b, h, i: (b, 0, COL_C_V + h)),
            pl.BlockSpec((1, Lc, LANES), lambda b, h, i: (b, 0, COL_C_K + h)),
            pl.BlockSpec((1, Lc, LANES), lambda b, h, i: (b, 0, COL_C_V + h)),
            pl.BlockSpec((2, 2 * WIN_R - 2, GRID_W, 2 * GRID_W), lambda b, h, i: (h, 0, 0, 0)),
        ],
        out_specs=pl.BlockSpec((1, tq, LANES), lambda b, h, i: (b, i, h)),
        out_shape=jax.ShapeDtypeStruct((B, L, HP * LANES), F32),
        compiler_params=_cparams(("parallel", "parallel", "parallel")),
        name="natten_lat",
    )(proj_lat, proj_lat, proj_lat, proj_ctx, proj_ctx, bias)


def _natten_ctx_kernel(q_ref, k_ref, v_ref, o_ref):
    lane = lax.broadcasted_iota(jnp.int32, (1, LANES), 1)
    low = lane < C_DH
    q = q_ref[0] * (C_DH ** -0.5)
    kv = [(k_ref[0].astype(BF16), v_ref[0].astype(BF16))]
    o0 = _softmax_pv(jnp.where(low, q, 0).astype(BF16), kv)
    o1 = _softmax_pv(jnp.where(low, 0, q).astype(BF16), kv)
    o_ref[0] = jnp.where(low, o0, o1)


def _natten_ctx(proj_ctx):
    B, Lc, _ = proj_ctx.shape
    HP = C_HEADS // 2
    return pl.pallas_call(
        _natten_ctx_kernel,
        grid=(B, HP),
        in_specs=[
            pl.BlockSpec((1, Lc, LANES), lambda b, h: (b, 0, COL_C_Q + h)),
            pl.BlockSpec((1, Lc, LANES), lambda b, h: (b, 0, COL_C_K + h)),
            pl.BlockSpec((1, Lc, LANES), lambda b, h: (b, 0, COL_C_V + h)),
        ],
        out_specs=pl.BlockSpec((1, Lc, LANES), lambda b, h: (b, 0, h)),
        out_shape=jax.ShapeDtypeStruct((B, Lc, HP * LANES), F32),
        compiler_params=_cparams(("parallel", "parallel")),
        name="natten_ctx",
    )(proj_ctx, proj_ctx, proj_ctx)


def _merge_kernel(ya_ref, yb_ref, yc_ref, g0_ref, g1_ref, g2_ref, wup_ref, wout_ref, x_ref, gt_ref, o_ref):
    acc = None
    for n, (y_ref, g_ref) in enumerate(((ya_ref, g0_ref), (yb_ref, g1_ref), (yc_ref, g2_ref))):
        up = _dot(y_ref[0].astype(BF16), wup_ref[n])
        t = jax.nn.sigmoid(g_ref[0]) * up
        acc = t if acc is None else acc + t
    r = _dot(acc.astype(BF16), wout_ref[...])
    o_ref[0] = x_ref[0] + gt_ref[0] * r


def _merge(ya, yb, yc, proj, w_up, w_out, x, gate, tm):
    B, L, D = x.shape
    gcol = COL_GATE * LANES // D
    yspec = pl.BlockSpec((1, tm, BRANCH_W), lambda b, i: (b, i, 0))
    return pl.pallas_call(
        _merge_kernel,
        grid=(B, L // tm),
        in_specs=[
            yspec, yspec, yspec,
            pl.BlockSpec((1, tm, D), lambda b, i: (b, i, gcol)),
            pl.BlockSpec((1, tm, D), lambda b, i: (b, i, gcol + 1)),
            pl.BlockSpec((1, tm, D), lambda b, i: (b, i, gcol + 2)),
            pl.BlockSpec((N_BRANCH, BRANCH_W, D), lambda b, i: (0, 0, 0)),
            pl.BlockSpec((D, D), lambda b, i: (0, 0)),
            pl.BlockSpec((1, tm, D), lambda b, i: (b, i, 0)),
            pl.BlockSpec((1, 1, D), lambda b, i: (b, 0, 0)),
        ],
        out_specs=pl.BlockSpec((1, tm, D), lambda b, i: (b, i, 0)),
        out_shape=jax.ShapeDtypeStruct((B, L, D), F32),
        compiler_params=_cparams(("parallel", "parallel")),
        name="merge",
    )(ya, yb, yc, proj, proj, proj, w_up, w_out, x, gate)


PEER_CAND = P_TOPK + 1
PEER_A_PAD = 24
PEER_CAND_ROWS = PEER_A_PAD + 7 * 8 + 16


PEER_NO_RANK = float(N_KEYS)


def _extract_top(s, n, with_rank=False):
    vals = []
    rank = jnp.full(s.shape, PEER_NO_RANK, F32) if with_rank else None
    for r in range(n):
        m = jnp.max(s, axis=0, keepdims=True)
        vals.append(m)
        hit = s == m
        if with_rank:
            rank = jnp.where(hit, float(r), rank)
        s = jnp.where(hit, NEG_INF, s)
    return (vals, rank) if with_rank else vals


def _peerq_kernel(x_ref, sh_ref, sc_ref, g_ref, wq_ref, keys_ref,
                  xn_ref, r2_ref, e2_ref, n1_ref, e1_ref, ab_ref, cand_ref):
    x = x_ref[0]
    tm = x.shape[0]
    y = x * lax.rsqrt(jnp.mean(x * x, axis=-1, keepdims=True) + EPS) * g_ref[...]
    xn = (y * (1.0 + sc_ref[0]) + sh_ref[0]).astype(BF16)
    xn_ref[...] = xn
    q = _dot(xn, wq_ref[...]).astype(BF16)
    row8 = lax.broadcasted_iota(jnp.int32, (8, 1), 0)
    row16 = lax.broadcasted_iota(jnp.int32, (16, 1), 0)
    for h in range(P_HEADS):
        for tc in range(tm // LANES):
            qt = q[tc * LANES:(tc + 1) * LANES, :]
            s = []
            for p in range(2):
                hp = 2 * h + p
                st = _dot_nt(keys_ref[hp], qt[:, hp * P_DKH:(hp + 1) * P_DKH])
                s.append(st)
                ab_ref[p] = jnp.full((PEER_A_PAD, LANES), NEG_INF, F32)
                if p == 0:
                    vals = _extract_top(st, PEER_CAND)
                else:
                    vals, rank2 = _extract_top(st, PEER_CAND, with_rank=True)
                for r, m in enumerate(vals):
                    ab_ref[p, r:r + 1, :] = m
            a_all, b_all = ab_ref[0], ab_ref[1]
            cand_ref[0:PEER_A_PAD, :] = a_all[0:1, :] + b_all
            for i in range(1, 8):
                n_i = PEER_CAND // (i + 1)
                cand_ref[PEER_A_PAD + 8 * (i - 1):PEER_A_PAD + 8 * i, :] = jnp.where(
                    row8 < n_i, a_all[i:i + 1, :] + b_all[0:8, :], NEG_INF)
            cand_ref[PEER_A_PAD + 56:PEER_A_PAD + 72, :] = jnp.where(
                row16 + 8 < PEER_CAND, a_all[8:24, :] + b_all[0:1, :], NEG_INF)
            top = _extract_top(cand_ref[...], PEER_CAND)
            z = functools.reduce(jnp.add, [jnp.exp(v - top[0]) for v in top[:P_TOPK]])
            thr = 0.5 * (top[P_TOPK - 1] + top[P_TOPK])
            n1 = jnp.zeros((N_KEYS, LANES), F32)
            for i in range(P_TOPK):
                a_i = a_all[i:i + 1, :]
                cnt = jnp.sum(jnp.where(a_i + b_all >= thr, 1.0, 0.0), axis=0, keepdims=True)
                n1 = jnp.where(s[0] == a_i, cnt, n1)
            r2_ref[tc, h] = rank2.astype(BF16)
            e2_ref[tc, h] = (jnp.exp(s[1] - b_all[0:1, :]) / z).astype(BF16)
            n1_ref[tc, h] = n1
            e1_ref[tc, h] = jnp.exp(s[0] - a_all[0:1, :])


def _peerq(x, shift, scale, g, wq, keys, tm):
    B, L, D = x.shape
    T = B * L
    nb = L // tm
    tok_spec = pl.BlockSpec((tm // LANES, P_HEADS, N_KEYS, LANES), lambda b, i: (b * nb + i, 0, 0, 0))
    tok_shape = lambda dt: jax.ShapeDtypeStruct((T // LANES, P_HEADS, N_KEYS, LANES), dt)
    return pl.pallas_call(
        _peerq_kernel,
        grid=(B, nb),
        in_specs=[
            pl.BlockSpec((1, tm, D), lambda b, i: (b, i, 0)),
            pl.BlockSpec((1, 1, D), lambda b, i: (b, 0, 0)),
            pl.BlockSpec((1, 1, D), lambda b, i: (b, 0, 0)),
            pl.BlockSpec((1, D), lambda b, i: (0, 0)),
            pl.BlockSpec((D, 2 * P_HEADS * P_DKH), lambda b, i: (0, 0)),
            pl.BlockSpec((2 * P_HEADS, N_KEYS, P_DKH), lambda b, i: (0, 0, 0)),
        ],
        out_specs=[pl.BlockSpec((tm, D), lambda b, i: (b * nb + i, 0))] + [tok_spec] * 4,
        out_shape=[jax.ShapeDtypeStruct((T, D), BF16), tok_shape(BF16), tok_shape(BF16), tok_shape(F32),
                   tok_shape(F32)],
        scratch_shapes=[pltpu.VMEM((2, PEER_A_PAD, LANES), F32), pltpu.VMEM((PEER_CAND_ROWS, LANES), F32)],
        compiler_params=_cparams(("parallel", "parallel")),
        name="peer_query",
    )(x, shift, scale, g, wq, keys)


PEER_EC = 1024
PEER_TT = 1024
PEER_TSUB = 256
PEER_GATE_ROWS = 64


def _gelu(x):
    return 0.5 * x * (1.0 + lax.erf(x * math.sqrt(0.5)))


def _peer_kernel(xn_ref, u_ref, vt_ref, r2_ref, e2_ref, n1_ref, e1_ref, x_ref, gt_ref, fg_ref,
                 o_ref, acc_ref, act0_ref, act1_ref, pt0_ref, pt1_ref, *, final_norm):
    c = pl.program_id(1)
    act_refs, pt_refs = (act0_ref, act1_ref), (pt0_ref, pt1_ref)

    @pl.when(c == 0)
    def _():
        acc_ref[...] = jnp.zeros_like(acc_ref)

    TT = xn_ref.shape[0]
    tsub = min(PEER_TSUB, TT)
    n_sub = TT // tsub
    n_a = PEER_EC // N_KEYS

    def hidden(j):
        act_refs[j % 2][...] = _gelu(_dot_nt(u_ref[...], xn_ref[j * tsub:(j + 1) * tsub, :])).astype(BF16)

    def gated(j):
        act_ref, pt_ref = act_refs[j % 2], pt_refs[j % 2]
        for tl in range(tsub // LANES):
            tc = j * (tsub // LANES) + tl
            cols = slice(tl * LANES, (tl + 1) * LANES)
            n1s = [[n1_ref[tc, h, a:a + 1, :].astype(BF16) for h in range(P_HEADS)] for a in range(n_a)]
            e1s = [[e1_ref[tc, h, a:a + 1, :].astype(BF16) for h in range(P_HEADS)] for a in range(n_a)]
            for b0 in range(0, N_KEYS, PEER_GATE_ROWS):
                brows = slice(b0, b0 + PEER_GATE_ROWS)
                gates = [None] * n_a
                for h in range(P_HEADS):
                    r2 = r2_ref[tc, h, brows, :]
                    e2 = e2_ref[tc, h, brows, :]
                    for a in range(n_a):
                        t = e1s[a][h] * jnp.minimum(jnp.maximum(n1s[a][h] - r2, 0.0), e2)
                        gates[a] = t if gates[a] is None else gates[a] + t
                for a in range(n_a):
                    rows = slice(a * N_KEYS + b0, a * N_KEYS + b0 + PEER_GATE_ROWS)
                    pt_ref[rows, cols] = gates[a] * act_ref[rows, cols]

    def project(j):
        acc_ref[:, j * tsub:(j + 1) * tsub] += _dot(vt_ref[...], pt_refs[j % 2][...])

    hidden(0)
    for j in range(n_sub):
        if j + 1 < n_sub:
            hidden(j + 1)
        gated(j)
        project(j)

    @pl.when(c == pl.num_programs(1) - 1)
    def _():
        y = x_ref[...] + gt_ref[0] * acc_ref[...].T
        if final_norm:
            y = y * lax.rsqrt(jnp.mean(y * y, axis=-1, keepdims=True) + EPS) * fg_ref[...]
        o_ref[...] = y


def _peer(xn, r2, e2, n1, e1, u, vt, x, gate, final_g, final_norm):
    B, L, D = x.shape
    T = B * L
    TT = min(PEER_TT, L)
    tsub = min(PEER_TSUB, TT)
    per_b = L // TT
    tok_spec = pl.BlockSpec((TT // LANES, P_HEADS, N_KEYS, LANES), lambda i, c: (i, 0, 0, 0))
    key_spec = pl.BlockSpec((TT // LANES, P_HEADS, PEER_EC // N_KEYS, LANES), lambda i, c: (i, 0, c, 0))
    out = pl.pallas_call(
        functools.partial(_peer_kernel, final_norm=final_norm),
        grid=(T // TT, N_EXPERTS // PEER_EC),
        in_specs=[
            pl.BlockSpec((TT, D), lambda i, c: (i, 0)),
            pl.BlockSpec((PEER_EC, D), lambda i, c: (c, 0)),
            pl.BlockSpec((D, PEER_EC), lambda i, c: (0, c)),
            tok_spec, tok_spec, key_spec, key_spec,
            pl.BlockSpec((TT, D), lambda i, c: (i, 0)),
            pl.BlockSpec((1, 1, D), lambda i, c: (i // per_b, 0, 0)),
            pl.BlockSpec((1, D), lambda i, c: (0, 0)),
        ],
        out_specs=pl.BlockSpec((TT, D), lambda i, c: (i, 0)),
        out_shape=jax.ShapeDtypeStruct((T, D), F32),
        scratch_shapes=[pltpu.VMEM((D, TT), F32),
                        pltpu.VMEM((PEER_EC, tsub), BF16), pltpu.VMEM((PEER_EC, tsub), BF16),
                        pltpu.VMEM((PEER_EC, tsub), BF16), pltpu.VMEM((PEER_EC, tsub), BF16)],
        compiler_params=_cparams(("parallel", "arbitrary")),
        name="peer_experts",
    )(xn, u, vt, r2, e2, n1, e1, x.reshape(T, D), gate, final_g)
    return out.reshape(B, L, D)


GDN_CONV_TILE = 512
GDN_PAD = 8


def _gdn_prep_kernel(q_ref, k_ref, v_ref, cwq_ref, cwk_ref, cwv_ref, qo_ref, ko_ref, vo_ref, pad_ref):
    Ls = q_ref.shape[1]
    T = min(GDN_CONV_TILE, Ls)
    zeros = jnp.zeros((GDN_PAD, LANES), F32)
    pad_ref[0:GDN_PAD, :] = zeros
    pad_ref[GDN_PAD + Ls:2 * GDN_PAD + Ls, :] = zeros
    for which, (x_ref, cw_ref, o_ref) in enumerate(((q_ref, cwq_ref, qo_ref), (k_ref, cwk_ref, ko_ref),
                                                    (v_ref, cwv_ref, vo_ref))):
        pad_ref[GDN_PAD:GDN_PAD + Ls, :] = x_ref[0]
        for t0 in range(0, Ls, T):
            acc = None
            for j in range(CONV_K):
                off = GDN_PAD - CONV_K // 2 + j + t0
                t = cw_ref[j:j + 1, :] * pad_ref[off:off + T, :]
                acc = t if acc is None else acc + t
            y = acc * jax.nn.sigmoid(acc)
            if which < 2:
                y = y * lax.rsqrt(jnp.sum(y * y, axis=-1, keepdims=True) + EPS)
            if which == 0:
                y = y * (A_DK ** -0.5)
            o_ref[0, t0:t0 + T, :] = y


def _gdn_prep(proj, conv_w):
    B, Ls, _ = proj.shape
    seq = lambda col: pl.BlockSpec((1, Ls, LANES), lambda b, h: (b, 0, col + h))
    cw = lambda col: pl.BlockSpec((CONV_K, LANES), lambda b, h: (0, col + h))
    out = pl.BlockSpec((1, Ls, LANES), lambda b, h: (b, 0, h))
    shape = jax.ShapeDtypeStruct((B, Ls, A_HEADS * LANES), F32)
    return pl.pallas_call(
        _gdn_prep_kernel,
        grid=(B, A_HEADS),
        in_specs=[seq(COL_A_Q), seq(COL_A_K), seq(COL_A_V), cw(0), cw(A_HEADS), cw(2 * A_HEADS)],
        out_specs=[out] * 3,
        out_shape=[shape] * 3,
        scratch_shapes=[pltpu.VMEM((Ls + 2 * GDN_PAD, LANES), F32)],
        compiler_params=_cparams(("parallel", "parallel")),
        name="gdn_prep",
    )(proj, proj, proj, conv_w, conv_w, conv_w)


def _softplus(x):
    return jnp.maximum(x, 0.0) + jnp.log1p(jnp.exp(-jnp.abs(x)))


GDN_GROUP = 8
GDN_INV_BLOCK = 16


def _gdn_prepass_group(h, chunks, nega_ref, dtb_ref, scr):
    mneg_scr, c_scr, qp_scr, dl_scr, o_scr = scr
    C = CHUNK
    row = lax.broadcasted_iota(jnp.int32, (C, C), 0)
    col = lax.broadcasted_iota(jnp.int32, (C, C), 1)
    lane = lax.broadcasted_iota(jnp.int32, (1, LANES), 1)
    eye = row == col
    chains = []
    for q, k, v, ba, row0, chunk_id in chunks:
        kb16 = k.astype(BF16)
        qk_raw = _dot_nt(q.astype(BF16), kb16)
        for d in range(2):
            before_eq_rc = (col <= row) if d == 0 else (col >= row)
            before_rc = (col < row) if d == 0 else (col > row)
            before_eq_cr = (row <= col) if d == 0 else (row >= col)
            bcol = jnp.sum(jnp.where(lane == d * A_HEADS + h, ba, 0.0), axis=-1, keepdims=True)
            acol = jnp.sum(jnp.where(lane == (2 + d) * A_HEADS + h, ba, 0.0), axis=-1, keepdims=True)
            beta = jax.nn.sigmoid(bcol)
            g = nega_ref[d, h] * _softplus(acol + dtb_ref[d, h])
            g_cols = jnp.broadcast_to(g, (C, C))
            gc_row = jnp.sum(jnp.where(before_eq_cr, g_cols, 0.0), axis=0, keepdims=True)
            g_row = jnp.sum(jnp.where(eye, g_cols, 0.0), axis=0, keepdims=True)
            gc_col = jnp.sum(jnp.where(before_eq_rc, jnp.broadcast_to(g_row, (C, C)), 0.0), axis=-1, keepdims=True)
            g_total = jnp.sum(g_row, axis=-1, keepdims=True)
            decay = jnp.exp(jnp.where(before_eq_rc, gc_col - gc_row, NEG_INF))
            kbeta = k * beta
            e_gc = jnp.exp(gc_col)
            chains.append(dict(
                d=d, row0=row0, chunk_id=chunk_id,
                n_pow=-jnp.where(before_rc, _dot_nt(kbeta.astype(BF16), kb16) * decay, 0.0),
                qk=jnp.where(before_eq_rc, qk_raw * decay, 0.0).astype(BF16),
                rhs=jnp.concatenate([v * beta, kbeta * e_gc], axis=-1).astype(BF16),
                kd=(k * jnp.exp(g_total - gc_col)).astype(BF16),
                qd=q * e_gc,
                dl=jnp.exp(g_total)))
    eye_f = jnp.where(eye, 1.0, 0.0)
    same_blk = (row // GDN_INV_BLOCK) == (col // GDN_INV_BLOCK)
    n_d = [jnp.where(same_blk, ch['n_pow'], 0.0) for ch in chains]
    a_o = [jnp.where(same_blk, 0.0, -ch['n_pow']) for ch in chains]
    d_inv = [eye_f + n for n in n_d]
    for _ in range(3):
        n_d = [_dot_split(n, n) for n in n_d]
        d_inv = [dd + _dot_split(dd, n) for dd, n in zip(d_inv, n_d)]
    ms = [_dot_split(dd, a) for dd, a in zip(d_inv, a_o)]
    m2 = [_dot_split(m, m) for m in ms]
    x1 = [(eye_f - m) + _dot_split(eye_f - m, mm) for m, mm in zip(ms, m2)]
    invs = [_dot_split(x, dd) for x, dd in zip(x1, d_inv)]
    uws = [_dot(inv.astype(BF16), ch['rhs']).astype(BF16) for inv, ch in zip(invs, chains)]
    kts = [_dot_tn(ch['kd'], uw) for ch, uw in zip(chains, uws)]
    qqs = [_dot(ch['qk'], uw) for ch, uw in zip(chains, uws)]
    for ch, kt, qq in zip(chains, kts, qqs):
        d, cid = ch['d'], ch['chunk_id']
        rows = pl.ds(ch['row0'], C)
        c_scr[d, cid] = kt[:, :A_DV]
        mneg_scr[d, cid] = (-kt[:, A_DV:]).astype(BF16)
        o_scr[d, rows, :] = qq[:, :A_DV]
        qp_scr[d, rows, :] = (ch['qd'] - qq[:, A_DV:]).astype(BF16)
        dl_scr[d, pl.ds(cid * 8, 8), :] = jnp.broadcast_to(ch['dl'], (8, LANES))


def _gdn_scan_step(states, scr, row0s, chunk_ids):
    mneg_scr, c_scr, qp_scr, dl_scr, o_scr = scr
    s16 = [S.astype(BF16) for S in states]
    upd = [_dot(mneg_scr[d, chunk_ids[d]], s16[d]) for d in range(2)]
    for d in range(2):
        rows = pl.ds(row0s[d], CHUNK)
        o_scr[d, rows, :] += _dot(qp_scr[d, rows, :], s16[d])
    return tuple(states[d] * dl_scr[d, pl.ds(chunk_ids[d] * 8, 1), :] + c_scr[d, chunk_ids[d]] + upd[d]
                 for d in range(2))


def _gdn_scan_kernel(nega_ref, dtb_ref, qc_ref, kc_ref, vc_ref, bac_ref, zc_ref, ql_ref, kl_ref, vl_ref, bal_ref,
                     zl_ref, ng_ref, yl_ref, yc_ref, mneg_scr, c_scr, qp_scr, dl_scr, o_scr):
    h = pl.program_id(1)
    Lc, L = qc_ref.shape[1], ql_ref.shape[1]
    nc, nl = Lc // CHUNK, L // CHUNK
    scr = (mneg_scr, c_scr, qp_scr, dl_scr, o_scr)

    for c0 in range(0, nc, GDN_GROUP):
        _gdn_prepass_group(h, [(qc_ref[0, c * CHUNK:(c + 1) * CHUNK, :], kc_ref[0, c * CHUNK:(c + 1) * CHUNK, :],
                                vc_ref[0, c * CHUNK:(c + 1) * CHUNK, :], bac_ref[0, c * CHUNK:(c + 1) * CHUNK, :],
                                c * CHUNK, c) for c in range(c0, min(c0 + GDN_GROUP, nc))],
                           nega_ref, dtb_ref, scr)

    def pre_body(grp, carry):
        chunks = []
        for j in range(GDN_GROUP):
            c = grp * GDN_GROUP + j
            r = pl.multiple_of(c * CHUNK, CHUNK)
            rows = pl.ds(r, CHUNK)
            chunks.append((ql_ref[0, rows, :], kl_ref[0, rows, :], vl_ref[0, rows, :], bal_ref[0, rows, :],
                           Lc + r, nc + c))
        _gdn_prepass_group(h, chunks, nega_ref, dtb_ref, scr)
        return carry

    lax.fori_loop(0, nl // GDN_GROUP, pre_body, 0)

    states = (jnp.zeros((A_DK, A_DV), F32), jnp.zeros((A_DK, A_DV), F32))
    for s in range(nc):
        states = _gdn_scan_step(states, scr, (s * CHUNK, (nc - 1 - s) * CHUNK), (s, nc - 1 - s))

    def scan_body(s, states):
        cf = s
        cb = nl - 1 - s
        return _gdn_scan_step(states, scr,
                              (pl.multiple_of(Lc + cf * CHUNK, CHUNK), pl.multiple_of(Lc + cb * CHUNK, CHUNK)),
                              (nc + cf, nc + cb))

    lax.fori_loop(0, nl, scan_body, states)

    def finish(z_ref, y_ref, base, n):
        T = min(GDN_CONV_TILE, n)
        for t0 in range(0, n, T):
            o = o_scr[0, base + t0:base + t0 + T, :] + o_scr[1, base + t0:base + t0 + T, :]
            y = o * lax.rsqrt(jnp.mean(o * o, axis=-1, keepdims=True) + EPS) * ng_ref[...]
            z = z_ref[0, t0:t0 + T, :]
            y_ref[0, t0:t0 + T, :] = y * (z * jax.nn.sigmoid(z))

    finish(zl_ref, yl_ref, Lc, L)
    finish(zc_ref, yc_ref, 0, Lc)


def _gdn_scan(nega, dtb, qkv_ctx, ba_ctx, proj_ctx, qkv_lat, ba_lat, proj_lat, norm_g):
    B, L, _ = proj_lat.shape
    Lc = proj_ctx.shape[1]
    Lt = L + Lc
    head = lambda n: pl.BlockSpec((1, n, LANES), lambda b, h: (b, 0, h))
    full = lambda n: pl.BlockSpec((1, n, LANES), lambda b, h: (b, 0, 0))
    zcol = lambda n: pl.BlockSpec((1, n, LANES), lambda b, h: (b, 0, COL_A_Z + h))
    smem = pl.BlockSpec(memory_space=pltpu.SMEM)
    return pl.pallas_call(
        _gdn_scan_kernel,
        grid=(B, A_HEADS),
        in_specs=[smem, smem,
                  head(Lc), head(Lc), head(Lc), full(Lc), zcol(Lc),
                  head(L), head(L), head(L), full(L), zcol(L),
                  pl.BlockSpec((1, LANES), lambda b, h: (0, 0))],
        out_specs=[head(L), head(Lc)],
        out_shape=[jax.ShapeDtypeStruct((B, L, A_HEADS * LANES), F32),
                   jax.ShapeDtypeStruct((B, Lc, A_HEADS * LANES), F32)],
        scratch_shapes=[
            pltpu.VMEM((2, Lt // CHUNK, A_DK, A_DV), BF16),
            pltpu.VMEM((2, Lt // CHUNK, A_DK, A_DV), F32),
            pltpu.VMEM((2, Lt, A_DK), BF16),
            pltpu.VMEM((2, Lt // CHUNK * 8, LANES), F32),
            pltpu.VMEM((2, Lt, A_DV), F32),
        ],
        compiler_params=_cparams(("parallel", "parallel")),
        name="gdn_scan",
    )(nega, dtb, *qkv_ctx, ba_ctx, proj_ctx, *qkv_lat, ba_lat, proj_lat, norm_g)


def _gdn_pallas(proj_lat, ba_lat, proj_ctx, ba_ctx, conv_w, a_log, dt_bias, norm_g):
    qkv_lat = _gdn_prep(proj_lat, conv_w)
    qkv_ctx = _gdn_prep(proj_ctx, conv_w)
    return _gdn_scan(-jnp.exp(a_log), dt_bias, qkv_ctx, ba_ctx, proj_ctx, qkv_lat, ba_lat, proj_lat, norm_g[None, :])


def _layer(i, x, xc, c, c_ctx, p, cos_t, sin_t, ctx_out, final_g, final_norm):
    B, L, D = x.shape
    Lc = xc.shape[1]
    n_rows = -(-(B + 1) // SUBLANES) * SUBLANES
    cond = jnp.concatenate([c, c_ctx[None, :], jnp.zeros((n_rows - B - 1, D), F32)], axis=0)
    mod = _adaln(cond, p['ada_w'], p['ada_b'])
    mod_lat = mod[:B, None, :]
    mod_ctx = jnp.broadcast_to(mod[B][None, None, :], (B, 1, 6 * D))
    sh1, sc1, gt1, sh2, sc2, gt2 = jnp.split(mod_lat, 6, axis=-1)
    csh1, csc1, cgt1, csh2, csc2, cgt2 = jnp.split(mod_ctx, 6, axis=-1)
    lam_init = 0.8 - 0.6 * math.exp(-0.3 * i)

    w_in = p['w_in']
    n_a = 4 * A_HEADS * A_DK
    w_main = jnp.concatenate([w_in[:, :n_a], w_in[:, n_a + 4 * A_HEADS:]], axis=1).astype(BF16)
    w_ba = jnp.pad(w_in[:, n_a:n_a + 4 * A_HEADS], ((0, 0), (0, LANES - 4 * A_HEADS))).astype(BF16)
    n1 = p['norm1_g'][None, :]
    proj_lat, ba_lat = _inproj(x, sh1, sc1, n1, w_main, w_ba, min(INPROJ_ROWS, L))
    proj_ctx, ba_ctx = _inproj(xc, csh1, csc1, n1, w_main, w_ba, Lc)

    ya, ya_c = _gdn_pallas(proj_lat, ba_lat, proj_ctx, ba_ctx, p['gdn_conv'], p['gdn_a_log'], p['gdn_dt_bias'],
                           p['gdn_norm_g'])

    lp = p['diff_lambda']
    q_r, k_r = _rope(proj_lat, cos_t, sin_t, min(ROPE_ROWS, L))
    subln = p['diff_subln_g'][None, :]
    yb = _diff_lat(lp, q_r, k_r, proj_lat, proj_ctx, subln, lam_init, min(DIFF_Q_TILE, L))
    yb_c = _diff_ctx(lp, proj_ctx, subln, lam_init) if ctx_out else None

    yc = _natten_lat(proj_lat, proj_ctx, _natten_bias(p['na_rpb']))
    yc_c = _natten_ctx(proj_ctx) if ctx_out else None

    w_up = p['w_up'].astype(BF16)
    w_out = p['w_out'].astype(BF16)
    wq = p['peer_wq'].astype(BF16)
    keys = p['peer_keys'].reshape(2 * P_HEADS, N_KEYS, P_DKH).astype(BF16)
    u = p['peer_u'].astype(BF16)
    vt = p['peer_v'].T.astype(BF16)
    n2 = p['norm2_g'][None, :]
    fg = final_g[None, :]

    x = _merge(ya, yb, yc, proj_lat, w_up, w_out, x, gt1, min(MERGE_ROWS, L))
    pq = _peerq(x, sh2, sc2, n2, wq, keys, min(PEERQ_ROWS, L))
    x = _peer(*pq, u, vt, x, gt2, fg, final_norm)
    if ctx_out:
        xc = _merge(ya_c, yb_c, yc_c, proj_ctx, w_up, w_out, xc, cgt1, Lc)
        pq = _peerq(xc, csh2, csc2, n2, wq, keys, Lc)
        xc = _peer(*pq, u, vt, xc, cgt2, fg, False)
    return x, xc


def kernel(x, c, ctx, c_ctx, norm1_g, norm2_g, ada_w, ada_b, w_in, gdn_conv, gdn_a_log, gdn_dt_bias, gdn_norm_g,
           diff_lambda, diff_subln_g, na_rpb, w_up, w_out, peer_wq, peer_keys, peer_u, peer_v, final_g):
    cos_t, sin_t = _rope_tables(x.shape[1])
    xc = ctx
    for i in range(DEPTH):
        p = dict(norm1_g=norm1_g[i], norm2_g=norm2_g[i], ada_w=ada_w[i], ada_b=ada_b[i], w_in=w_in[i],
                 gdn_conv=gdn_conv[i], gdn_a_log=gdn_a_log[i], gdn_dt_bias=gdn_dt_bias[i], gdn_norm_g=gdn_norm_g[i],
                 diff_lambda=diff_lambda[i], diff_subln_g=diff_subln_g[i], na_rpb=na_rpb[i], w_up=w_up[i],
                 w_out=w_out[i], peer_wq=peer_wq[i], peer_keys=peer_keys[i], peer_u=peer_u[i], peer_v=peer_v[i])
        x, xc = _layer(i, x, xc, c, c_ctx, p, cos_t, sin_t, i < DEPTH - 1, final_g, i == DEPTH - 1)
    return x
```

```python
import functools
import math

import numpy as np
import jax
import jax.numpy as jnp
from jax import lax
from jax.experimental import pallas as pl
from jax.experimental.pallas import tpu as pltpu

F32 = jnp.float32
BF16 = jnp.bfloat16

D_MODEL = 1024
DEPTH = 2
GRID_W = 64
EPS = 1e-6
NEG_INF = -1e30

A_HEADS = 4
A_DK = 128
A_DV = 128
CONV_K = 5
CHUNK = 64
B_HEADS = 4
B_DH = 64
ROPE_BASE = 10000.0
C_HEADS = 8
C_DH = 64
WIN_R = 8
WIN_C = 16
N_BRANCH = 3
BRANCH_W = 512
P_HEADS = 8
N_KEYS = 128
N_EXPERTS = N_KEYS * N_KEYS
P_DKH = 128
P_TOPK = 16

LANES = 128
VMEM_LIMIT = 56 * 1024 * 1024

COL_A_Q, COL_A_K, COL_A_V, COL_A_Z = 0, 4, 8, 12
COL_B_Q, COL_B_K, COL_B_V = 16, 20, 24
COL_C_Q, COL_C_K, COL_C_V = 28, 32, 36
COL_GATE = 40
MAIN_COLS = 64 * LANES


def _cparams(sem):
    return pltpu.CompilerParams(dimension_semantics=sem, vmem_limit_bytes=VMEM_LIMIT)


def _dot(a, b):
    return jnp.dot(a, b, preferred_element_type=F32)


def _dot_split(a, b):
    ah = a.astype(BF16)
    al = (a - ah.astype(F32)).astype(BF16)
    bh = b.astype(BF16)
    bl = (b - bh.astype(F32)).astype(BF16)
    return _dot(ah, bh) + (_dot(ah, bl) + _dot(al, bh))


def _dot_nt(a, b):
    return lax.dot_general(a, b, (((1,), (1,)), ((), ())), preferred_element_type=F32)


def _dot_tn(a, b):
    return lax.dot_general(a, b, (((0,), (0,)), ((), ())), preferred_element_type=F32)


INPROJ_ROWS = 1024
ROPE_ROWS = 512
DIFF_Q_TILE = 512
MERGE_ROWS = 512
PEERQ_ROWS = 256
ADALN_COLS = 1024
SUBLANES = 8


def _adaln_kernel(c_ref, w_ref, b_ref, o_ref):
    c = c_ref[...]
    h = (c * jax.nn.sigmoid(c)).astype(BF16)
    o_ref[...] = _dot(h, w_ref[...].astype(BF16)) + b_ref[...]


def _adaln(cond, ada_w, ada_b):
    R, D = cond.shape
    N = ada_w.shape[1]
    return pl.pallas_call(
        _adaln_kernel,
        grid=(N // ADALN_COLS,),
        in_specs=[
            pl.BlockSpec((R, D), lambda j: (0, 0)),
            pl.BlockSpec((D, ADALN_COLS), lambda j: (0, j)),
            pl.BlockSpec((1, ADALN_COLS), lambda j: (0, j)),
        ],
        out_specs=pl.BlockSpec((R, ADALN_COLS), lambda j: (0, j)),
        out_shape=jax.ShapeDtypeStruct((R, N), F32),
        compiler_params=_cparams(("parallel",)),
        name="adaln",
    )(cond, ada_w, ada_b[None, :])


def _inproj_kernel(x_ref, sh_ref, sc_ref, g_ref, w_ref, wba_ref, o_ref, oba_ref, hn_ref):
    @pl.when(pl.program_id(2) == 0)
    def _():
        x = x_ref[0]
        y = x * lax.rsqrt(jnp.mean(x * x, axis=-1, keepdims=True) + EPS) * g_ref[...]
        h = (y * (1.0 + sc_ref[0]) + sh_ref[0]).astype(BF16)
        hn_ref[...] = h
        oba_ref[0] = _dot(h, wba_ref[...])

    o_ref[0] = _dot(hn_ref[...], w_ref[...])


def _inproj(x, shift, scale, g, w_main, w_ba, tm):
    B, L, D = x.shape
    tn = 1024
    return pl.pallas_call(
        _inproj_kernel,
        grid=(B, L // tm, MAIN_COLS // tn),
        in_specs=[
            pl.BlockSpec((1, tm, D), lambda b, i, j: (b, i, 0)),
            pl.BlockSpec((1, 1, D), lambda b, i, j: (b, 0, 0)),
            pl.BlockSpec((1, 1, D), lambda b, i, j: (b, 0, 0)),
            pl.BlockSpec((1, D), lambda b, i, j: (0, 0)),
            pl.BlockSpec((D, tn), lambda b, i, j: (0, j)),
            pl.BlockSpec((D, LANES), lambda b, i, j: (0, 0)),
        ],
        out_specs=[
            pl.BlockSpec((1, tm, tn), lambda b, i, j: (b, i, j)),
            pl.BlockSpec((1, tm, LANES), lambda b, i, j: (b, i, 0)),
        ],
        out_shape=[
            jax.ShapeDtypeStruct((B, L, MAIN_COLS), F32),
            jax.ShapeDtypeStruct((B, L, LANES), F32),
        ],
        scratch_shapes=[pltpu.VMEM((tm, D), BF16)],
        compiler_params=_cparams(("parallel", "parallel", "arbitrary")),
        name="inproj",
    )(x, shift, scale, g, w_main, w_ba)


def _rope_tables(L):
    t = np.arange(L)
    row_pos, col_pos = t // GRID_W, t % GRID_W
    lane = np.arange(LANES)
    axis = (lane % 64) // 32
    f = lane % 16
    inv = 1.0 / (ROPE_BASE ** (f.astype(np.float32) / 16.0))
    pos = np.where(axis[None, :] == 0, row_pos[:, None], col_pos[:, None]).astype(np.float32)
    ang = jnp.asarray(pos) * jnp.asarray(inv.astype(np.float32))[None, :]
    first = jnp.asarray(((lane % 32) < 16)[None, :])
    return jnp.cos(ang), jnp.where(first, -jnp.sin(ang), jnp.sin(ang))


def _rope_kernel(q_ref, k_ref, cos_ref, sin_ref, qo_ref, ko_ref):
    lane = lax.broadcasted_iota(jnp.int32, (1, LANES), 1)
    first = (lane % 32) < 16
    c, s = cos_ref[...], sin_ref[...]

    def rope(x):
        partner = jnp.where(first, pltpu.roll(x, LANES - 16, 1), pltpu.roll(x, 16, 1))
        return x * c + partner * s

    for h in range(B_HEADS):
        sl = slice(h * LANES, (h + 1) * LANES)
        qo_ref[0, :, sl] = (rope(q_ref[0, :, sl]) * (B_DH ** -0.5)).astype(BF16)
        ko_ref[0, :, sl] = rope(k_ref[0, :, sl]).astype(BF16)


def _rope(proj, cos_t, sin_t, tr):
    B, L, _ = proj.shape
    W = B_HEADS * LANES
    return pl.pallas_call(
        _rope_kernel,
        grid=(B, L // tr),
        in_specs=[
            pl.BlockSpec((1, tr, W), lambda b, i: (b, i, COL_B_Q * LANES // W)),
            pl.BlockSpec((1, tr, W), lambda b, i: (b, i, COL_B_K * LANES // W)),
            pl.BlockSpec((tr, LANES), lambda b, i: (i, 0)),
            pl.BlockSpec((tr, LANES), lambda b, i: (i, 0)),
        ],
        out_specs=[pl.BlockSpec((1, tr, W), lambda b, i: (b, i, 0))] * 2,
        out_shape=[jax.ShapeDtypeStruct((B, L, W), BF16)] * 2,
        compiler_params=_cparams(("parallel", "parallel")),
        name="rope",
    )(proj, proj, cos_t, sin_t)


def _softmax_pv(q, key_vals):
    scores = [_dot_nt(q, k) for k, _ in key_vals]
    m = functools.reduce(jnp.maximum, [s.max(axis=-1, keepdims=True) for s in scores])
    es = [jnp.exp(s - m) for s in scores]
    denom = functools.reduce(jnp.add, [e.sum(axis=-1, keepdims=True) for e in es])
    o = functools.reduce(jnp.add, [_dot(e.astype(BF16), v) for e, (_, v) in zip(es, key_vals)])
    return o / denom


DIFF_Q_ROWS = 256


def _diff_finish(o0, o1, lp, g, lam_init):
    lam = (jnp.exp(jnp.sum(lp[0:1] * lp[1:2], axis=-1, keepdims=True))
           - jnp.exp(jnp.sum(lp[2:3] * lp[3:4], axis=-1, keepdims=True)) + lam_init)
    o = o0 - lam * o1
    return o * lax.rsqrt(jnp.mean(o * o, axis=-1, keepdims=True) + EPS) * g * (1.0 - lam_init)


def _diff_lat_kernel(lam_ref, q_ref, kl_ref, vl_ref, kc_ref, vc_ref, g_ref, o_ref, *, lam_init):
    lane = lax.broadcasted_iota(jnp.int32, (1, LANES), 1)
    low = lane < B_DH
    tq = q_ref.shape[1]
    kl, vl = kl_ref[0], vl_ref[0].astype(BF16)
    kc, vc = kc_ref[0].astype(BF16), vc_ref[0].astype(BF16)
    units = [dict(r0=r0, mp=mp) for r0 in range(0, tq, DIFF_Q_ROWS) for mp in range(2)]

    def scores(un):
        q = q_ref[0, un['r0']:un['r0'] + DIFF_Q_ROWS, :]
        q = (jnp.where(low, q, 0) if un['mp'] == 0 else jnp.where(low, 0, q)).astype(BF16)
        un['s'] = (_dot_nt(q, kl), _dot_nt(q, kc))

    def softmax(un):
        s_l, s_c = un.pop('s')
        m = jnp.maximum(s_l.max(axis=-1, keepdims=True), s_c.max(axis=-1, keepdims=True))
        e_l, e_c = jnp.exp(s_l - m), jnp.exp(s_c - m)
        un['denom'] = e_l.sum(axis=-1, keepdims=True) + e_c.sum(axis=-1, keepdims=True)
        un['e'] = (e_l.astype(BF16), e_c.astype(BF16))

    def values(un):
        e_l, e_c = un.pop('e')
        un['o'] = (_dot(e_l, vl) + _dot(e_c, vc)) / un['denom']

    scores(units[0])
    scores(units[1])
    for k, un in enumerate(units):
        softmax(un)
        if k + 2 < len(units):
            scores(units[k + 2])
        values(un)
    for k in range(0, len(units), 2):
        r0 = units[k]['r0']
        o_ref[0, r0:r0 + DIFF_Q_ROWS, :] = _diff_finish(units[k]['o'], units[k + 1]['o'], lam_ref[...], g_ref[...],
                                                        lam_init)


def _diff_lat(lam, q_r, k_r, proj_lat, proj_ctx, subln_g, lam_init, tq):
    B, L, _ = proj_lat.shape
    Lc = proj_ctx.shape[1]
    return pl.pallas_call(
        functools.partial(_diff_lat_kernel, lam_init=lam_init),
        grid=(B, B_HEADS, L // tq),
        in_specs=[
            pl.BlockSpec((4, B_DH), lambda b, h, i: (0, 0)),
            pl.BlockSpec((1, tq, LANES), lambda b, h, i: (b, i, h)),
            pl.BlockSpec((1, L, LANES), lambda b, h, i: (b, 0, h)),
            pl.BlockSpec((1, L, LANES), lambda b, h, i: (b, 0, COL_B_V + h)),
            pl.BlockSpec((1, Lc, LANES), lambda b, h, i: (b, 0, COL_B_K + h)),
            pl.BlockSpec((1, Lc, LANES), lambda b, h, i: (b, 0, COL_B_V + h)),
            pl.BlockSpec((1, LANES), lambda b, h, i: (0, 0)),
        ],
        out_specs=pl.BlockSpec((1, tq, LANES), lambda b, h, i: (b, i, h)),
        out_shape=jax.ShapeDtypeStruct((B, L, B_HEADS * LANES), F32),
        compiler_params=_cparams(("parallel", "parallel", "parallel")),
        name="diff_lat",
    )(lam, q_r, k_r, proj_lat, proj_ctx, proj_ctx, subln_g)


def _diff_ctx_kernel(lam_ref, q_ref, k_ref, v_ref, g_ref, o_ref, *, lam_init):
    lane = lax.broadcasted_iota(jnp.int32, (1, LANES), 1)
    low = lane < B_DH
    q = q_ref[0] * (B_DH ** -0.5)
    kv = [(k_ref[0].astype(BF16), v_ref[0].astype(BF16))]
    o0 = _softmax_pv(jnp.where(low, q, 0).astype(BF16), kv)
    o1 = _softmax_pv(jnp.where(low, 0, q).astype(BF16), kv)
    o_ref[0] = _diff_finish(o0, o1, lam_ref[...], g_ref[...], lam_init)


def _diff_ctx(lam, proj_ctx, subln_g, lam_init):
    B, Lc, _ = proj_ctx.shape
    return pl.pallas_call(
        functools.partial(_diff_ctx_kernel, lam_init=lam_init),
        grid=(B, B_HEADS),
        in_specs=[
            pl.BlockSpec((4, B_DH), lambda b, h: (0, 0)),
            pl.BlockSpec((1, Lc, LANES), lambda b, h: (b, 0, COL_B_Q + h)),
            pl.BlockSpec((1, Lc, LANES), lambda b, h: (b, 0, COL_B_K + h)),
            pl.BlockSpec((1, Lc, LANES), lambda b, h: (b, 0, COL_B_V + h)),
            pl.BlockSpec((1, LANES), lambda b, h: (0, 0)),
        ],
        out_specs=pl.BlockSpec((1, Lc, LANES), lambda b, h: (b, 0, h)),
        out_shape=jax.ShapeDtypeStruct((B, Lc, B_HEADS * LANES), F32),
        compiler_params=_cparams(("parallel", "parallel")),
        name="diff_ctx",
    )(lam, proj_ctx, proj_ctx, proj_ctx, subln_g)


NA_ROWS_PER_STEP = 8
NA_KEYS = WIN_R * GRID_W


def _natten_bias(rpb):
    col = np.arange(GRID_W)
    col_start = np.clip(col - WIN_C // 2, 0, GRID_W - WIN_C)
    col_mask = (col[None, :] >= col_start[:, None]) & (col[None, :] < col_start[:, None] + WIN_C)
    dc = np.clip(col[None, :] - col[:, None], -(WIN_C - 1), WIN_C - 1) + WIN_C - 1
    bias = jnp.where(jnp.asarray(col_mask), rpb.astype(F32)[:, :, dc], NEG_INF)
    return jnp.concatenate([bias[:, :-1], bias[:, 1:]], axis=-1)


def _natten_kernel(q_ref, k_ref, v_ref, kc_ref, vc_ref, bias_ref, o_ref, *, rows):
    lane = lax.broadcasted_iota(jnp.int32, (1, LANES), 1)
    low = lane < C_DH
    kc = kc_ref[0].astype(BF16)
    vc = vc_ref[0].astype(BF16)
    units = []
    for rr in range(NA_ROWS_PER_STEP):
        r = pl.program_id(2) * NA_ROWS_PER_STEP + rr
        rs = jnp.clip(r - WIN_R // 2, 0, rows - WIN_R)
        cfg = r - rs
        start = pl.multiple_of(rs * GRID_W, GRID_W)
        q = q_ref[0, rr * GRID_W:(rr + 1) * GRID_W, :] * (C_DH ** -0.5)
        q2 = jnp.concatenate([jnp.where(low, q, 0), jnp.where(low, 0, q)], axis=0).astype(BF16)
        bias = jnp.concatenate(
            [jnp.concatenate([bias_ref[hh, WIN_R - 1 - cfg + j] for j in range(0, WIN_R, 2)], axis=-1)
             for hh in range(2)], axis=0)
        units.append(dict(q=q2, bias=bias, start=start))
    for un in units:
        kw = k_ref[0, pl.ds(un['start'], NA_KEYS), :].astype(BF16)
        un['s_lat'] = _dot_nt(un['q'], kw) + un['bias']
        un['s_ctx'] = _dot_nt(un['q'], kc)
    for un in units:
        s_lat, s_ctx = un['s_lat'], un['s_ctx']
        m = jnp.maximum(s_lat.max(axis=-1, keepdims=True), s_ctx.max(axis=-1, keepdims=True))
        e_lat, e_ctx = jnp.exp(s_lat - m), jnp.exp(s_ctx - m)
        un['denom'] = e_lat.sum(axis=-1, keepdims=True) + e_ctx.sum(axis=-1, keepdims=True)
        un['e_lat'], un['e_ctx'] = e_lat.astype(BF16), e_ctx.astype(BF16)
    for rr, un in enumerate(units):
        vw = v_ref[0, pl.ds(un['start'], NA_KEYS), :].astype(BF16)
        o = (_dot(un['e_lat'], vw) + _dot(un['e_ctx'], vc)) / un['denom']
        o_ref[0, rr * GRID_W:(rr + 1) * GRID_W, :] = jnp.where(low, o[:GRID_W], o[GRID_W:])


def _natten_lat(proj_lat, proj_ctx, bias):
    B, L, _ = proj_lat.shape
    Lc = proj_ctx.shape[1]
    rows = L // GRID_W
    HP = C_HEADS // 2
    tq = NA_ROWS_PER_STEP * GRID_W
    return pl.pallas_call(
        functools.partial(_natten_kernel, rows=rows),
        grid=(B, HP, rows // NA_ROWS_PER_STEP),
        in_specs=[
            pl.BlockSpec((1, tq, LANES), lambda b, h, i: (b, i, COL_C_Q + h)),
            pl.BlockSpec((1, L, LANES), lambda b, h, i: (b, 0, COL_C_K + h)),
            pl.BlockSpec((1, L, LANES), lambda b, h, i: (b, 0, COL_C_V + h)),
            pl.BlockSpec((1, Lc, LANES), lambda b, h, i: (b, 0, COL_C_K + h)),
            pl.BlockSpec((1, Lc, LANES), lambda b, h, i: (b, 0, COL_C_V + h)),
            pl.BlockSpec((2, 2 * WIN_R - 2, GRID_W, 2 * GRID_W), lambda b, h, i: (h, 0, 0, 0)),
        ],
        out_specs=pl.BlockSpec((1, tq, LANES), lambda b, h, i: (b, i, h)),
        out_shape=jax.ShapeDtypeStruct((B, L, HP * LANES), F32),
        compiler_params=_cparams(("parallel", "parallel", "parallel")),
        name="natten_lat",
    )(proj_lat, proj_lat, proj_lat, proj_ctx, proj_ctx, bias)


def _natten_ctx_kernel(q_ref, k_ref, v_ref, o_ref):
    lane = lax.broadcasted_iota(jnp.int32, (1, LANES), 1)
    low = lane < C_DH
    q = q_ref[0] * (C_DH ** -0.5)
    kv = [(k_ref[0].astype(BF16), v_ref[0].astype(BF16))]
    o0 = _softmax_pv(jnp.where(low, q, 0).astype(BF16), kv)
    o1 = _softmax_pv(jnp.where(low, 0, q).astype(BF16), kv)
    o_ref[0] = jnp.where(low, o0, o1)


def _natten_ctx(proj_ctx):
    B, Lc, _ = proj_ctx.shape
    HP = C_HEADS // 2
    return pl.pallas_call(
        _natten_ctx_kernel,
        grid=(B, HP),
        in_specs=[
            pl.BlockSpec((1, Lc, LANES), lambda b, h: (b, 0, COL_C_Q + h)),
            pl.BlockSpec((1, Lc, LANES), lambda b, h: (b, 0, COL_C_K + h)),
            pl.BlockSpec((1, Lc, LANES), lambda b, h: (b, 0, COL_C_V + h)),
        ],
        out_specs=pl.BlockSpec((1, Lc, LANES), lambda b, h: (b, 0, h)),
        out_shape=jax.ShapeDtypeStruct((B, Lc, HP * LANES), F32),
        compiler_params=_cparams(("parallel", "parallel")),
        name="natten_ctx",
    )(proj_ctx, proj_ctx, proj_ctx)


def _merge_kernel(ya_ref, yb_ref, yc_ref, g0_ref, g1_ref, g2_ref, wup_ref, wout_ref, x_ref, gt_ref, o_ref):
    acc = None
    for n, (y_ref, g_ref) in enumerate(((ya_ref, g0_ref), (yb_ref, g1_ref), (yc_ref, g2_ref))):
        up = _dot(y_ref[0].astype(BF16), wup_ref[n])
        t = jax.nn.sigmoid(g_ref[0]) * up
        acc = t if acc is None else acc + t
    r = _dot(acc.astype(BF16), wout_ref[...])
    o_ref[0] = x_ref[0] + gt_ref[0] * r


def _merge(ya, yb, yc, proj, w_up, w_out, x, gate, tm):
    B, L, D = x.shape
    gcol = COL_GATE * LANES // D
    yspec = pl.BlockSpec((1, tm, BRANCH_W), lambda b, i: (b, i, 0))
    return pl.pallas_call(
        _merge_kernel,
        grid=(B, L // tm),
        in_specs=[
            yspec, yspec, yspec,
            pl.BlockSpec((1, tm, D), lambda b, i: (b, i, gcol)),
            pl.BlockSpec((1, tm, D), lambda b, i: (b, i, gcol + 1)),
            pl.BlockSpec((1, tm, D), lambda b, i: (b, i, gcol + 2)),
            pl.BlockSpec((N_BRANCH, BRANCH_W, D), lambda b, i: (0, 0, 0)),
            pl.BlockSpec((D, D), lambda b, i: (0, 0)),
            pl.BlockSpec((1, tm, D), lambda b, i: (b, i, 0)),
            pl.BlockSpec((1, 1, D), lambda b, i: (b, 0, 0)),
        ],
        out_specs=pl.BlockSpec((1, tm, D), lambda b, i: (b, i, 0)),
        out_shape=jax.ShapeDtypeStruct((B, L, D), F32),
        compiler_params=_cparams(("parallel", "parallel")),
        name="merge",
    )(ya, yb, yc, proj, proj, proj, w_up, w_out, x, gate)


PEER_CAND = P_TOPK
PEER_A_PAD = 16
PEER_CAND_ROWS = PEER_A_PAD + 7 * 8 + 8


PEER_NO_RANK = float(N_KEYS)


def _extract_top(s, n, with_rank=False):
    vals = []
    rank = jnp.full(s.shape, PEER_NO_RANK, F32) if with_rank else None
    for r in range(n):
        m = jnp.max(s, axis=0, keepdims=True)
        vals.append(m)
        hit = s == m
        if with_rank:
            rank = jnp.where(hit, float(r), rank)
        s = jnp.where(hit, NEG_INF, s)
    return (vals, rank) if with_rank else vals


def _peerq_kernel(x_ref, sh_ref, sc_ref, g_ref, wq_ref, keys_ref,
                  xn_ref, r2_ref, e2_ref, n1_ref, e1_ref, ab_ref, cand_ref):
    x = x_ref[0]
    tm = x.shape[0]
    y = x * lax.rsqrt(jnp.mean(x * x, axis=-1, keepdims=True) + EPS) * g_ref[...]
    xn = (y * (1.0 + sc_ref[0]) + sh_ref[0]).astype(BF16)
    xn_ref[...] = xn
    q = _dot(xn, wq_ref[...]).astype(BF16)
    row8 = lax.broadcasted_iota(jnp.int32, (8, 1), 0)
    for h in range(P_HEADS):
        for tc in range(tm // LANES):
            qt = q[tc * LANES:(tc + 1) * LANES, :]
            s = []
            for p in range(2):
                hp = 2 * h + p
                st = _dot_nt(keys_ref[hp], qt[:, hp * P_DKH:(hp + 1) * P_DKH])
                s.append(st)
                if p == 0:
                    vals = _extract_top(st, PEER_CAND)
                else:
                    vals, rank2 = _extract_top(st, PEER_CAND, with_rank=True)
                for r, m in enumerate(vals):
                    ab_ref[p, r:r + 1, :] = m
            a_all, b_all = ab_ref[0], ab_ref[1]
            cand_ref[0:PEER_A_PAD, :] = a_all[0:1, :] + b_all
            for i in range(1, 8):
                n_i = PEER_CAND // (i + 1)
                cand_ref[PEER_A_PAD + 8 * (i - 1):PEER_A_PAD + 8 * i, :] = jnp.where(
                    row8 < n_i, a_all[i:i + 1, :] + b_all[0:8, :], NEG_INF)
            cand_ref[PEER_A_PAD + 56:PEER_A_PAD + 64, :] = a_all[8:16, :] + b_all[0:1, :]
            top = _extract_top(cand_ref[...], PEER_CAND)
            z = functools.reduce(jnp.add, [jnp.exp(v - top[0]) for v in top])
            thr = top[P_TOPK - 1]
            n1 = jnp.zeros((N_KEYS, LANES), F32)
            for i in range(P_TOPK):
                a_i = a_all[i:i + 1, :]
                cnt = jnp.sum(jnp.where(a_i + b_all >= thr, 1.0, 0.0), axis=0, keepdims=True)
                n1 = jnp.where(s[0] == a_i, cnt, n1)
            r2_ref[tc, h] = rank2.astype(BF16)
            e2_ref[tc, h] = (jnp.exp(s[1] - b_all[0:1, :]) / z).astype(BF16)
            n1_ref[tc, h] = n1
            e1_ref[tc, h] = jnp.exp(s[0] - a_all[0:1, :])


def _peerq(x, shift, scale, g, wq, keys, tm):
    B, L, D = x.shape
    T = B * L
    nb = L // tm
    tok_spec = pl.BlockSpec((tm // LANES, P_HEADS, N_KEYS, LANES), lambda b, i: (b * nb + i, 0, 0, 0))
    tok_shape = lambda dt: jax.ShapeDtypeStruct((T // LANES, P_HEADS, N_KEYS, LANES), dt)
    return pl.pallas_call(
        _peerq_kernel,
        grid=(B, nb),
        in_specs=[
            pl.BlockSpec((1, tm, D), lambda b, i: (b, i, 0)),
            pl.BlockSpec((1, 1, D), lambda b, i: (b, 0, 0)),
            pl.BlockSpec((1, 1, D), lambda b, i: (b, 0, 0)),
            pl.BlockSpec((1, D), lambda b, i: (0, 0)),
            pl.BlockSpec((D, 2 * P_HEADS * P_DKH), lambda b, i: (0, 0)),
            pl.BlockSpec((2 * P_HEADS, N_KEYS, P_DKH), lambda b, i: (0, 0, 0)),
        ],
        out_specs=[pl.BlockSpec((tm, D), lambda b, i: (b * nb + i, 0))] + [tok_spec] * 4,
        out_shape=[jax.ShapeDtypeStruct((T, D), BF16), tok_shape(BF16), tok_shape(BF16), tok_shape(F32),
                   tok_shape(F32)],
        scratch_shapes=[pltpu.VMEM((2, PEER_A_PAD, LANES), F32), pltpu.VMEM((PEER_CAND_ROWS, LANES), F32)],
        compiler_params=_cparams(("parallel", "parallel")),
        name="peer_query",
    )(x, shift, scale, g, wq, keys)


PEER_EC = 1024
PEER_TT = 1024
PEER_TSUB = 256
PEER_GATE_ROWS = 64


def _gelu(x):
    return 0.5 * x * (1.0 + lax.erf(x * math.sqrt(0.5)))


def _peer_kernel(xn_ref, u_ref, vt_ref, r2_ref, e2_ref, n1_ref, e1_ref, x_ref, gt_ref, fg_ref,
                 o_ref, acc_ref, act0_ref, act1_ref, pt0_ref, pt1_ref, *, final_norm):
    c = pl.program_id(1)
    act_refs, pt_refs = (act0_ref, act1_ref), (pt0_ref, pt1_ref)

    @pl.when(c == 0)
    def _():
        acc_ref[...] = jnp.zeros_like(acc_ref)

    TT = xn_ref.shape[0]
    tsub = min(PEER_TSUB, TT)
    n_sub = TT // tsub
    n_a = PEER_EC // N_KEYS

    def hidden(j):
        act_refs[j % 2][...] = _gelu(_dot_nt(u_ref[...], xn_ref[j * tsub:(j + 1) * tsub, :])).astype(BF16)

    def gated(j):
        act_ref, pt_ref = act_refs[j % 2], pt_refs[j % 2]
        for tl in range(tsub // LANES):
            tc = j * (tsub // LANES) + tl
            cols = slice(tl * LANES, (tl + 1) * LANES)
            n1s = [[n1_ref[tc, h, a:a + 1, :].astype(BF16) for h in range(P_HEADS)] for a in range(n_a)]
            e1s = [[e1_ref[tc, h, a:a + 1, :].astype(BF16) for h in range(P_HEADS)] for a in range(n_a)]
            for b0 in range(0, N_KEYS, PEER_GATE_ROWS):
                brows = slice(b0, b0 + PEER_GATE_ROWS)
                gates = [None] * n_a
                for h in range(P_HEADS):
                    r2 = r2_ref[tc, h, brows, :]
                    e2 = e2_ref[tc, h, brows, :]
                    for a in range(n_a):
                        t = e1s[a][h] * jnp.minimum(jnp.maximum(n1s[a][h] - r2, 0.0), e2)
                        gates[a] = t if gates[a] is None else gates[a] + t
                for a in range(n_a):
                    rows = slice(a * N_KEYS + b0, a * N_KEYS + b0 + PEER_GATE_ROWS)
                    pt_ref[rows, cols] = gates[a] * act_ref[rows, cols]

    def project(j):
        acc_ref[:, j * tsub:(j + 1) * tsub] += _dot(vt_ref[...], pt_refs[j % 2][...])

    hidden(0)
    for j in range(n_sub):
        if j + 1 < n_sub:
            hidden(j + 1)
        gated(j)
        project(j)

    @pl.when(c == pl.num_programs(1) - 1)
    def _():
        y = x_ref[...] + gt_ref[0] * acc_ref[...].T
        if final_norm:
            y = y * lax.rsqrt(jnp.mean(y * y, axis=-1, keepdims=True) + EPS) * fg_ref[...]
        o_ref[...] = y


def _peer(xn, r2, e2, n1, e1, u, vt, x, gate, final_g, final_norm):
    B, L, D = x.shape
    T = B * L
    TT = min(PEER_TT, L)
    tsub = min(PEER_TSUB, TT)
    per_b = L // TT
    tok_spec = pl.BlockSpec((TT // LANES, P_HEADS, N_KEYS, LANES), lambda i, c: (i, 0, 0, 0))
    key_spec = pl.BlockSpec((TT // LANES, P_HEADS, PEER_EC // N_KEYS, LANES), lambda i, c: (i, 0, c, 0))
    out = pl.pallas_call(
        functools.partial(_peer_kernel, final_norm=final_norm),
        grid=(T // TT, N_EXPERTS // PEER_EC),
        in_specs=[
            pl.BlockSpec((TT, D), lambda i, c: (i, 0)),
            pl.BlockSpec((PEER_EC, D), lambda i, c: (c, 0)),
            pl.BlockSpec((D, PEER_EC), lambda i, c: (0, c)),
            tok_spec, tok_spec, key_spec, key_spec,
            pl.BlockSpec((TT, D), lambda i, c: (i, 0)),
            pl.BlockSpec((1, 1, D), lambda i, c: (i // per_b, 0, 0)),
            pl.BlockSpec((1, D), lambda i, c: (0, 0)),
        ],
        out_specs=pl.BlockSpec((TT, D), lambda i, c: (i, 0)),
        out_shape=jax.ShapeDtypeStruct((T, D), F32),
        scratch_shapes=[pltpu.VMEM((D, TT), F32),
                        pltpu.VMEM((PEER_EC, tsub), BF16), pltpu.VMEM((PEER_EC, tsub), BF16),
                        pltpu.VMEM((PEER_EC, tsub), BF16), pltpu.VMEM((PEER_EC, tsub), BF16)],
        compiler_params=_cparams(("parallel", "arbitrary")),
        name="peer_experts",
    )(xn, u, vt, r2, e2, n1, e1, x.reshape(T, D), gate, final_g)
    return out.reshape(B, L, D)


GDN_CONV_TILE = 512
GDN_PAD = 8


def _gdn_prep_kernel(q_ref, k_ref, v_ref, cwq_ref, cwk_ref, cwv_ref, qo_ref, ko_ref, vo_ref, pad_ref):
    Ls = q_ref.shape[1]
    T = min(GDN_CONV_TILE, Ls)
    zeros = jnp.zeros((GDN_PAD, LANES), F32)
    pad_ref[0:GDN_PAD, :] = zeros
    pad_ref[GDN_PAD + Ls:2 * GDN_PAD + Ls, :] = zeros
    for which, (x_ref, cw_ref, o_ref) in enumerate(((q_ref, cwq_ref, qo_ref), (k_ref, cwk_ref, ko_ref),
                                                    (v_ref, cwv_ref, vo_ref))):
        pad_ref[GDN_PAD:GDN_PAD + Ls, :] = x_ref[0]
        for t0 in range(0, Ls, T):
            acc = None
            for j in range(CONV_K):
                off = GDN_PAD - CONV_K // 2 + j + t0
                t = cw_ref[j:j + 1, :] * pad_ref[off:off + T, :]
                acc = t if acc is None else acc + t
            y = acc * jax.nn.sigmoid(acc)
            if which < 2:
                y = y * lax.rsqrt(jnp.sum(y * y, axis=-1, keepdims=True) + EPS)
            if which == 0:
                y = y * (A_DK ** -0.5)
            o_ref[0, t0:t0 + T, :] = y


def _gdn_prep(proj, conv_w):
    B, Ls, _ = proj.shape
    seq = lambda col: pl.BlockSpec((1, Ls, LANES), lambda b, h: (b, 0, col + h))
    cw = lambda col: pl.BlockSpec((CONV_K, LANES), lambda b, h: (0, col + h))
    out = pl.BlockSpec((1, Ls, LANES), lambda b, h: (b, 0, h))
    shape = jax.ShapeDtypeStruct((B, Ls, A_HEADS * LANES), F32)
    return pl.pallas_call(
        _gdn_prep_kernel,
        grid=(B, A_HEADS),
        in_specs=[seq(COL_A_Q), seq(COL_A_K), seq(COL_A_V), cw(0), cw(A_HEADS), cw(2 * A_HEADS)],
        out_specs=[out] * 3,
        out_shape=[shape] * 3,
        scratch_shapes=[pltpu.VMEM((Ls + 2 * GDN_PAD, LANES), F32)],
        compiler_params=_cparams(("parallel", "parallel")),
        name="gdn_prep",
    )(proj, proj, proj, conv_w, conv_w, conv_w)


def _softplus(x):
    return jnp.maximum(x, 0.0) + jnp.log1p(jnp.exp(-jnp.abs(x)))


GDN_GROUP = 8
GDN_INV_BLOCK = 16


def _gdn_prepass_group(h, chunks, nega_ref, dtb_ref, scr):
    mneg_scr, c_scr, qp_scr, dl_scr, o_scr = scr
    C = CHUNK
    row = lax.broadcasted_iota(jnp.int32, (C, C), 0)
    col = lax.broadcasted_iota(jnp.int32, (C, C), 1)
    lane = lax.broadcasted_iota(jnp.int32, (1, LANES), 1)
    eye = row == col
    chains = []
    for q, k, v, ba, row0, chunk_id in chunks:
        kb16 = k.astype(BF16)
        qk_raw = _dot_nt(q.astype(BF16), kb16)
        for d in range(2):
            before_eq_rc = (col <= row) if d == 0 else (col >= row)
            before_rc = (col < row) if d == 0 else (col > row)
            before_eq_cr = (row <= col) if d == 0 else (row >= col)
            bcol = jnp.sum(jnp.where(lane == d * A_HEADS + h, ba, 0.0), axis=-1, keepdims=True)
            acol = jnp.sum(jnp.where(lane == (2 + d) * A_HEADS + h, ba, 0.0), axis=-1, keepdims=True)
            beta = jax.nn.sigmoid(bcol)
            g = nega_ref[d, h] * _softplus(acol + dtb_ref[d, h])
            g_cols = jnp.broadcast_to(g, (C, C))
            gc_row = jnp.sum(jnp.where(before_eq_cr, g_cols, 0.0), axis=0, keepdims=True)
            g_row = jnp.sum(jnp.where(eye, g_cols, 0.0), axis=0, keepdims=True)
            gc_col = jnp.sum(jnp.where(before_eq_rc, jnp.broadcast_to(g_row, (C, C)), 0.0), axis=-1, keepdims=True)
            g_total = jnp.sum(g_row, axis=-1, keepdims=True)
            decay = jnp.exp(jnp.where(before_eq_rc, gc_col - gc_row, NEG_INF))
            kbeta = k * beta
            e_gc = jnp.exp(gc_col)
            chains.append(dict(
                d=d, row0=row0, chunk_id=chunk_id,
                n_pow=-jnp.where(before_rc, _dot_nt(kbeta.astype(BF16), kb16) * decay, 0.0),
                qk=jnp.where(before_eq_rc, qk_raw * decay, 0.0).astype(BF16),
                rhs=jnp.concatenate([v * beta, kbeta * e_gc], axis=-1).astype(BF16),
                kd=(k * jnp.exp(g_total - gc_col)).astype(BF16),
                qd=q * e_gc,
                dl=jnp.exp(g_total)))
    eye_f = jnp.where(eye, 1.0, 0.0)
    same_blk = (row // GDN_INV_BLOCK) == (col // GDN_INV_BLOCK)
    n_d = [jnp.where(same_blk, ch['n_pow'], 0.0) for ch in chains]
    a_o = [jnp.where(same_blk, 0.0, -ch['n_pow']) for ch in chains]
    d_inv = [eye_f + n for n in n_d]
    for _ in range(3):
        n_d = [_dot_split(n, n) for n in n_d]
        d_inv = [dd + _dot_split(dd, n) for dd, n in zip(d_inv, n_d)]
    b16 = lambda t: t.astype(BF16)
    ms = [_dot(b16(dd), b16(a)) for dd, a in zip(d_inv, a_o)]
    m2 = [_dot(b16(m), b16(m)) for m in ms]
    x1 = [(eye_f - m) + _dot(b16(eye_f - m), b16(mm)) for m, mm in zip(ms, m2)]
    invs = [_dot(b16(x), b16(dd)) for x, dd in zip(x1, d_inv)]
    uws = [_dot(inv.astype(BF16), ch['rhs']).astype(BF16) for inv, ch in zip(invs, chains)]
    kts = [_dot_tn(ch['kd'], uw) for ch, uw in zip(chains, uws)]
    qqs = [_dot(ch['qk'], uw) for ch, uw in zip(chains, uws)]
    for ch, kt, qq in zip(chains, kts, qqs):
        d, cid = ch['d'], ch['chunk_id']
        rows = pl.ds(ch['row0'], C)
        c_scr[d, cid] = kt[:, :A_DV]
        mneg_scr[d, cid] = (-kt[:, A_DV:]).astype(BF16)
        o_scr[d, rows, :] = qq[:, :A_DV]
        qp_scr[d, rows, :] = (ch['qd'] - qq[:, A_DV:]).astype(BF16)
        dl_scr[d, pl.ds(cid * 8, 8), :] = jnp.broadcast_to(ch['dl'], (8, LANES))


def _gdn_scan_step(states, scr, row0s, chunk_ids):
    mneg_scr, c_scr, qp_scr, dl_scr, o_scr = scr
    s16 = [S.astype(BF16) for S in states]
    upd = [_dot(mneg_scr[d, chunk_ids[d]], s16[d]) for d in range(2)]
    for d in range(2):
        rows = pl.ds(row0s[d], CHUNK)
        o_scr[d, rows, :] += _dot(qp_scr[d, rows, :], s16[d])
    return tuple(states[d] * dl_scr[d, pl.ds(chunk_ids[d] * 8, 1), :] + c_scr[d, chunk_ids[d]] + upd[d]
                 for d in range(2))


def _gdn_scan_kernel(nega_ref, dtb_ref, qc_ref, kc_ref, vc_ref, bac_ref, zc_ref, ql_ref, kl_ref, vl_ref, bal_ref,
                     zl_ref, ng_ref, yl_ref, yc_ref, mneg_scr, c_scr, qp_scr, dl_scr, o_scr):
    h = pl.program_id(1)
    Lc, L = qc_ref.shape[1], ql_ref.shape[1]
    nc, nl = Lc // CHUNK, L // CHUNK
    scr = (mneg_scr, c_scr, qp_scr, dl_scr, o_scr)

    for c0 in range(0, nc, GDN_GROUP):
        _gdn_prepass_group(h, [(qc_ref[0, c * CHUNK:(c + 1) * CHUNK, :], kc_ref[0, c * CHUNK:(c + 1) * CHUNK, :],
                                vc_ref[0, c * CHUNK:(c + 1) * CHUNK, :], bac_ref[0, c * CHUNK:(c + 1) * CHUNK, :],
                                c * CHUNK, c) for c in range(c0, min(c0 + GDN_GROUP, nc))],
                           nega_ref, dtb_ref, scr)

    def pre_body(grp, carry):
        chunks = []
        for j in range(GDN_GROUP):
            c = grp * GDN_GROUP + j
            r = pl.multiple_of(c * CHUNK, CHUNK)
            rows = pl.ds(r, CHUNK)
            chunks.append((ql_ref[0, rows, :], kl_ref[0, rows, :], vl_ref[0, rows, :], bal_ref[0, rows, :],
                           Lc + r, nc + c))
        _gdn_prepass_group(h, chunks, nega_ref, dtb_ref, scr)
        return carry

    lax.fori_loop(0, nl // GDN_GROUP, pre_body, 0)

    states = (jnp.zeros((A_DK, A_DV), F32), jnp.zeros((A_DK, A_DV), F32))
    for s in range(nc):
        states = _gdn_scan_step(states, scr, (s * CHUNK, (nc - 1 - s) * CHUNK), (s, nc - 1 - s))

    def scan_body(s, states):
        cf = s
        cb = nl - 1 - s
        return _gdn_scan_step(states, scr,
                              (pl.multiple_of(Lc + cf * CHUNK, CHUNK), pl.multiple_of(Lc + cb * CHUNK, CHUNK)),
                              (nc + cf, nc + cb))

    lax.fori_loop(0, nl, scan_body, states)

    def finish(z_ref, y_ref, base, n):
        T = min(GDN_CONV_TILE, n)
        for t0 in range(0, n, T):
            o = o_scr[0, base + t0:base + t0 + T, :] + o_scr[1, base + t0:base + t0 + T, :]
            y = o * lax.rsqrt(jnp.mean(o * o, axis=-1, keepdims=True) + EPS) * ng_ref[...]
            z = z_ref[0, t0:t0 + T, :]
            y_ref[0, t0:t0 + T, :] = y * (z * jax.nn.sigmoid(z))

    finish(zl_ref, yl_ref, Lc, L)
    finish(zc_ref, yc_ref, 0, Lc)


def _gdn_scan(nega, dtb, qkv_ctx, ba_ctx, proj_ctx, qkv_lat, ba_lat, proj_lat, norm_g):
    B, L, _ = proj_lat.shape
    Lc = proj_ctx.shape[1]
    Lt = L + Lc
    head = lambda n: pl.BlockSpec((1, n, LANES), lambda b, h: (b, 0, h))
    full = lambda n: pl.BlockSpec((1, n, LANES), lambda b, h: (b, 0, 0))
    zcol = lambda n: pl.BlockSpec((1, n, LANES), lambda b, h: (b, 0, COL_A_Z + h))
    smem = pl.BlockSpec(memory_space=pltpu.SMEM)
    return pl.pallas_call(
        _gdn_scan_kernel,
        grid=(B, A_HEADS),
        in_specs=[smem, smem,
                  head(Lc), head(Lc), head(Lc), full(Lc), zcol(Lc),
                  head(L), head(L), head(L), full(L), zcol(L),
                  pl.BlockSpec((1, LANES), lambda b, h: (0, 0))],
        out_specs=[head(L), head(Lc)],
        out_shape=[jax.ShapeDtypeStruct((B, L, A_HEADS * LANES), F32),
                   jax.ShapeDtypeStruct((B, Lc, A_HEADS * LANES), F32)],
        scratch_shapes=[
            pltpu.VMEM((2, Lt // CHUNK, A_DK, A_DV), BF16),
            pltpu.VMEM((2, Lt // CHUNK, A_DK, A_DV), F32),
            pltpu.VMEM((2, Lt, A_DK), BF16),
            pltpu.VMEM((2, Lt // CHUNK * 8, LANES), F32),
            pltpu.VMEM((2, Lt, A_DV), F32),
        ],
        compiler_params=_cparams(("parallel", "parallel")),
        name="gdn_scan",
    )(nega, dtb, *qkv_ctx, ba_ctx, proj_ctx, *qkv_lat, ba_lat, proj_lat, norm_g)


def _gdn_pallas(proj_lat, ba_lat, proj_ctx, ba_ctx, conv_w, a_log, dt_bias, norm_g):
    qkv_lat = _gdn_prep(proj_lat, conv_w)
    qkv_ctx = _gdn_prep(proj_ctx, conv_w)
    return _gdn_scan(-jnp.exp(a_log), dt_bias, qkv_ctx, ba_ctx, proj_ctx, qkv_lat, ba_lat, proj_lat, norm_g[None, :])


def _layer(i, x, xc, c, c_ctx, p, cos_t, sin_t, ctx_out, final_g, final_norm):
    B, L, D = x.shape
    Lc = xc.shape[1]
    n_rows = -(-(B + 1) // SUBLANES) * SUBLANES
    cond = jnp.concatenate([c, c_ctx[None, :], jnp.zeros((n_rows - B - 1, D), F32)], axis=0)
    mod = _adaln(cond, p['ada_w'], p['ada_b'])
    mod_lat = mod[:B, None, :]
    mod_ctx = jnp.broadcast_to(mod[B][None, None, :], (B, 1, 6 * D))
    sh1, sc1, gt1, sh2, sc2, gt2 = jnp.split(mod_lat, 6, axis=-1)
    csh1, csc1, cgt1, csh2, csc2, cgt2 = jnp.split(mod_ctx, 6, axis=-1)
    lam_init = 0.8 - 0.6 * math.exp(-0.3 * i)

    w_in = p['w_in']
    n_a = 4 * A_HEADS * A_DK
    w_main = jnp.concatenate([w_in[:, :n_a], w_in[:, n_a + 4 * A_HEADS:]], axis=1).astype(BF16)
    w_ba = jnp.pad(w_in[:, n_a:n_a + 4 * A_HEADS], ((0, 0), (0, LANES - 4 * A_HEADS))).astype(BF16)
    n1 = p['norm1_g'][None, :]
    proj_lat, ba_lat = _inproj(x, sh1, sc1, n1, w_main, w_ba, min(INPROJ_ROWS, L))
    proj_ctx, ba_ctx = _inproj(xc, csh1, csc1, n1, w_main, w_ba, Lc)

    ya, ya_c = _gdn_pallas(proj_lat, ba_lat, proj_ctx, ba_ctx, p['gdn_conv'], p['gdn_a_log'], p['gdn_dt_bias'],
                           p['gdn_norm_g'])

    lp = p['diff_lambda']
    q_r, k_r = _rope(proj_lat, cos_t, sin_t, min(ROPE_ROWS, L))
    subln = p['diff_subln_g'][None, :]
    yb = _diff_lat(lp, q_r, k_r, proj_lat, proj_ctx, subln, lam_init, min(DIFF_Q_TILE, L))
    yb_c = _diff_ctx(lp, proj_ctx, subln, lam_init) if ctx_out else None

    yc = _natten_lat(proj_lat, proj_ctx, _natten_bias(p['na_rpb']))
    yc_c = _natten_ctx(proj_ctx) if ctx_out else None

    w_up = p['w_up'].astype(BF16)
    w_out = p['w_out'].astype(BF16)
    wq = p['peer_wq'].astype(BF16)
    keys = p['peer_keys'].reshape(2 * P_HEADS, N_KEYS, P_DKH).astype(BF16)
    u = p['peer_u'].astype(BF16)
    vt = p['peer_v'].T.astype(BF16)
    n2 = p['norm2_g'][None, :]
    fg = final_g[None, :]

    x = _merge(ya, yb, yc, proj_lat, w_up, w_out, x, gt1, min(MERGE_ROWS, L))
    pq = _peerq(x, sh2, sc2, n2, wq, keys, min(PEERQ_ROWS, L))
    x = _peer(*pq, u, vt, x, gt2, fg, final_norm)
    if ctx_out:
        xc = _merge(ya_c, yb_c, yc_c, proj_ctx, w_up, w_out, xc, cgt1, Lc)
        pq = _peerq(xc, csh2, csc2, n2, wq, keys, Lc)
        xc = _peer(*pq, u, vt, xc, cgt2, fg, False)
    return x, xc


def kernel(x, c, ctx, c_ctx, norm1_g, norm2_g, ada_w, ada_b, w_in, gdn_conv, gdn_a_log, gdn_dt_bias, gdn_norm_g,
           diff_lambda, diff_subln_g, na_rpb, w_up, w_out, peer_wq, peer_keys, peer_u, peer_v, final_g):
    cos_t, sin_t = _rope_tables(x.shape[1])
    xc = ctx
    for i in range(DEPTH):
        p = dict(norm1_g=norm1_g[i], norm2_g=norm2_g[i], ada_w=ada_w[i], ada_b=ada_b[i], w_in=w_in[i],
                 gdn_conv=gdn_conv[i], gdn_a_log=gdn_a_log[i], gdn_dt_bias=gdn_dt_bias[i], gdn_norm_g=gdn_norm_g[i],
                 diff_lambda=diff_lambda[i], diff_subln_g=diff_subln_g[i], na_rpb=na_rpb[i], w_up=w_up[i],
                 w_out=w_out[i], peer_wq=peer_wq[i], peer_keys=peer_keys[i], peer_u=peer_u[i], peer_v=peer_v[i])
        x, xc = _layer(i, x, xc, c, c_ctx, p, cos_t, sin_t, i < DEPTH - 1, final_g, i == DEPTH - 1)
    return x
```

```python
import functools
import math

import numpy as np
import jax
import jax.numpy as jnp
from jax import lax
from jax.experimental import pallas as pl
from jax.experimental.pallas import tpu as pltpu

F32 = jnp.float32
BF16 = jnp.bfloat16

D_MODEL = 1024
DEPTH = 2
GRID_W = 64
EPS = 1e-6
NEG_INF = -1e30

A_HEADS = 4
A_DK = 128
A_DV = 128
CONV_K = 5
CHUNK = 64
B_HEADS = 4
B_DH = 64
ROPE_BASE = 10000.0
C_HEADS = 8
C_DH = 64
WIN_R = 8
WIN_C = 16
N_BRANCH = 3
BRANCH_W = 512
P_HEADS = 8
N_KEYS = 128
N_EXPERTS = N_KEYS * N_KEYS
P_DKH = 128
P_TOPK = 16

LANES = 128
VMEM_LIMIT = 56 * 1024 * 1024

COL_A_Q, COL_A_K, COL_A_V, COL_A_Z = 0, 4, 8, 12
COL_B_Q, COL_B_K, COL_B_V = 16, 20, 24
COL_C_Q, COL_C_K, COL_C_V = 28, 32, 36
COL_GATE = 40
MAIN_COLS = 64 * LANES


def _cparams(sem):
    return pltpu.CompilerParams(dimension_semantics=sem, vmem_limit_bytes=VMEM_LIMIT)


def _dot(a, b):
    return jnp.dot(a, b, preferred_element_type=F32)


def _dot_split(a, b):
    ah = a.astype(BF16)
    al = (a - ah.astype(F32)).astype(BF16)
    bh = b.astype(BF16)
    bl = (b - bh.astype(F32)).astype(BF16)
    return _dot(ah, bh) + (_dot(ah, bl) + _dot(al, bh))


def _dot_nt(a, b):
    return lax.dot_general(a, b, (((1,), (1,)), ((), ())), preferred_element_type=F32)


def _dot_tn(a, b):
    return lax.dot_general(a, b, (((0,), (0,)), ((), ())), preferred_element_type=F32)


INPROJ_ROWS = 1024
ROPE_ROWS = 512
DIFF_Q_TILE = 512
MERGE_ROWS = 512
PEERQ_ROWS = 256
ADALN_COLS = 1024
SUBLANES = 8


def _adaln_kernel(c_ref, w_ref, b_ref, o_ref):
    c = c_ref[...]
    h = (c * jax.nn.sigmoid(c)).astype(BF16)
    o_ref[...] = _dot(h, w_ref[...].astype(BF16)) + b_ref[...]


def _adaln(cond, ada_w, ada_b):
    R, D = cond.shape
    N = ada_w.shape[1]
    return pl.pallas_call(
        _adaln_kernel,
        grid=(N // ADALN_COLS,),
        in_specs=[
            pl.BlockSpec((R, D), lambda j: (0, 0)),
            pl.BlockSpec((D, ADALN_COLS), lambda j: (0, j)),
            pl.BlockSpec((1, ADALN_COLS), lambda j: (0, j)),
        ],
        out_specs=pl.BlockSpec((R, ADALN_COLS), lambda j: (0, j)),
        out_shape=jax.ShapeDtypeStruct((R, N), F32),
        compiler_params=_cparams(("parallel",)),
        name="adaln",
    )(cond, ada_w, ada_b[None, :])


def _inproj_kernel(x_ref, sh_ref, sc_ref, g_ref, w_ref, wba_ref, o_ref, oba_ref, hn_ref):
    @pl.when(pl.program_id(2) == 0)
    def _():
        x = x_ref[0]
        y = x * lax.rsqrt(jnp.mean(x * x, axis=-1, keepdims=True) + EPS) * g_ref[...]
        h = (y * (1.0 + sc_ref[0]) + sh_ref[0]).astype(BF16)
        hn_ref[...] = h
        oba_ref[0] = _dot(h, wba_ref[...])

    o_ref[0] = _dot(hn_ref[...], w_ref[...])


def _inproj(x, shift, scale, g, w_main, w_ba, tm):
    B, L, D = x.shape
    tn = 1024
    return pl.pallas_call(
        _inproj_kernel,
        grid=(B, L // tm, MAIN_COLS // tn),
        in_specs=[
            pl.BlockSpec((1, tm, D), lambda b, i, j: (b, i, 0)),
            pl.BlockSpec((1, 1, D), lambda b, i, j: (b, 0, 0)),
            pl.BlockSpec((1, 1, D), lambda b, i, j: (b, 0, 0)),
            pl.BlockSpec((1, D), lambda b, i, j: (0, 0)),
            pl.BlockSpec((D, tn), lambda b, i, j: (0, j)),
            pl.BlockSpec((D, LANES), lambda b, i, j: (0, 0)),
        ],
        out_specs=[
            pl.BlockSpec((1, tm, tn), lambda b, i, j: (b, i, j)),
            pl.BlockSpec((1, tm, LANES), lambda b, i, j: (b, i, 0)),
        ],
        out_shape=[
            jax.ShapeDtypeStruct((B, L, MAIN_COLS), F32),
            jax.ShapeDtypeStruct((B, L, LANES), F32),
        ],
        scratch_shapes=[pltpu.VMEM((tm, D), BF16)],
        compiler_params=_cparams(("parallel", "parallel", "arbitrary")),
        name="inproj",
    )(x, shift, scale, g, w_main, w_ba)


def _rope_tables(L):
    t = np.arange(L)
    row_pos, col_pos = t // GRID_W, t % GRID_W
    lane = np.arange(LANES)
    axis = (lane % 64) // 32
    f = lane % 16
    inv = 1.0 / (ROPE_BASE ** (f.astype(np.float32) / 16.0))
    pos = np.where(axis[None, :] == 0, row_pos[:, None], col_pos[:, None]).astype(np.float32)
    ang = jnp.asarray(pos) * jnp.asarray(inv.astype(np.float32))[None, :]
    first = jnp.asarray(((lane % 32) < 16)[None, :])
    return jnp.cos(ang), jnp.where(first, -jnp.sin(ang), jnp.sin(ang))


def _rope_kernel(q_ref, k_ref, cos_ref, sin_ref, qo_ref, ko_ref):
    lane = lax.broadcasted_iota(jnp.int32, (1, LANES), 1)
    first = (lane % 32) < 16
    c, s = cos_ref[...], sin_ref[...]

    def rope(x):
        partner = jnp.where(first, pltpu.roll(x, LANES - 16, 1), pltpu.roll(x, 16, 1))
        return x * c + partner * s

    for h in range(B_HEADS):
        sl = slice(h * LANES, (h + 1) * LANES)
        qo_ref[0, :, sl] = (rope(q_ref[0, :, sl]) * (B_DH ** -0.5)).astype(BF16)
        ko_ref[0, :, sl] = rope(k_ref[0, :, sl]).astype(BF16)


def _rope(proj, cos_t, sin_t, tr):
    B, L, _ = proj.shape
    W = B_HEADS * LANES
    return pl.pallas_call(
        _rope_kernel,
        grid=(B, L // tr),
        in_specs=[
            pl.BlockSpec((1, tr, W), lambda b, i: (b, i, COL_B_Q * LANES // W)),
            pl.BlockSpec((1, tr, W), lambda b, i: (b, i, COL_B_K * LANES // W)),
            pl.BlockSpec((tr, LANES), lambda b, i: (i, 0)),
            pl.BlockSpec((tr, LANES), lambda b, i: (i, 0)),
        ],
        out_specs=[pl.BlockSpec((1, tr, W), lambda b, i: (b, i, 0))] * 2,
        out_shape=[jax.ShapeDtypeStruct((B, L, W), BF16)] * 2,
        compiler_params=_cparams(("parallel", "parallel")),
        name="rope",
    )(proj, proj, cos_t, sin_t)


def _softmax_pv(q, key_vals):
    scores = [_dot_nt(q, k) for k, _ in key_vals]
    m = functools.reduce(jnp.maximum, [s.max(axis=-1, keepdims=True) for s in scores])
    es = [jnp.exp(s - m) for s in scores]
    denom = functools.reduce(jnp.add, [e.sum(axis=-1, keepdims=True) for e in es])
    o = functools.reduce(jnp.add, [_dot(e.astype(BF16), v) for e, (_, v) in zip(es, key_vals)])
    return o / denom


DIFF_Q_ROWS = 256


def _diff_finish(o0, o1, lp, g, lam_init):
    lam = (jnp.exp(jnp.sum(lp[0:1] * lp[1:2], axis=-1, keepdims=True))
           - jnp.exp(jnp.sum(lp[2:3] * lp[3:4], axis=-1, keepdims=True)) + lam_init)
    o = o0 - lam * o1
    return o * lax.rsqrt(jnp.mean(o * o, axis=-1, keepdims=True) + EPS) * g * (1.0 - lam_init)


def _diff_lat_kernel(lam_ref, q_ref, kl_ref, vl_ref, kc_ref, vc_ref, g_ref, o_ref, *, lam_init):
    lane = lax.broadcasted_iota(jnp.int32, (1, LANES), 1)
    low = lane < B_DH
    tq = q_ref.shape[1]
    kl, vl = kl_ref[0], vl_ref[0].astype(BF16)
    kc, vc = kc_ref[0].astype(BF16), vc_ref[0].astype(BF16)
    units = [dict(r0=r0, mp=mp) for r0 in range(0, tq, DIFF_Q_ROWS) for mp in range(2)]

    def scores(un):
        q = q_ref[0, un['r0']:un['r0'] + DIFF_Q_ROWS, :]
        q = (jnp.where(low, q, 0) if un['mp'] == 0 else jnp.where(low, 0, q)).astype(BF16)
        un['s'] = (_dot_nt(q, kl), _dot_nt(q, kc))

    def softmax(un):
        s_l, s_c = un.pop('s')
        m = jnp.maximum(s_l.max(axis=-1, keepdims=True), s_c.max(axis=-1, keepdims=True))
        e_l, e_c = jnp.exp(s_l - m), jnp.exp(s_c - m)
        un['denom'] = e_l.sum(axis=-1, keepdims=True) + e_c.sum(axis=-1, keepdims=True)
        un['e'] = (e_l.astype(BF16), e_c.astype(BF16))

    def values(un):
        e_l, e_c = un.pop('e')
        un['o'] = (_dot(e_l, vl) + _dot(e_c, vc)) / un['denom']

    scores(units[0])
    scores(units[1])
    for k, un in enumerate(units):
        softmax(un)
        if k + 2 < len(units):
            scores(units[k + 2])
        values(un)
    for k in range(0, len(units), 2):
        r0 = units[k]['r0']
        o_ref[0, r0:r0 + DIFF_Q_ROWS, :] = _diff_finish(units[k]['o'], units[k + 1]['o'], lam_ref[...], g_ref[...],
                                                        lam_init)


def _diff_lat(lam, q_r, k_r, proj_lat, proj_ctx, subln_g, lam_init, tq):
    B, L, _ = proj_lat.shape
    Lc = proj_ctx.shape[1]
    return pl.pallas_call(
        functools.partial(_diff_lat_kernel, lam_init=lam_init),
        grid=(B, B_HEADS, L // tq),
        in_specs=[
            pl.BlockSpec((4, B_DH), lambda b, h, i: (0, 0)),
            pl.BlockSpec((1, tq, LANES), lambda b, h, i: (b, i, h)),
            pl.BlockSpec((1, L, LANES), lambda b, h, i: (b, 0, h)),
            pl.BlockSpec((1, L, LANES), lambda b, h, i: (b, 0, COL_B_V + h)),
            pl.BlockSpec((1, Lc, LANES), lambda b, h, i: (b, 0, COL_B_K + h)),
            pl.BlockSpec((1, Lc, LANES), lambda b, h, i: (b, 0, COL_B_V + h)),
            pl.BlockSpec((1, LANES), lambda b, h, i: (0, 0)),
        ],
        out_specs=pl.BlockSpec((1, tq, LANES), lambda b, h, i: (b, i, h)),
        out_shape=jax.ShapeDtypeStruct((B, L, B_HEADS * LANES), F32),
        compiler_params=_cparams(("parallel", "parallel", "parallel")),
        name="diff_lat",
    )(lam, q_r, k_r, proj_lat, proj_ctx, proj_ctx, subln_g)


def _diff_ctx_kernel(lam_ref, q_ref, k_ref, v_ref, g_ref, o_ref, *, lam_init):
    lane = lax.broadcasted_iota(jnp.int32, (1, LANES), 1)
    low = lane < B_DH
    q = q_ref[0] * (B_DH ** -0.5)
    kv = [(k_ref[0].astype(BF16), v_ref[0].astype(BF16))]
    o0 = _softmax_pv(jnp.where(low, q, 0).astype(BF16), kv)
    o1 = _softmax_pv(jnp.where(low, 0, q).astype(BF16), kv)
    o_ref[0] = _diff_finish(o0, o1, lam_ref[...], g_ref[...], lam_init)


def _diff_ctx(lam, proj_ctx, subln_g, lam_init):
    B, Lc, _ = proj_ctx.shape
    return pl.pallas_call(
        functools.partial(_diff_ctx_kernel, lam_init=lam_init),
        grid=(B, B_HEADS),
        in_specs=[
            pl.BlockSpec((4, B_DH), lambda b, h: (0, 0)),
            pl.BlockSpec((1, Lc, LANES), lambda b, h: (b, 0, COL_B_Q + h)),
            pl.BlockSpec((1, Lc, LANES), lambda b, h: (b, 0, COL_B_K + h)),
            pl.BlockSpec((1, Lc, LANES), lambda b, h: (b, 0, COL_B_V + h)),
            pl.BlockSpec((1, LANES), lambda b, h: (0, 0)),
        ],
        out_specs=pl.BlockSpec((1, Lc, LANES), lambda b, h: (b, 0, h)),
        out_shape=jax.ShapeDtypeStruct((B, Lc, B_HEADS * LANES), F32),
        compiler_params=_cparams(("parallel", "parallel")),
        name="diff_ctx",
    )(lam, proj_ctx, proj_ctx, proj_ctx, subln_g)


NA_ROWS_PER_STEP = 8
NA_KEYS = WIN_R * GRID_W


def _natten_bias(rpb):
    col = np.arange(GRID_W)
    col_start = np.clip(col - WIN_C // 2, 0, GRID_W - WIN_C)
    col_mask = (col[None, :] >= col_start[:, None]) & (col[None, :] < col_start[:, None] + WIN_C)
    dc = np.clip(col[None, :] - col[:, None], -(WIN_C - 1), WIN_C - 1) + WIN_C - 1
    bias = jnp.where(jnp.asarray(col_mask), rpb.astype(F32)[:, :, dc], NEG_INF)
    return jnp.concatenate([bias[:, :-1], bias[:, 1:]], axis=-1)


def _natten_kernel(q_ref, k_ref, v_ref, kc_ref, vc_ref, bias_ref, o_ref, *, rows):
    lane = lax.broadcasted_iota(jnp.int32, (1, LANES), 1)
    low = lane < C_DH
    kc = kc_ref[0].astype(BF16)
    vc = vc_ref[0].astype(BF16)
    units = []
    for rr in range(NA_ROWS_PER_STEP):
        r = pl.program_id(2) * NA_ROWS_PER_STEP + rr
        rs = jnp.clip(r - WIN_R // 2, 0, rows - WIN_R)
        cfg = r - rs
        start = pl.multiple_of(rs * GRID_W, GRID_W)
        q = q_ref[0, rr * GRID_W:(rr + 1) * GRID_W, :] * (C_DH ** -0.5)
        q2 = jnp.concatenate([jnp.where(low, q, 0), jnp.where(low, 0, q)], axis=0).astype(BF16)
        bias = jnp.concatenate(
            [jnp.concatenate([bias_ref[hh, WIN_R - 1 - cfg + j] for j in range(0, WIN_R, 2)], axis=-1)
             for hh in range(2)], axis=0)
        units.append(dict(q=q2, bias=bias, start=start))
    for un in units:
        kw = k_ref[0, pl.ds(un['start'], NA_KEYS), :].astype(BF16)
        un['s_lat'] = _dot_nt(un['q'], kw) + un['bias']
        un['s_ctx'] = _dot_nt(un['q'], kc)
    for un in units:
        s_lat, s_ctx = un['s_lat'], un['s_ctx']
        m = jnp.maximum(s_lat.max(axis=-1, keepdims=True), s_ctx.max(axis=-1, keepdims=True))
        e_lat, e_ctx = jnp.exp(s_lat - m), jnp.exp(s_ctx - m)
        un['denom'] = e_lat.sum(axis=-1, keepdims=True) + e_ctx.sum(axis=-1, keepdims=True)
        un['e_lat'], un['e_ctx'] = e_lat.astype(BF16), e_ctx.astype(BF16)
    for rr, un in enumerate(units):
        vw = v_ref[0, pl.ds(un['start'], NA_KEYS), :].astype(BF16)
        o = (_dot(un['e_lat'], vw) + _dot(un['e_ctx'], vc)) / un['denom']
        o_ref[0, rr * GRID_W:(rr + 1) * GRID_W, :] = jnp.where(low, o[:GRID_W], o[GRID_W:])


def _natten_lat(proj_lat, proj_ctx, bias):
    B, L, _ = proj_lat.shape
    Lc = proj_ctx.shape[1]
    rows = L // GRID_W
    HP = C_HEADS // 2
    tq = NA_ROWS_PER_STEP * GRID_W
    return pl.pallas_call(
        functools.partial(_natten_kernel, rows=rows),
        grid=(B, HP, rows // NA_ROWS_PER_STEP),
        in_specs=[
            pl.BlockSpec((1, tq, LANES), lambda b, h, i: (b, i, COL_C_Q + h)),
            pl.BlockSpec((1, L, LANES), lambda b, h, i: (b, 0, COL_C_K + h)),
            pl.BlockSpec((1, L, LANES), lambda b, h, i: (b, 0, COL_C_V + h)),
            pl.BlockSpec((1, Lc, LANES), lambda b, h, i: (b, 0, COL_C_K + h)),
            pl.BlockSpec((1, Lc, LANES), lambda b, h, i: (b, 0, COL_C_V + h)),
            pl.BlockSpec((2, 2 * WIN_R - 2, GRID_W, 2 * GRID_W), lambda b, h, i: (h, 0, 0, 0)),
        ],
        out_specs=pl.BlockSpec((1, tq, LANES), lambda b, h, i: (b, i, h)),
        out_shape=jax.ShapeDtypeStruct((B, L, HP * LANES), F32),
        compiler_params=_cparams(("parallel", "parallel", "parallel")),
        name="natten_lat",
    )(proj_lat, proj_lat, proj_lat, proj_ctx, proj_ctx, bias)


def _natten_ctx_kernel(q_ref, k_ref, v_ref, o_ref):
    lane = lax.broadcasted_iota(jnp.int32, (1, LANES), 1)
    low = lane < C_DH
    q = q_ref[0] * (C_DH ** -0.5)
    kv = [(k_ref[0].astype(BF16), v_ref[0].astype(BF16))]
    o0 = _softmax_pv(jnp.where(low, q, 0).astype(BF16), kv)
    o1 = _softmax_pv(jnp.where(low, 0, q).astype(BF16), kv)
    o_ref[0] = jnp.where(low, o0, o1)


def _natten_ctx(proj_ctx):
    B, Lc, _ = proj_ctx.shape
    HP = C_HEADS // 2
    return pl.pallas_call(
        _natten_ctx_kernel,
        grid=(B, HP),
        in_specs=[
            pl.BlockSpec((1, Lc, LANES), lambda b, h: (b, 0, COL_C_Q + h)),
            pl.BlockSpec((1, Lc, LANES), lambda b, h: (b, 0, COL_C_K + h)),
            pl.BlockSpec((1, Lc, LANES), lambda b, h: (b, 0, COL_C_V + h)),
        ],
        out_specs=pl.BlockSpec((1, Lc, LANES), lambda b, h: (b, 0, h)),
        out_shape=jax.ShapeDtypeStruct((B, Lc, HP * LANES), F32),
        compiler_params=_cparams(("parallel", "parallel")),
        name="natten_ctx",
    )(proj_ctx, proj_ctx, proj_ctx)


def _merge_kernel(ya_ref, yb_ref, yc_ref, g0_ref, g1_ref, g2_ref, wup_ref, wout_ref, x_ref, gt_ref, o_ref):
    acc = None
    for n, (y_ref, g_ref) in enumerate(((ya_ref, g0_ref), (yb_ref, g1_ref), (yc_ref, g2_ref))):
        up = _dot(y_ref[0].astype(BF16), wup_ref[n])
        t = jax.nn.sigmoid(g_ref[0]) * up
        acc = t if acc is None else acc + t
    r = _dot(acc.astype(BF16), wout_ref[...])
    o_ref[0] = x_ref[0] + gt_ref[0] * r


def _merge(ya, yb, yc, proj, w_up, w_out, x, gate, tm):
    B, L, D = x.shape
    gcol = COL_GATE * LANES // D
    yspec = pl.BlockSpec((1, tm, BRANCH_W), lambda b, i: (b, i, 0))
    return pl.pallas_call(
        _merge_kernel,
        grid=(B, L // tm),
        in_specs=[
            yspec, yspec, yspec,
            pl.BlockSpec((1, tm, D), lambda b, i: (b, i, gcol)),
            pl.BlockSpec((1, tm, D), lambda b, i: (b, i, gcol + 1)),
            pl.BlockSpec((1, tm, D), lambda b, i: (b, i, gcol + 2)),
            pl.BlockSpec((N_BRANCH, BRANCH_W, D), lambda b, i: (0, 0, 0)),
            pl.BlockSpec((D, D), lambda b, i: (0, 0)),
            pl.BlockSpec((1, tm, D), lambda b, i: (b, i, 0)),
            pl.BlockSpec((1, 1, D), lambda b, i: (b, 0, 0)),
        ],
        out_specs=pl.BlockSpec((1, tm, D), lambda b, i: (b, i, 0)),
        out_shape=jax.ShapeDtypeStruct((B, L, D), F32),
        compiler_params=_cparams(("parallel", "parallel")),
        name="merge",
    )(ya, yb, yc, proj, proj, proj, w_up, w_out, x, gate)


PEER_CAND = P_TOPK
PEER_A_PAD = 16
PEER_CAND_ROWS = PEER_A_PAD + 7 * 8 + 8


PEER_NO_RANK = float(N_KEYS)


def _extract_top(s, n, with_rank=False):
    vals = []
    rank = jnp.full(s.shape, PEER_NO_RANK, F32) if with_rank else None
    for r in range(n):
        m = jnp.max(s, axis=0, keepdims=True)
        vals.append(m)
        hit = s == m
        if with_rank:
            rank = jnp.where(hit, float(r), rank)
        s = jnp.where(hit, NEG_INF, s)
    return (vals, rank) if with_rank else vals


def _peerq_kernel(x_ref, sh_ref, sc_ref, g_ref, wq_ref, keys_ref,
                  xn_ref, r2_ref, e2_ref, n1_ref, e1_ref, ab_ref, cand_ref):
    x = x_ref[0]
    tm = x.shape[0]
    y = x * lax.rsqrt(jnp.mean(x * x, axis=-1, keepdims=True) + EPS) * g_ref[...]
    xn = (y * (1.0 + sc_ref[0]) + sh_ref[0]).astype(BF16)
    xn_ref[...] = xn
    q = _dot(xn, wq_ref[...]).astype(BF16)
    row8 = lax.broadcasted_iota(jnp.int32, (8, 1), 0)
    for h in range(P_HEADS):
        for tc in range(tm // LANES):
            qt = q[tc * LANES:(tc + 1) * LANES, :]
            s = []
            for p in range(2):
                hp = 2 * h + p
                st = _dot_nt(keys_ref[hp], qt[:, hp * P_DKH:(hp + 1) * P_DKH])
                s.append(st)
                if p == 0:
                    vals = _extract_top(st, PEER_CAND)
                else:
                    vals, rank2 = _extract_top(st, PEER_CAND, with_rank=True)
                for r, m in enumerate(vals):
                    ab_ref[p, r:r + 1, :] = m
            a_all, b_all = ab_ref[0], ab_ref[1]
            cand_ref[0:PEER_A_PAD, :] = a_all[0:1, :] + b_all
            for i in range(1, 8):
                n_i = PEER_CAND // (i + 1)
                cand_ref[PEER_A_PAD + 8 * (i - 1):PEER_A_PAD + 8 * i, :] = jnp.where(
                    row8 < n_i, a_all[i:i + 1, :] + b_all[0:8, :], NEG_INF)
            cand_ref[PEER_A_PAD + 56:PEER_A_PAD + 64, :] = a_all[8:16, :] + b_all[0:1, :]
            top = _extract_top(cand_ref[...], PEER_CAND)
            z = functools.reduce(jnp.add, [jnp.exp(v - top[0]) for v in top])
            thr = top[P_TOPK - 1]
            n1 = jnp.zeros((N_KEYS, LANES), F32)
            for i in range(P_TOPK):
                a_i = a_all[i:i + 1, :]
                cnt = jnp.sum(jnp.where(a_i + b_all >= thr, 1.0, 0.0), axis=0, keepdims=True)
                n1 = jnp.where(s[0] == a_i, cnt, n1)
            r2_ref[tc, h] = rank2.astype(BF16)
            e2_ref[tc, h] = (jnp.exp(s[1] - b_all[0:1, :]) / z).astype(BF16)
            n1_ref[tc, h] = n1
            e1_ref[tc, h] = jnp.exp(s[0] - a_all[0:1, :])


def _peerq(x, shift, scale, g, wq, keys, tm):
    B, L, D = x.shape
    T = B * L
    nb = L // tm
    tok_spec = pl.BlockSpec((tm // LANES, P_HEADS, N_KEYS, LANES), lambda b, i: (b * nb + i, 0, 0, 0))
    tok_shape = lambda dt: jax.ShapeDtypeStruct((T // LANES, P_HEADS, N_KEYS, LANES), dt)
    return pl.pallas_call(
        _peerq_kernel,
        grid=(B, nb),
        in_specs=[
            pl.BlockSpec((1, tm, D), lambda b, i: (b, i, 0)),
            pl.BlockSpec((1, 1, D), lambda b, i: (b, 0, 0)),
            pl.BlockSpec((1, 1, D), lambda b, i: (b, 0, 0)),
            pl.BlockSpec((1, D), lambda b, i: (0, 0)),
            pl.BlockSpec((D, 2 * P_HEADS * P_DKH), lambda b, i: (0, 0)),
            pl.BlockSpec((2 * P_HEADS, N_KEYS, P_DKH), lambda b, i: (0, 0, 0)),
        ],
        out_specs=[pl.BlockSpec((tm, D), lambda b, i: (b * nb + i, 0))] + [tok_spec] * 4,
        out_shape=[jax.ShapeDtypeStruct((T, D), BF16), tok_shape(BF16), tok_shape(BF16), tok_shape(F32),
                   tok_shape(F32)],
        scratch_shapes=[pltpu.VMEM((2, PEER_A_PAD, LANES), F32), pltpu.VMEM((PEER_CAND_ROWS, LANES), F32)],
        compiler_params=_cparams(("parallel", "parallel")),
        name="peer_query",
    )(x, shift, scale, g, wq, keys)


PEER_EC = 1024
PEER_TT = 1024
PEER_TSUB = 256
PEER_GATE_ROWS = 64


def _gelu(x):
    return 0.5 * x * (1.0 + lax.erf(x * math.sqrt(0.5)))


def _peer_kernel(xn_ref, u_ref, vt_ref, r2_ref, e2_ref, n1_ref, e1_ref, x_ref, gt_ref, fg_ref,
                 o_ref, acc_ref, act0_ref, act1_ref, pt0_ref, pt1_ref, *, final_norm):
    c = pl.program_id(1)
    act_refs, pt_refs = (act0_ref, act1_ref), (pt0_ref, pt1_ref)

    @pl.when(c == 0)
    def _():
        acc_ref[...] = jnp.zeros_like(acc_ref)

    TT = xn_ref.shape[0]
    tsub = min(PEER_TSUB, TT)
    n_sub = TT // tsub
    n_a = PEER_EC // N_KEYS

    def hidden(j):
        act_refs[j % 2][...] = _gelu(_dot_nt(u_ref[...], xn_ref[j * tsub:(j + 1) * tsub, :])).astype(BF16)

    def gated(j):
        act_ref, pt_ref = act_refs[j % 2], pt_refs[j % 2]
        for tl in range(tsub // LANES):
            tc = j * (tsub // LANES) + tl
            cols = slice(tl * LANES, (tl + 1) * LANES)
            bc = lambda row: jnp.concatenate(
                [jnp.broadcast_to(row, (2 * SUBLANES, LANES)).astype(BF16)] * (PEER_GATE_ROWS // (2 * SUBLANES)), axis=0)
            n1s = [[bc(n1_ref[tc, h, a:a + 1, :]) for h in range(P_HEADS)] for a in range(n_a)]
            e1s = [[bc(e1_ref[tc, h, a:a + 1, :]) for h in range(P_HEADS)] for a in range(n_a)]
            for b0 in range(0, N_KEYS, PEER_GATE_ROWS):
                brows = slice(b0, b0 + PEER_GATE_ROWS)
                gates = [None] * n_a
                for h in range(P_HEADS):
                    r2 = r2_ref[tc, h, brows, :]
                    e2 = e2_ref[tc, h, brows, :]
                    for a in range(n_a):
                        t = e1s[a][h] * jnp.minimum(jnp.maximum(n1s[a][h] - r2, 0.0), e2)
                        gates[a] = t if gates[a] is None else gates[a] + t
                for a in range(n_a):
                    rows = slice(a * N_KEYS + b0, a * N_KEYS + b0 + PEER_GATE_ROWS)
                    pt_ref[rows, cols] = gates[a] * act_ref[rows, cols]

    def project(j):
        acc_ref[:, j * tsub:(j + 1) * tsub] += _dot(vt_ref[...], pt_refs[j % 2][...])

    hidden(0)
    for j in range(n_sub):
        if j + 1 < n_sub:
            hidden(j + 1)
        gated(j)
        project(j)

    @pl.when(c == pl.num_programs(1) - 1)
    def _():
        ffn = acc_ref[...].T
        seg = TT // gt_ref.shape[0]
        for sq in range(gt_ref.shape[0]):
            rows = slice(sq * seg, (sq + 1) * seg)
            y = x_ref[rows, :] + gt_ref[sq] * ffn[rows, :]
            if final_norm:
                y = y * lax.rsqrt(jnp.mean(y * y, axis=-1, keepdims=True) + EPS) * fg_ref[...]
            o_ref[rows, :] = y


def _peer(xn, r2, e2, n1, e1, u, vt, x, gate, final_g, final_norm):
    B, L, D = x.shape
    T = B * L
    TT = min(PEER_TT, T)
    tsub = min(PEER_TSUB, TT)
    if TT <= L:
        per_b = L // TT
        gate_spec = pl.BlockSpec((1, 1, D), lambda i, c: (i // per_b, 0, 0))
    else:
        gate_spec = pl.BlockSpec((TT // L, 1, D), lambda i, c: (i, 0, 0))
    tok_spec = pl.BlockSpec((TT // LANES, P_HEADS, N_KEYS, LANES), lambda i, c: (i, 0, 0, 0))
    key_spec = pl.BlockSpec((TT // LANES, P_HEADS, PEER_EC // N_KEYS, LANES), lambda i, c: (i, 0, c, 0))
    out = pl.pallas_call(
        functools.partial(_peer_kernel, final_norm=final_norm),
        grid=(T // TT, N_EXPERTS // PEER_EC),
        in_specs=[
            pl.BlockSpec((TT, D), lambda i, c: (i, 0)),
            pl.BlockSpec((PEER_EC, D), lambda i, c: (c, 0)),
            pl.BlockSpec((D, PEER_EC), lambda i, c: (0, c)),
            tok_spec, tok_spec, key_spec, key_spec,
            pl.BlockSpec((TT, D), lambda i, c: (i, 0)),
            gate_spec,
            pl.BlockSpec((1, D), lambda i, c: (0, 0)),
        ],
        out_specs=pl.BlockSpec((TT, D), lambda i, c: (i, 0)),
        out_shape=jax.ShapeDtypeStruct((T, D), F32),
        scratch_shapes=[pltpu.VMEM((D, TT), F32),
                        pltpu.VMEM((PEER_EC, tsub), BF16), pltpu.VMEM((PEER_EC, tsub), BF16),
                        pltpu.VMEM((PEER_EC, tsub), BF16), pltpu.VMEM((PEER_EC, tsub), BF16)],
        compiler_params=_cparams(("parallel", "arbitrary")),
        name="peer_experts",
    )(xn, u, vt, r2, e2, n1, e1, x.reshape(T, D), gate, final_g)
    return out.reshape(B, L, D)


GDN_CONV_TILE = 512
GDN_PAD = 8


def _gdn_prep_kernel(q_ref, k_ref, v_ref, cwq_ref, cwk_ref, cwv_ref, qo_ref, ko_ref, vo_ref, pad_ref):
    Ls = q_ref.shape[1]
    T = min(GDN_CONV_TILE, Ls)
    zeros = jnp.zeros((GDN_PAD, LANES), F32)
    pad_ref[0:GDN_PAD, :] = zeros
    pad_ref[GDN_PAD + Ls:2 * GDN_PAD + Ls, :] = zeros
    for which, (x_ref, cw_ref, o_ref) in enumerate(((q_ref, cwq_ref, qo_ref), (k_ref, cwk_ref, ko_ref),
                                                    (v_ref, cwv_ref, vo_ref))):
        pad_ref[GDN_PAD:GDN_PAD + Ls, :] = x_ref[0]
        for t0 in range(0, Ls, T):
            acc = None
            for j in range(CONV_K):
                off = GDN_PAD - CONV_K // 2 + j + t0
                t = cw_ref[j:j + 1, :] * pad_ref[off:off + T, :]
                acc = t if acc is None else acc + t
            y = acc * jax.nn.sigmoid(acc)
            if which < 2:
                y = y * lax.rsqrt(jnp.sum(y * y, axis=-1, keepdims=True) + EPS)
            if which == 0:
                y = y * (A_DK ** -0.5)
            o_ref[0, t0:t0 + T, :] = y


def _gdn_prep(proj, conv_w):
    B, Ls, _ = proj.shape
    seq = lambda col: pl.BlockSpec((1, Ls, LANES), lambda b, h: (b, 0, col + h))
    cw = lambda col: pl.BlockSpec((CONV_K, LANES), lambda b, h: (0, col + h))
    out = pl.BlockSpec((1, Ls, LANES), lambda b, h: (b, 0, h))
    shape = jax.ShapeDtypeStruct((B, Ls, A_HEADS * LANES), F32)
    return pl.pallas_call(
        _gdn_prep_kernel,
        grid=(B, A_HEADS),
        in_specs=[seq(COL_A_Q), seq(COL_A_K), seq(COL_A_V), cw(0), cw(A_HEADS), cw(2 * A_HEADS)],
        out_specs=[out] * 3,
        out_shape=[shape] * 3,
        scratch_shapes=[pltpu.VMEM((Ls + 2 * GDN_PAD, LANES), F32)],
        compiler_params=_cparams(("parallel", "parallel")),
        name="gdn_prep",
    )(proj, proj, proj, conv_w, conv_w, conv_w)


def _softplus(x):
    return jnp.maximum(x, 0.0) + jnp.log1p(jnp.exp(-jnp.abs(x)))


GDN_GROUP = 8
GDN_INV_BLOCK = 16


def _gdn_prepass_group(h, chunks, nega_ref, dtb_ref, scr):
    mneg_scr, c_scr, qp_scr, dl_scr, o_scr = scr
    C = CHUNK
    row = lax.broadcasted_iota(jnp.int32, (C, C), 0)
    col = lax.broadcasted_iota(jnp.int32, (C, C), 1)
    lane = lax.broadcasted_iota(jnp.int32, (1, LANES), 1)
    eye = row == col
    chains = []
    for q, k, v, ba, row0, chunk_id in chunks:
        kb16 = k.astype(BF16)
        qk_raw = _dot_nt(q.astype(BF16), kb16)
        for d in range(2):
            before_eq_rc = (col <= row) if d == 0 else (col >= row)
            before_rc = (col < row) if d == 0 else (col > row)
            before_eq_cr = (row <= col) if d == 0 else (row >= col)
            bcol = jnp.sum(jnp.where(lane == d * A_HEADS + h, ba, 0.0), axis=-1, keepdims=True)
            acol = jnp.sum(jnp.where(lane == (2 + d) * A_HEADS + h, ba, 0.0), axis=-1, keepdims=True)
            beta = jax.nn.sigmoid(bcol)
            g = nega_ref[d, h] * _softplus(acol + dtb_ref[d, h])
            g_cols = jnp.broadcast_to(g, (C, C))
            gc_row = jnp.sum(jnp.where(before_eq_cr, g_cols, 0.0), axis=0, keepdims=True)
            g_row = jnp.sum(jnp.where(eye, g_cols, 0.0), axis=0, keepdims=True)
            gc_col = jnp.sum(jnp.where(before_eq_rc, jnp.broadcast_to(g_row, (C, C)), 0.0), axis=-1, keepdims=True)
            g_total = jnp.sum(g_row, axis=-1, keepdims=True)
            decay = jnp.exp(jnp.where(before_eq_rc, gc_col - gc_row, NEG_INF))
            kbeta = k * beta
            e_gc = jnp.exp(gc_col)
            chains.append(dict(
                d=d, row0=row0, chunk_id=chunk_id,
                n_pow=-jnp.where(before_rc, _dot_nt(kbeta.astype(BF16), kb16) * decay, 0.0),
                qk=jnp.where(before_eq_rc, qk_raw * decay, 0.0).astype(BF16),
                rhs=jnp.concatenate([v * beta, kbeta * e_gc], axis=-1).astype(BF16),
                kd=(k * jnp.exp(g_total - gc_col)).astype(BF16),
                qd=q * e_gc,
                dl=jnp.exp(g_total)))
    eye_f = jnp.where(eye, 1.0, 0.0)
    same_blk = (row // GDN_INV_BLOCK) == (col // GDN_INV_BLOCK)
    n_d = [jnp.where(same_blk, ch['n_pow'], 0.0) for ch in chains]
    a_o = [jnp.where(same_blk, 0.0, -ch['n_pow']) for ch in chains]
    d_inv = [eye_f + n for n in n_d]
    for _ in range(3):
        n_d = [_dot_split(n, n) for n in n_d]
        d_inv = [dd + _dot_split(dd, n) for dd, n in zip(d_inv, n_d)]
    b16 = lambda t: t.astype(BF16)
    ms = [_dot(b16(dd), b16(a)) for dd, a in zip(d_inv, a_o)]
    m2 = [_dot(b16(m), b16(m)) for m in ms]
    x1 = [(eye_f - m) + _dot(b16(eye_f - m), b16(mm)) for m, mm in zip(ms, m2)]
    invs = [_dot(b16(x), b16(dd)) for x, dd in zip(x1, d_inv)]
    uws = [_dot(inv.astype(BF16), ch['rhs']).astype(BF16) for inv, ch in zip(invs, chains)]
    kts = [_dot_tn(ch['kd'], uw) for ch, uw in zip(chains, uws)]
    qqs = [_dot(ch['qk'], uw) for ch, uw in zip(chains, uws)]
    for ch, kt, qq in zip(chains, kts, qqs):
        d, cid = ch['d'], ch['chunk_id']
        rows = pl.ds(ch['row0'], C)
        c_scr[d, cid] = kt[:, :A_DV]
        mneg_scr[d, cid] = (-kt[:, A_DV:]).astype(BF16)
        o_scr[d, rows, :] = qq[:, :A_DV]
        qp_scr[d, rows, :] = (ch['qd'] - qq[:, A_DV:]).astype(BF16)
        dl_scr[d, pl.ds(cid * 8, 8), :] = jnp.broadcast_to(ch['dl'], (8, LANES))


def _gdn_scan_step(states, scr, row0s, chunk_ids):
    mneg_scr, c_scr, qp_scr, dl_scr, o_scr = scr
    s16 = [S.astype(BF16) for S in states]
    upd = [_dot(mneg_scr[d, chunk_ids[d]], s16[d]) for d in range(2)]
    for d in range(2):
        rows = pl.ds(row0s[d], CHUNK)
        o_scr[d, rows, :] += _dot(qp_scr[d, rows, :], s16[d])
    return tuple(states[d] * dl_scr[d, pl.ds(chunk_ids[d] * 8, 1), :] + c_scr[d, chunk_ids[d]] + upd[d]
                 for d in range(2))


def _gdn_scan_kernel(nega_ref, dtb_ref, qc_ref, kc_ref, vc_ref, bac_ref, zc_ref, ql_ref, kl_ref, vl_ref, bal_ref,
                     zl_ref, ng_ref, yl_ref, yc_ref, mneg_scr, c_scr, qp_scr, dl_scr, o_scr):
    h = pl.program_id(1)
    Lc, L = qc_ref.shape[1], ql_ref.shape[1]
    nc, nl = Lc // CHUNK, L // CHUNK
    scr = (mneg_scr, c_scr, qp_scr, dl_scr, o_scr)

    for c0 in range(0, nc, GDN_GROUP):
        _gdn_prepass_group(h, [(qc_ref[0, c * CHUNK:(c + 1) * CHUNK, :], kc_ref[0, c * CHUNK:(c + 1) * CHUNK, :],
                                vc_ref[0, c * CHUNK:(c + 1) * CHUNK, :], bac_ref[0, c * CHUNK:(c + 1) * CHUNK, :],
                                c * CHUNK, c) for c in range(c0, min(c0 + GDN_GROUP, nc))],
                           nega_ref, dtb_ref, scr)

    def pre_body(grp, carry):
        chunks = []
        for j in range(GDN_GROUP):
            c = grp * GDN_GROUP + j
            r = pl.multiple_of(c * CHUNK, CHUNK)
            rows = pl.ds(r, CHUNK)
            chunks.append((ql_ref[0, rows, :], kl_ref[0, rows, :], vl_ref[0, rows, :], bal_ref[0, rows, :],
                           Lc + r, nc + c))
        _gdn_prepass_group(h, chunks, nega_ref, dtb_ref, scr)
        return carry

    lax.fori_loop(0, nl // GDN_GROUP, pre_body, 0)

    states = (jnp.zeros((A_DK, A_DV), F32), jnp.zeros((A_DK, A_DV), F32))
    for s in range(nc):
        states = _gdn_scan_step(states, scr, (s * CHUNK, (nc - 1 - s) * CHUNK), (s, nc - 1 - s))

    def scan_body(s, states):
        cf = s
        cb = nl - 1 - s
        return _gdn_scan_step(states, scr,
                              (pl.multiple_of(Lc + cf * CHUNK, CHUNK), pl.multiple_of(Lc + cb * CHUNK, CHUNK)),
                              (nc + cf, nc + cb))

    lax.fori_loop(0, nl, scan_body, states)

    def finish(z_ref, y_ref, base, n):
        T = min(GDN_CONV_TILE, n)
        for t0 in range(0, n, T):
            o = o_scr[0, base + t0:base + t0 + T, :] + o_scr[1, base + t0:base + t0 + T, :]
            y = o * lax.rsqrt(jnp.mean(o * o, axis=-1, keepdims=True) + EPS) * ng_ref[...]
            z = z_ref[0, t0:t0 + T, :]
            y_ref[0, t0:t0 + T, :] = y * (z * jax.nn.sigmoid(z))

    finish(zl_ref, yl_ref, Lc, L)
    finish(zc_ref, yc_ref, 0, Lc)


def _gdn_scan(nega, dtb, qkv_ctx, ba_ctx, proj_ctx, qkv_lat, ba_lat, proj_lat, norm_g):
    B, L, _ = proj_lat.shape
    Lc = proj_ctx.shape[1]
    Lt = L + Lc
    head = lambda n: pl.BlockSpec((1, n, LANES), lambda b, h: (b, 0, h))
    full = lambda n: pl.BlockSpec((1, n, LANES), lambda b, h: (b, 0, 0))
    zcol = lambda n: pl.BlockSpec((1, n, LANES), lambda b, h: (b, 0, COL_A_Z + h))
    smem = pl.BlockSpec(memory_space=pltpu.SMEM)
    return pl.pallas_call(
        _gdn_scan_kernel,
        grid=(B, A_HEADS),
        in_specs=[smem, smem,
                  head(Lc), head(Lc), head(Lc), full(Lc), zcol(Lc),
                  head(L), head(L), head(L), full(L), zcol(L),
                  pl.BlockSpec((1, LANES), lambda b, h: (0, 0))],
        out_specs=[head(L), head(Lc)],
        out_shape=[jax.ShapeDtypeStruct((B, L, A_HEADS * LANES), F32),
                   jax.ShapeDtypeStruct((B, Lc, A_HEADS * LANES), F32)],
        scratch_shapes=[
            pltpu.VMEM((2, Lt // CHUNK, A_DK, A_DV), BF16),
            pltpu.VMEM((2, Lt // CHUNK, A_DK, A_DV), F32),
            pltpu.VMEM((2, Lt, A_DK), BF16),
            pltpu.VMEM((2, Lt // CHUNK * 8, LANES), F32),
            pltpu.VMEM((2, Lt, A_DV), F32),
        ],
        compiler_params=_cparams(("parallel", "parallel")),
        name="gdn_scan",
    )(nega, dtb, *qkv_ctx, ba_ctx, proj_ctx, *qkv_lat, ba_lat, proj_lat, norm_g)


def _gdn_pallas(proj_lat, ba_lat, proj_ctx, ba_ctx, conv_w, a_log, dt_bias, norm_g):
    qkv_lat = _gdn_prep(proj_lat, conv_w)
    qkv_ctx = _gdn_prep(proj_ctx, conv_w)
    return _gdn_scan(-jnp.exp(a_log), dt_bias, qkv_ctx, ba_ctx, proj_ctx, qkv_lat, ba_lat, proj_lat, norm_g[None, :])


def _layer(i, x, xc, c, c_ctx, p, cos_t, sin_t, ctx_out, final_g, final_norm):
    B, L, D = x.shape
    Lc = xc.shape[1]
    n_rows = -(-(B + 1) // SUBLANES) * SUBLANES
    cond = jnp.concatenate([c, c_ctx[None, :], jnp.zeros((n_rows - B - 1, D), F32)], axis=0)
    mod = _adaln(cond, p['ada_w'], p['ada_b'])
    mod_lat = mod[:B, None, :]
    mod_ctx = jnp.broadcast_to(mod[B][None, None, :], (B, 1, 6 * D))
    sh1, sc1, gt1, sh2, sc2, gt2 = jnp.split(mod_lat, 6, axis=-1)
    csh1, csc1, cgt1, csh2, csc2, cgt2 = jnp.split(mod_ctx, 6, axis=-1)
    lam_init = 0.8 - 0.6 * math.exp(-0.3 * i)

    w_in = p['w_in']
    n_a = 4 * A_HEADS * A_DK
    w_main = jnp.concatenate([w_in[:, :n_a], w_in[:, n_a + 4 * A_HEADS:]], axis=1).astype(BF16)
    w_ba = jnp.pad(w_in[:, n_a:n_a + 4 * A_HEADS], ((0, 0), (0, LANES - 4 * A_HEADS))).astype(BF16)
    n1 = p['norm1_g'][None, :]
    proj_lat, ba_lat = _inproj(x, sh1, sc1, n1, w_main, w_ba, min(INPROJ_ROWS, L))
    proj_ctx, ba_ctx = _inproj(xc, csh1, csc1, n1, w_main, w_ba, Lc)

    ya, ya_c = _gdn_pallas(proj_lat, ba_lat, proj_ctx, ba_ctx, p['gdn_conv'], p['gdn_a_log'], p['gdn_dt_bias'],
                           p['gdn_norm_g'])

    lp = p['diff_lambda']
    q_r, k_r = _rope(proj_lat, cos_t, sin_t, min(ROPE_ROWS, L))
    subln = p['diff_subln_g'][None, :]
    yb = _diff_lat(lp, q_r, k_r, proj_lat, proj_ctx, subln, lam_init, min(DIFF_Q_TILE, L))
    yb_c = _diff_ctx(lp, proj_ctx, subln, lam_init) if ctx_out else None

    yc = _natten_lat(proj_lat, proj_ctx, _natten_bias(p['na_rpb']))
    yc_c = _natten_ctx(proj_ctx) if ctx_out else None

    w_up = p['w_up'].astype(BF16)
    w_out = p['w_out'].astype(BF16)
    wq = p['peer_wq'].astype(BF16)
    keys = p['peer_keys'].reshape(2 * P_HEADS, N_KEYS, P_DKH).astype(BF16)
    u = p['peer_u'].astype(BF16)
    vt = p['peer_v'].T.astype(BF16)
    n2 = p['norm2_g'][None, :]
    fg = final_g[None, :]

    x = _merge(ya, yb, yc, proj_lat, w_up, w_out, x, gt1, min(MERGE_ROWS, L))
    pq = _peerq(x, sh2, sc2, n2, wq, keys, min(PEERQ_ROWS, L))
    x = _peer(*pq, u, vt, x, gt2, fg, final_norm)
    if ctx_out:
        xc = _merge(ya_c, yb_c, yc_c, proj_ctx, w_up, w_out, xc, cgt1, Lc)
        pq = _peerq(xc, csh2, csc2, n2, wq, keys, Lc)
        xc = _peer(*pq, u, vt, xc, cgt2, fg, False)
    return x, xc


def kernel(x, c, ctx, c_ctx, norm1_g, norm2_g, ada_w, ada_b, w_in, gdn_conv, gdn_a_log, gdn_dt_bias, gdn_norm_g,
           diff_lambda, diff_subln_g, na_rpb, w_up, w_out, peer_wq, peer_keys, peer_u, peer_v, final_g):
    cos_t, sin_t = _rope_tables(x.shape[1])
    xc = ctx
    for i in range(DEPTH):
        p = dict(norm1_g=norm1_g[i], norm2_g=norm2_g[i], ada_w=ada_w[i], ada_b=ada_b[i], w_in=w_in[i],
                 gdn_conv=gdn_conv[i], gdn_a_log=gdn_a_log[i], gdn_dt_bias=gdn_dt_bias[i], gdn_norm_g=gdn_norm_g[i],
                 diff_lambda=diff_lambda[i], diff_subln_g=diff_subln_g[i], na_rpb=na_rpb[i], w_up=w_up[i],
                 w_out=w_out[i], peer_wq=peer_wq[i], peer_keys=peer_keys[i], peer_u=peer_u[i], peer_v=peer_v[i])
        x, xc = _layer(i, x, xc, c, c_ctx, p, cos_t, sin_t, i < DEPTH - 1, final_g, i == DEPTH - 1)
    return x
```
